```python
import math
import jax, jax.numpy as jnp
from jax import lax
import numpy as np

D_MODEL = 2048
BATCH = 16
SEQ = 256
DEPTH = 2
DEC_BATCH = 8
DEC_SEQ = 2048
PAST_LEN = 256

GRID_W = 64
A_HEADS = 8
A_QK_DIM = 64
A_HEAD_DIM = 2 * A_QK_DIM
B_HEADS = 8
B_HEAD_DIM = 128
A_WIDTH = A_HEADS * A_HEAD_DIM
B_WIDTH = B_HEADS * B_HEAD_DIM
MIX_WIDTH = A_WIDTH + B_WIDTH
NA_WIN_R = 8
NA_WIN_C = 16
FOURIER_GROUPS = 4
FOURIER_GROUP_DIM = D_MODEL // FOURIER_GROUPS
N_EXPERTS = 16
N_EXPERT_GROUPS = 4
EXPERTS_PER_GROUP = N_EXPERTS // N_EXPERT_GROUPS
TOP_K = 2
D_EXPERT = 1408
EXPERT_BLOCK = 256
Q_BLOCK = 128
ROPE_THETA = 10000.0
NORM_EPS = 1e-6
SUBLN_EPS = 1e-5

kernel_name = "hybrid_diff_na_fourier_moe_step"


def rmsnorm(x, g, eps=NORM_EPS):
    x32 = x.astype(jnp.float32)
    y = x32 * lax.rsqrt(jnp.mean(x32 * x32, axis=-1, keepdims=True) + eps)
    return (y * g.astype(jnp.float32)).astype(x.dtype)


def modulate(cond, w_ada, b_ada):
    m = jax.nn.silu(cond) @ w_ada + b_ada
    return jnp.split(m[:, None, :], 6, axis=-1)


def rope_1d(x, pos):
    half = x.shape[-1] // 2
    freqs = ROPE_THETA ** (-jnp.arange(half, dtype=jnp.float32) / half)
    ang = pos[:, None] * freqs[None, :]
    cos = jnp.cos(ang)[None, :, None, :].astype(x.dtype)
    sin = jnp.sin(ang)[None, :, None, :].astype(x.dtype)
    x1, x2 = x[..., :half], x[..., half:]
    return jnp.concatenate([x1 * cos - x2 * sin, x2 * cos + x1 * sin], axis=-1)


def axial_rope(x):
    L = x.shape[1]
    t = jnp.arange(L)
    row = (t // GRID_W).astype(jnp.float32)
    col = (t % GRID_W).astype(jnp.float32)
    half = x.shape[-1] // 2
    return jnp.concatenate([rope_1d(x[..., :half], row), rope_1d(x[..., half:], col)], axis=-1)


def axial_rope_diff_heads(t):
    B, L, H, dh = t.shape
    return axial_rope(t.reshape(B, L, H * 2, A_QK_DIM)).reshape(B, L, H, dh)


def over_query_blocks(fn, q):
    B, L = q.shape[:2]
    nb = L // Q_BLOCK
    qb = jnp.moveaxis(q.reshape((B, nb, Q_BLOCK) + q.shape[2:]), 1, 0)
    o = lax.map(fn, qb)
    return jnp.moveaxis(o, 0, 1).reshape((B, L) + o.shape[3:])


def diff_lambda(lq1, lk1, lq2, lk2, lam_init):
    s1 = jnp.sum((lq1 * lk1).astype(jnp.float32))
    s2 = jnp.sum((lq2 * lk2).astype(jnp.float32))
    return jnp.exp(s1) - jnp.exp(s2) + lam_init


def diff_attention(q, k, v, lam):
    B, Lk, H, _ = k.shape
    k2 = k.reshape(B, Lk, H, 2, A_QK_DIM)
    scale = A_QK_DIM ** -0.5

    def block(qb):
        qb2 = qb.reshape(qb.shape[0], qb.shape[1], H, 2, A_QK_DIM)
        s = jnp.einsum('bqhnd,bkhnd->bhnqk', qb2, k2).astype(jnp.float32) * scale
        p = jax.nn.softmax(s, axis=-1)
        p = p[:, :, 0] - lam * p[:, :, 1]
        return jnp.einsum('bhqk,bkhd->bqhd', p.astype(v.dtype), v)

    return over_query_blocks(block, q)


def softmax_attention(q, k, v):
    scale = q.shape[-1] ** -0.5

    def block(qb):
        s = jnp.einsum('bqhd,bkhd->bhqk', qb, k).astype(jnp.float32) * scale
        p = jax.nn.softmax(s, axis=-1).astype(v.dtype)
        return jnp.einsum('bhqk,bkhd->bqhd', p, v)

    return over_query_blocks(block, q)


def neighbourhood_attention(q, k, v, k_ctx, v_ctx, rpb):
    B, N, H, d = q.shape
    rows = N // GRID_W
    kr = min(NA_WIN_R, rows)
    qg = q.reshape(B, rows, GRID_W, H, d)
    kg = k.reshape(B, rows, GRID_W, H, d)
    vg = v.reshape(B, rows, GRID_W, H, d)
    qc = jnp.arange(GRID_W)
    kc = jnp.arange(GRID_W)
    cs = jnp.clip(qc - NA_WIN_C // 2, 0, GRID_W - NA_WIN_C)
    col_mask = (kc[None, :] >= cs[:, None]) & (kc[None, :] < cs[:, None] + NA_WIN_C)
    col_idx = jnp.clip(kc[None, :] - qc[:, None] + NA_WIN_C - 1, 0, 2 * NA_WIN_C - 2)
    scale = d ** -0.5
    nwin = kr * GRID_W

    def row(args):
        r, q_r = args
        rs = jnp.clip(r - kr // 2, 0, rows - kr)
        kb = lax.dynamic_slice_in_dim(kg, rs, kr, axis=1)
        vb = lax.dynamic_slice_in_dim(vg, rs, kr, axis=1)
        row_idx = rs + jnp.arange(kr) - r + NA_WIN_R - 1
        bias = rpb[:, row_idx[:, None, None], col_idx[None, :, :]]
        s = jnp.einsum('bqhd,bjkhd->bhqjk', q_r, kb).astype(jnp.float32) * scale
        s = s + jnp.transpose(bias, (0, 2, 1, 3))[None].astype(jnp.float32)
        s = jnp.where(col_mask[None, None, :, None, :], s, -jnp.inf)
        s_ctx = jnp.einsum('bqhd,bchd->bhqc', q_r, k_ctx).astype(jnp.float32) * scale
        p = jax.nn.softmax(jnp.concatenate([s.reshape(B, H, GRID_W, nwin), s_ctx], axis=-1), axis=-1)
        p = p.astype(v.dtype)
        p_win = p[..., :nwin].reshape(B, H, GRID_W, kr, GRID_W)
        return (jnp.einsum('bhqjk,bjkhd->bqhd', p_win, vb)
                + jnp.einsum('bhqc,bchd->bqhd', p[..., nwin:], v_ctx))

    out = lax.map(row, (jnp.arange(rows), jnp.moveaxis(qg, 1, 0)))
    return jnp.moveaxis(out, 0, 1).reshape(B, N, H, d)


def fourier_mix(h, w_out):
    B, L, D = h.shape
    hg = h.astype(jnp.float32).reshape(B, L, FOURIER_GROUPS, FOURIER_GROUP_DIM)
    f = jnp.fft.fft2(hg, axes=(1, 3), norm='ortho').real.astype(h.dtype).reshape(B, L, D)
    return f @ w_out


def route(x, w_router, router_bias):
    T = x.shape[0]
    scores = jax.nn.sigmoid((x @ w_router).astype(jnp.float32))
    sel = scores + router_bias.astype(jnp.float32)
    gscore = lax.top_k(sel.reshape(T, N_EXPERT_GROUPS, EXPERTS_PER_GROUP), 2)[0].sum(-1)
    g = jnp.argmax(gscore, axis=-1)
    in_group = (jnp.arange(N_EXPERTS) // EXPERTS_PER_GROUP)[None, :] == g[:, None]
    _, idx = lax.top_k(jnp.where(in_group, sel, -jnp.inf), TOP_K)
    w = jnp.take_along_axis(scores, idx, axis=-1)
    return idx, w / jnp.sum(w, axis=-1, keepdims=True)


def moe(h, w_router, router_bias, w_gate, w_up, w_down):
    B, L, D = h.shape
    T = B * L
    x = h.reshape(T, D)
    idx, w = route(x, w_router, router_bias)
    flat_e = idx.reshape(-1)
    flat_tok = jnp.repeat(jnp.arange(T, dtype=jnp.int32), TOP_K)
    flat_w = w.reshape(-1)
    order = jnp.argsort(flat_e)
    e_sorted, tok_sorted, w_sorted = flat_e[order], flat_tok[order], flat_w[order]
    counts = jnp.bincount(flat_e, length=N_EXPERTS)
    padded = (counts + EXPERT_BLOCK - 1) // EXPERT_BLOCK * EXPERT_BLOCK
    start = jnp.cumsum(counts) - counts
    pad_end = jnp.cumsum(padded)
    pad_start = pad_end - padded
    dest = pad_start[e_sorted] + jnp.arange(T * TOP_K) - start[e_sorted]
    n_blocks = -(-(T * TOP_K + N_EXPERTS * (EXPERT_BLOCK - 1)) // EXPERT_BLOCK)
    P = n_blocks * EXPERT_BLOCK
    row_tok = jnp.full((P,), T, jnp.int32).at[dest].set(tok_sorted)
    row_w = jnp.zeros((P,), w.dtype).at[dest].set(w_sorted)
    block_e = jnp.minimum(jnp.searchsorted(pad_end, jnp.arange(n_blocks) * EXPERT_BLOCK, side='right'),
                          N_EXPERTS - 1)
    x_rows = jnp.concatenate([x, jnp.zeros((1, D), x.dtype)], axis=0)[row_tok]
    x_rows = x_rows.reshape(n_blocks, EXPERT_BLOCK, D)

    def expert_block(args):
        xb, e = args
        return (jax.nn.silu(xb @ w_gate[e]) * (xb @ w_up[e])) @ w_down[e]

    y = lax.map(expert_block, (x_rows, block_e)).reshape(P, D)
    y = y * row_w[:, None].astype(y.dtype)
    out = jax.ops.segment_sum(y, row_tok, num_segments=T + 1)[:T]
    return out.reshape(B, L, D)


def even_projection(h, w_in):
    B, L, _ = h.shape
    sizes = [A_WIDTH, A_WIDTH, A_WIDTH, B_WIDTH, B_WIDTH, B_WIDTH]
    parts = jnp.split(h @ w_in, list(np.cumsum(sizes)[:-1]), axis=-1)
    qa, ka, va = (t.reshape(B, L, A_HEADS, A_HEAD_DIM) for t in parts[:3])
    qb, kb, vb = (t.reshape(B, L, B_HEADS, B_HEAD_DIM) for t in parts[3:])
    return qa, ka, va, qb, kb, vb


def trunk(x, cond, ctx_cache, w_ada, b_ada, norm1, norm2, w_in, w_mix_out, lambda_q1, lambda_k1,
          lambda_q2, lambda_k2, subln_gain, na_rel_bias, w_fourier_out, w_router, router_bias,
          w_exp_gate, w_exp_up, w_exp_down):
    latent = ctx_cache is not None
    new_cache = ([], [], [], [])
    for l in range(DEPTH):
        sh1, sc1, g1, sh2, sc2, g2 = modulate(cond, w_ada[l], b_ada[l])
        h = rmsnorm(x, norm1[l]) * (1 + sc1) + sh1
        if l % 2 == 0:
            j = l // 2
            B, L, _ = h.shape
            qa, ka, va, qb, kb, vb = even_projection(h, w_in[j])
            lam_init = 0.8 - 0.6 * math.exp(-0.3 * l)
            lam = diff_lambda(lambda_q1[j], lambda_k1[j], lambda_q2[j], lambda_k2[j], lam_init)
            if latent:
                ck_a, cv_a, ck_b, cv_b = (t[:, j] for t in ctx_cache)
                qa_r = axial_rope_diff_heads(qa)
                ka_r = axial_rope_diff_heads(ka)
                a = diff_attention(qa_r, jnp.concatenate([ck_a, ka_r], axis=1),
                                   jnp.concatenate([cv_a, va], axis=1), lam)
                b = neighbourhood_attention(qb, kb, vb, ck_b, cv_b, na_rel_bias[j])
            else:
                a = diff_attention(qa, ka, va, lam)
                b = softmax_attention(qb, kb, vb)
                for lst, t in zip(new_cache, (ka, va, kb, vb)):
                    lst.append(t)
            a = rmsnorm(a, subln_gain[j], SUBLN_EPS) * (1.0 - lam_init)
            mix = jnp.concatenate([a.reshape(B, L, A_WIDTH), b.reshape(B, L, B_WIDTH)], axis=-1) @ w_mix_out[j]
        else:
            mix = fourier_mix(h, w_fourier_out[l // 2])
        x = x + g1 * mix
        h = rmsnorm(x, norm2[l]) * (1 + sc2) + sh2
        x = x + g2 * moe(h, w_router, router_bias, w_exp_gate[l], w_exp_up[l], w_exp_down[l])
    stacked = None if latent else tuple(jnp.stack(lst, axis=1) for lst in new_cache)
    return x, stacked


def setup_inputs(seed: int = 0) -> dict:
    key = jax.random.key(seed)
    ks = jax.random.split(key, 32)
    n_even = (DEPTH + 1) // 2
    n_odd = DEPTH // 2
    f32 = jnp.float32
    nrm = lambda k, s, sc: jax.random.normal(k, s, f32) * sc
    return {
        'x_prompt': nrm(ks[0], (BATCH, SEQ, D_MODEL), 1.0),
        'x_sample': nrm(ks[1], (DEC_BATCH, DEC_SEQ, D_MODEL), 1.0),
        'cache_a_k': nrm(ks[2], (DEC_BATCH, n_even, PAST_LEN, A_HEADS, A_HEAD_DIM), 1.0),
        'cache_a_v': nrm(ks[3], (DEC_BATCH, n_even, PAST_LEN, A_HEADS, A_HEAD_DIM), 1.0),
        'cache_b_k': nrm(ks[4], (DEC_BATCH, n_even, PAST_LEN, B_HEADS, B_HEAD_DIM), 1.0),
        'cache_b_v': nrm(ks[5], (DEC_BATCH, n_even, PAST_LEN, B_HEADS, B_HEAD_DIM), 1.0),
        'c': nrm(ks[6], (DEC_BATCH, D_MODEL), 1.0),
        'c_ctx': nrm(ks[7], (D_MODEL,), 1.0),
        'w_ada': nrm(ks[8], (DEPTH, D_MODEL, 6 * D_MODEL), 0.5 * D_MODEL ** -0.5),
        'b_ada': nrm(ks[9], (DEPTH, 6 * D_MODEL), 0.01),
        'norm1': 1.0 + nrm(ks[10], (DEPTH, D_MODEL), 0.01),
        'norm2': 1.0 + nrm(ks[11], (DEPTH, D_MODEL), 0.01),
        'final_norm': 1.0 + nrm(ks[12], (D_MODEL,), 0.01),
        'w_in': nrm(ks[13], (n_even, D_MODEL, 3 * A_WIDTH + 3 * B_WIDTH), D_MODEL ** -0.5),
        'w_mix_out': nrm(ks[14], (n_even, MIX_WIDTH, D_MODEL), MIX_WIDTH ** -0.5),
        'lambda_q1': nrm(ks[15], (n_even, A_QK_DIM), 0.1),
        'lambda_k1': nrm(ks[16], (n_even, A_QK_DIM), 0.1),
        'lambda_q2': nrm(ks[17], (n_even, A_QK_DIM), 0.1),
        'lambda_k2': nrm(ks[18], (n_even, A_QK_DIM), 0.1),
        'subln_gain': 1.0 + nrm(ks[19], (n_even, A_HEAD_DIM), 0.01),
        'na_rel_bias': nrm(ks[20], (n_even, B_HEADS, 2 * NA_WIN_R - 1, 2 * NA_WIN_C - 1), 0.02),
        'w_fourier_out': nrm(ks[21], (n_odd, D_MODEL, D_MODEL), D_MODEL ** -0.5),
        'w_router': nrm(ks[22], (D_MODEL, N_EXPERTS), D_MODEL ** -0.5),
        'router_bias': nrm(ks[23], (N_EXPERTS,), 0.01),
        'w_exp_gate': nrm(ks[24], (DEPTH, N_EXPERTS, D_MODEL, D_EXPERT), D_MODEL ** -0.5),
        'w_exp_up': nrm(ks[25], (DEPTH, N_EXPERTS, D_MODEL, D_EXPERT), D_MODEL ** -0.5),
        'w_exp_down': nrm(ks[26], (DEPTH, N_EXPERTS, D_EXPERT, D_MODEL), D_EXPERT ** -0.5),
    }


def reference(x_prompt, x_sample, cache_a_k, cache_a_v, cache_b_k, cache_b_v, c, c_ctx, w_ada, b_ada,
              norm1, norm2, final_norm, w_in, w_mix_out, lambda_q1, lambda_k1, lambda_q2, lambda_k2,
              subln_gain, na_rel_bias, w_fourier_out, w_router, router_bias, w_exp_gate, w_exp_up,
              w_exp_down):
    x_p, new_cache = trunk(x_prompt, c_ctx[None, :], None, w_ada, b_ada, norm1, norm2, w_in, w_mix_out,
                           lambda_q1, lambda_k1, lambda_q2, lambda_k2, subln_gain, na_rel_bias,
                           w_fourier_out, w_router, router_bias, w_exp_gate, w_exp_up, w_exp_down)
    new_a_k, new_a_v, new_b_k, new_b_v = new_cache
    x_s, _ = trunk(x_sample, c, (cache_a_k, cache_a_v, cache_b_k, cache_b_v), w_ada, b_ada, norm1, norm2,
                   w_in, w_mix_out, lambda_q1, lambda_k1, lambda_q2, lambda_k2, subln_gain, na_rel_bias,
                   w_fourier_out, w_router, router_bias, w_exp_gate, w_exp_up, w_exp_down)
    y_prompt = rmsnorm(x_p, final_norm)
    y_sample = rmsnorm(x_s, final_norm)
    return (y_prompt, y_sample, new_a_k, new_a_v, new_b_k, new_b_v)
```

```python
import functools
import math
from typing import NamedTuple

import numpy as np
import jax
import jax.numpy as jnp
from jax import lax
from jax.experimental import pallas as pl
from jax.experimental.pallas import tpu as pltpu

F32 = jnp.float32
BF16 = jnp.bfloat16

LANES = 128
NEG_BIG = -1e30
VMEM_LIMIT = 56 * 1024 * 1024


class Cfg(NamedTuple):
    d_model: int = 2048
    batch: int = 16
    seq: int = 256
    depth: int = 2
    dec_batch: int = 8
    dec_seq: int = 2048
    past_len: int = 256
    grid_w: int = 64
    a_heads: int = 8
    a_qk: int = 64
    b_heads: int = 8
    b_dim: int = 128
    na_win_r: int = 8
    na_win_c: int = 16
    f_groups: int = 4
    n_experts: int = 16
    n_groups: int = 4
    d_expert: int = 1408
    rope_theta: float = 10000.0
    norm_eps: float = 1e-6
    subln_eps: float = 1e-5
    row_tile: int = 512
    q_tile: int = 256
    e_block: int = 256

    @property
    def a_dim(self):
        return 2 * self.a_qk

    @property
    def a_width(self):
        return self.a_heads * self.a_dim

    @property
    def b_width(self):
        return self.b_heads * self.b_dim

    @property
    def n_p(self):
        return self.batch * self.seq

    @property
    def n_s(self):
        return self.dec_batch * self.dec_seq

    @property
    def n_tok(self):
        return self.n_p + self.n_s

    @property
    def tm(self):
        return min(self.row_tile, self.n_p, self.dec_seq)

    @property
    def n_mod(self):
        return 1 + self.dec_batch


def _params(sem):
    return pltpu.CompilerParams(dimension_semantics=sem, vmem_limit_bytes=VMEM_LIMIT)


def _norm_mod(x, g, sh, sc, eps):
    ms = jnp.mean(x * x, axis=-1, keepdims=True)
    y = x * lax.rsqrt(ms + eps)
    return (y * g) * (1.0 + sc) + sh


def _silu(x):
    return x * jax.nn.sigmoid(x)


def _merged_mod_index(cfg, tm):
    def f(i):
        r = i * tm
        return jnp.where(r < cfg.n_p, 0, 1 + (r - cfg.n_p) // cfg.dec_seq)
    return f


def _ada_kernel(cond_ref, w_ref, b_ref, o_ref):
    s = _silu(cond_ref[...])
    s_hi = s.astype(BF16)
    s_lo = (s - s_hi.astype(F32)).astype(BF16)
    lhs = jnp.concatenate([s_hi, s_lo], axis=0)
    r = jnp.dot(lhs, w_ref[0].astype(BF16), preferred_element_type=F32)
    n = s.shape[0]
    o_ref[0] = r[:n] + r[n:] + b_ref[0]


def _modulation(cfg, cond, w_ada, b_ada):
    d = cfg.d_model
    r = cond.shape[0]
    tn = math.gcd(1024, 6 * d)
    return pl.pallas_call(
        _ada_kernel,
        grid=(cfg.depth, 6 * d // tn),
        in_specs=[
            pl.BlockSpec((r, d), lambda l, j: (0, 0)),
            pl.BlockSpec((1, d, tn), lambda l, j: (l, 0, j)),
            pl.BlockSpec((1, 1, tn), lambda l, j: (l, 0, j)),
        ],
        out_specs=pl.BlockSpec((1, r, tn), lambda l, j: (l, 0, j)),
        out_shape=jax.ShapeDtypeStruct((cfg.depth, r, 6 * d), F32),
        compiler_params=_params(("arbitrary", "arbitrary")),
        name="ada_modulation",
    )(cond, w_ada, b_ada.reshape(cfg.depth, 1, 6 * d))


def _proj_kernel(*refs, rope, n_rope_blocks, eps):
    if rope:
        x_ref, mod_ref, g_ref, w_ref, cos_ref, sa_ref, sb_ref, o_ref, h_ref = refs
    else:
        x_ref, mod_ref, g_ref, w_ref, o_ref, h_ref = refs
    j = pl.program_id(1)

    @pl.when(j == 0)
    def _():
        h = _norm_mod(x_ref[...], g_ref[...], mod_ref[0, 0:1, :], mod_ref[0, 1:2, :], eps)
        h_ref[...] = h.astype(BF16)

    acc = jnp.dot(h_ref[...], w_ref[...], preferred_element_type=F32)
    if not rope:
        o_ref[...] = acc.astype(o_ref.dtype)
        return

    @pl.when(j < n_rope_blocks)
    def _():
        cos, sa, sb = cos_ref[...], sa_ref[...], sb_ref[...]
        for c in range(acc.shape[1] // LANES):
            xa = acc[:, c * LANES:(c + 1) * LANES]
            up = pltpu.roll(xa, LANES - 16, 1)
            dn = pltpu.roll(xa, 16, 1)
            o_ref[:, c * LANES:(c + 1) * LANES] = (xa * cos + up * sa + dn * sb).astype(o_ref.dtype)

    @pl.when(j >= n_rope_blocks)
    def _():
        o_ref[...] = acc.astype(o_ref.dtype)


def _rope_tables(cfg):
    t = np.arange(cfg.dec_seq)
    row = (t // cfg.grid_w).astype(np.float64)
    col = (t % cfg.grid_w).astype(np.float64)
    lane = np.arange(LANES)
    l64 = lane % cfg.a_qk
    half = cfg.a_qk // 4
    freq = cfg.rope_theta ** (-(lane % half).astype(np.float64) / half)
    pos = np.where((l64 < cfg.a_qk // 2)[None, :], row[:, None], col[:, None])
    ang = pos * freq[None, :]
    first = (lane % (2 * half)) < half
    cos = np.cos(ang)
    sin = np.sin(ang)
    sa = np.where(first[None, :], -sin, 0.0)
    sb = np.where(first[None, :], 0.0, sin)
    return tuple(jnp.asarray(a, dtype=F32) for a in (cos, sa, sb))


def _projection(cfg, x, mod_l, gain, w, *, latent, out_dtype):
    m, d = x.shape
    n = w.shape[1]
    tm = min(cfg.row_tile, m)
    tn = min(1024, cfg.a_width)
    per_seq = cfg.dec_seq // tm if latent else 1
    mod_idx = (lambda i, j: (1 + i // per_seq, 0, 0)) if latent else (lambda i, j: (0, 0, 0))
    in_specs = [
        pl.BlockSpec((tm, d), lambda i, j: (i, 0)),
        pl.BlockSpec((1, 8, d), mod_idx),
        pl.BlockSpec((1, d), lambda i, j: (0, 0)),
        pl.BlockSpec((d, tn), lambda i, j: (0, j)),
    ]
    args = [x, mod_l, gain, w]
    if latent:
        tab_spec = pl.BlockSpec((tm, LANES), lambda i, j: (i % per_seq, 0))
        in_specs += [tab_spec, tab_spec, tab_spec]
        args += list(_rope_tables(cfg))
    kern = functools.partial(_proj_kernel, rope=latent, n_rope_blocks=2 * cfg.a_width // tn,
                             eps=cfg.norm_eps)
    return pl.pallas_call(
        kern,
        grid=(m // tm, n // tn),
        in_specs=in_specs,
        out_specs=pl.BlockSpec((tm, tn), lambda i, j: (i, j)),
        out_shape=jax.ShapeDtypeStruct((m, n), out_dtype),
        scratch_shapes=[pltpu.VMEM((tm, d), BF16)],
        compiler_params=_params(("arbitrary", "arbitrary")),
        name="qkv_projection_latent" if latent else "qkv_projection_context",
    )(*args)


def _nt_dot(a, b):
    return lax.dot_general(a, b, (((1,), (1,)), ((), ())), preferred_element_type=F32)


def _diff_attn_kernel(*refs, has_ctx, lam_init, eps, qk):
    if has_ctx:
        lam_ref, gain_ref, q_ref, k_ref, v_ref, ck_ref, cv_ref, o_ref = refs
    else:
        lam_ref, gain_ref, q_ref, k_ref, v_ref, o_ref = refs
    lv = lam_ref[...]
    s1 = jnp.sum(lv[0:1] * lv[1:2], axis=-1, keepdims=True)
    s2 = jnp.sum(lv[2:3] * lv[3:4], axis=-1, keepdims=True)
    lam = jnp.exp(s1) - jnp.exp(s2) + lam_init

    q = q_ref[...].astype(F32) * (qk ** -0.5)
    tq = q.shape[0]
    lane = lax.broadcasted_iota(jnp.int32, q.shape, 1)
    qs = jnp.concatenate([jnp.where(lane < qk, q, 0.0), jnp.where(lane >= qk, q, 0.0)],
                         axis=0).astype(BF16)
    s_new = _nt_dot(qs, k_ref[...].astype(BF16))
    m = jnp.max(s_new, axis=-1, keepdims=True)
    if has_ctx:
        s_ctx = _nt_dot(qs, ck_ref[0].astype(BF16))
        m = jnp.maximum(m, jnp.max(s_ctx, axis=-1, keepdims=True))
    p_new = jnp.exp(s_new - m)
    den = jnp.sum(p_new, axis=-1, keepdims=True)
    if has_ctx:
        p_ctx = jnp.exp(s_ctx - m)
        den = den + jnp.sum(p_ctx, axis=-1, keepdims=True)
    inv = 1.0 / den
    c1 = inv[:tq]
    c2 = lam * inv[tq:]
    pc = (p_new[:tq] * c1 - p_new[tq:] * c2).astype(BF16)
    o = jnp.dot(pc, v_ref[...].astype(BF16), preferred_element_type=F32)
    if has_ctx:
        pcc = (p_ctx[:tq] * c1 - p_ctx[tq:] * c2).astype(BF16)
        o = o + jnp.dot(pcc, cv_ref[0].astype(BF16), preferred_element_type=F32)
    ms = jnp.mean(o * o, axis=-1, keepdims=True)
    o = (o * lax.rsqrt(ms + eps)) * gain_ref[...]
    o_ref[...] = (o * (1.0 - lam_init)).astype(o_ref.dtype)


def _diff_attention(cfg, proj, lam_pack, gain, n_batch, seq_len, lam_init, ctx=None):
    hd = cfg.a_dim
    nh = cfg.a_heads
    tq = min(cfg.q_tile, seq_len)
    nq = seq_len // tq
    in_specs = [
        pl.BlockSpec((8, LANES), lambda b, h, qi: (0, 0)),
        pl.BlockSpec((1, hd), lambda b, h, qi: (0, 0)),
        pl.BlockSpec((tq, hd), lambda b, h, qi: (b * nq + qi, h)),
        pl.BlockSpec((seq_len, hd), lambda b, h, qi: (b, nh + h)),
        pl.BlockSpec((seq_len, hd), lambda b, h, qi: (b, 2 * nh + h)),
    ]
    args = [lam_pack, gain, proj, proj, proj]
    if ctx is not None:
        ck, cv = ctx
        past = ck.shape[1]
        cspec = pl.BlockSpec((1, past, hd), lambda b, h, qi: (b, 0, h))
        in_specs += [cspec, cspec]
        args += [ck, cv]
    kern = functools.partial(_diff_attn_kernel, has_ctx=ctx is not None, lam_init=lam_init,
                             eps=cfg.subln_eps, qk=cfg.a_qk)
    return pl.pallas_call(
        kern,
        grid=(n_batch, nh, nq),
        in_specs=in_specs,
        out_specs=pl.BlockSpec((tq, hd), lambda b, h, qi: (b * nq + qi, h)),
        out_shape=jax.ShapeDtypeStruct((n_batch * seq_len, cfg.a_width), BF16),
        compiler_params=_params(("arbitrary", "arbitrary", "arbitrary")),
        name="diff_attention_latent" if ctx is not None else "diff_attention_context",
    )(*args)


def _soft_attn_kernel(q_ref, k_ref, v_ref, o_ref, *, scale):
    s = _nt_dot(q_ref[...].astype(BF16), k_ref[...].astype(BF16)) * scale
    m = jnp.max(s, axis=-1, keepdims=True)
    p = jnp.exp(s - m)
    p = (p * (1.0 / jnp.sum(p, axis=-1, keepdims=True))).astype(BF16)
    o_ref[...] = jnp.dot(p, v_ref[...].astype(BF16), preferred_element_type=F32).astype(o_ref.dtype)


def _soft_attention(cfg, proj, n_batch, seq_len):
    hd = cfg.b_dim
    nh = cfg.b_heads
    base = 3 * cfg.a_width // hd
    return pl.pallas_call(
        functools.partial(_soft_attn_kernel, scale=hd ** -0.5),
        grid=(n_batch, nh),
        in_specs=[
            pl.BlockSpec((seq_len, hd), lambda b, h: (b, base + h)),
            pl.BlockSpec((seq_len, hd), lambda b, h: (b, base + nh + h)),
            pl.BlockSpec((seq_len, hd), lambda b, h: (b, base + 2 * nh + h)),
        ],
        out_specs=pl.BlockSpec((seq_len, hd), lambda b, h: (b, h)),
        out_shape=jax.ShapeDtypeStruct((n_batch * seq_len, cfg.b_width), BF16),
        compiler_params=_params(("arbitrary", "arbitrary")),
        name="softmax_attention_context",
    )(proj, proj, proj)


def _na_kernel(q_ref, k_ref, v_ref, ck_ref, cv_ref, bias_ref, o_ref, *, rows, kr, gw, win_r, scale):
    ck = ck_ref[0].astype(BF16)
    cv = cv_ref[0].astype(BF16)

    def body(r, carry):
        rs = jnp.clip(r - kr // 2, 0, rows - kr)
        q = q_ref[pl.ds(pl.multiple_of(r * gw, gw), gw), :]
        kw = k_ref[pl.ds(pl.multiple_of(rs * gw, gw), kr * gw), :]
        vw = v_ref[pl.ds(pl.multiple_of(rs * gw, gw), kr * gw), :]
        s = _nt_dot(q, kw) * scale + bias_ref[rs - r + win_r - 1, 0]
        sc = _nt_dot(q, ck) * scale
        m = jnp.maximum(jnp.max(s, axis=-1, keepdims=True), jnp.max(sc, axis=-1, keepdims=True))
        p = jnp.exp(s - m)
        pc = jnp.exp(sc - m)
        inv = 1.0 / (jnp.sum(p, axis=-1, keepdims=True) + jnp.sum(pc, axis=-1, keepdims=True))
        o = (jnp.dot((p * inv).astype(BF16), vw, preferred_element_type=F32)
             + jnp.dot((pc * inv).astype(BF16), cv, preferred_element_type=F32))
        o_ref[pl.ds(pl.multiple_of(r * gw, gw), gw), :] = o.astype(o_ref.dtype)
        return carry

    lax.fori_loop(0, rows, body, 0)


def _na_bias_table(cfg, rpb, kr):
    w = cfg.grid_w
    qc = jnp.arange(w)
    kc = jnp.arange(w)
    cs = jnp.clip(qc - cfg.na_win_c // 2, 0, w - cfg.na_win_c)
    col_mask = (kc[None, :] >= cs[:, None]) & (kc[None, :] < cs[:, None] + cfg.na_win_c)
    col_idx = jnp.clip(kc[None, :] - qc[:, None] + cfg.na_win_c - 1, 0, 2 * cfg.na_win_c - 2)
    row_idx = jnp.arange(cfg.na_win_r)[:, None] + jnp.arange(kr)[None, :]
    t = rpb[:, row_idx][:, :, :, col_idx]
    t = jnp.where(col_mask[None, None, None], t.astype(F32), NEG_BIG)
    t = jnp.transpose(t, (1, 0, 3, 2, 4))
    return t.reshape(cfg.na_win_r, cfg.b_heads, w, kr * w)


def _na_attention(cfg, proj, ck, cv, rpb):
    hd = cfg.b_dim
    nh = cfg.b_heads
    n = cfg.dec_seq
    rows = n // cfg.grid_w
    kr = min(cfg.na_win_r, rows)
    base = 3 * cfg.a_width // hd
    past = ck.shape[1]
    bias = _na_bias_table(cfg, rpb, kr)
    kern = functools.partial(_na_kernel, rows=rows, kr=kr, gw=cfg.grid_w, win_r=cfg.na_win_r,
                             scale=hd ** -0.5)
    cspec = pl.BlockSpec((1, past, hd), lambda b, h: (b, 0, h))
    return pl.pallas_call(
        kern,
        grid=(cfg.dec_batch, nh),
        in_specs=[
            pl.BlockSpec((n, hd), lambda b, h: (b, base + h)),
            pl.BlockSpec((n, hd), lambda b, h: (b, base + nh + h)),
            pl.BlockSpec((n, hd), lambda b, h: (b, base + 2 * nh + h)),
            cspec, cspec,
            pl.BlockSpec((cfg.na_win_r, 1, cfg.grid_w, kr * cfg.grid_w), lambda b, h: (0, h, 0, 0)),
        ],
        out_specs=pl.BlockSpec((n, hd), lambda b, h: (b, h)),
        out_shape=jax.ShapeDtypeStruct((cfg.n_s, cfg.b_width), BF16),
        compiler_params=_params(("arbitrary", "arbitrary")),
        name="neighbourhood_attention",
    )(proj, proj, proj, ck, cv, bias)


def _linres_kernel(*refs, n_parts, n_pb, x_split):
    i = pl.program_id(0)
    is_p = i < n_pb
    pos = 0
    acc = None
    w_ref = refs[2 * n_parts]
    k0 = 0
    for p in range(n_parts):
        a_p, a_s = refs[2 * p], refs[2 * p + 1]
        a = jnp.where(is_p, a_p[...], a_s[...])
        kk = a.shape[1]
        part = jnp.dot(a, w_ref[k0:k0 + kk, :], preferred_element_type=F32)
        acc = part if acc is None else acc + part
        k0 += kk
    pos = 2 * n_parts + 1
    if x_split:
        x = jnp.where(is_p, refs[pos][...], refs[pos + 1][...])
        pos += 2
    else:
        x = refs[pos][...]
        pos += 1
    mod_ref, o_ref = refs[pos], refs[pos + 1]
    o_ref[...] = x + mod_ref[0, 2:3, :] * acc


def _linear_residual(cfg, parts, w, x, mod_l):
    d = cfg.d_model
    tm = min(256, cfg.tm)
    n_pb = cfg.n_p // tm
    n_sb = cfg.n_s // tm
    p_idx = lambda i: (jnp.minimum(i, n_pb - 1), 0)
    s_idx = lambda i: (jnp.maximum(i - n_pb, 0), 0)
    in_specs, args = [], []
    for a_p, a_s in parts:
        kk = a_p.shape[1]
        in_specs += [pl.BlockSpec((tm, kk), p_idx), pl.BlockSpec((tm, kk), s_idx)]
        args += [a_p, a_s]
    in_specs.append(pl.BlockSpec(w.shape, lambda i: (0, 0)))
    args.append(w)
    x_split = isinstance(x, tuple)
    if x_split:
        in_specs += [pl.BlockSpec((tm, d), p_idx), pl.BlockSpec((tm, d), s_idx)]
        args += list(x)
    else:
        in_specs.append(pl.BlockSpec((tm, d), lambda i: (i, 0)))
        args.append(x)
    mi = _merged_mod_index(cfg, tm)
    in_specs.append(pl.BlockSpec((1, 8, d), lambda i: (mi(i), 0, 0)))
    args.append(mod_l)
    kern = functools.partial(_linres_kernel, n_parts=len(parts), n_pb=n_pb, x_split=x_split)
    return pl.pallas_call(
        kern,
        grid=(n_pb + n_sb,),
        in_specs=in_specs,
        out_specs=pl.BlockSpec((tm, d), lambda i: (i, 0)),
        out_shape=jax.ShapeDtypeStruct((cfg.n_tok, d), F32),
        compiler_params=_params(("arbitrary",)),
        name="linear_gated_residual",
    )(*args)


def _dft_chan_kernel(x_ref, mod_ref, g_ref, cs_ref, y_ref, *, groups, eps):
    h = _norm_mod(x_ref[...], g_ref[...], mod_ref[0, 0:1, :], mod_ref[0, 1:2, :], eps).astype(BF16)
    gd = h.shape[1] // groups
    for g in range(groups):
        r = jnp.dot(h[:, g * gd:(g + 1) * gd], cs_ref[...], preferred_element_type=F32)
        y_ref[0, :, g * gd:(g + 1) * gd] = r[:, :gd].astype(BF16)
        y_ref[1, :, g * gd:(g + 1) * gd] = r[:, gd:].astype(BF16)


def _dft_mats(n):
    k = np.arange(n)
    ang = 2.0 * np.pi * ((k[:, None] * k[None, :]) % n) / n
    return np.cos(ang), np.sin(ang)


def _dft_channels(cfg, x, mod_l, gain):
    d = cfg.d_model
    gd = d // cfg.f_groups
    tm = cfg.tm
    c, s = _dft_mats(gd)
    cs = jnp.asarray(np.concatenate([c, s], axis=1), dtype=F32).astype(BF16)
    mi = _merged_mod_index(cfg, tm)
    return pl.pallas_call(
        functools.partial(_dft_chan_kernel, groups=cfg.f_groups, eps=cfg.norm_eps),
        grid=(cfg.n_tok // tm,),
        in_specs=[
            pl.BlockSpec((tm, d), lambda i: (i, 0)),
            pl.BlockSpec((1, 8, d), lambda i: (mi(i), 0, 0)),
            pl.BlockSpec((1, d), lambda i: (0, 0)),
            pl.BlockSpec((gd, 2 * gd), lambda i: (0, 0)),
        ],
        out_specs=pl.BlockSpec((2, tm, d), lambda i: (0, i, 0)),
        out_shape=jax.ShapeDtypeStruct((2, cfg.n_tok, d), BF16),
        compiler_params=_params(("arbitrary",)),
        name="dft_channels",
    )(x, mod_l, gain, cs)


def _dft_seq_kernel(w_ref, y_ref, o_ref, *, scale):
    acc = (jnp.dot(w_ref[0], y_ref[0], preferred_element_type=F32)
           + jnp.dot(w_ref[1], y_ref[1], preferred_element_type=F32))
    o_ref[...] = (acc * scale).astype(o_ref.dtype)


def _dft_sequence(cfg, y, n_batch, seq_len, first_block):
    d = cfg.d_model
    c, s = _dft_mats(seq_len)
    wm = jnp.asarray(np.stack([c, -s]), dtype=F32).astype(BF16)
    tml = min(1024, seq_len)
    tn = min(512, d)
    nm = seq_len // tml
    scale = 1.0 / math.sqrt(seq_len * (d // cfg.f_groups))
    return pl.pallas_call(
        functools.partial(_dft_seq_kernel, scale=scale),
        grid=(n_batch, nm, d // tn),
        in_specs=[
            pl.BlockSpec((2, tml, seq_len), lambda b, mi, j: (0, mi, 0)),
            pl.BlockSpec((2, seq_len, tn), lambda b, mi, j: (0, first_block + b, j)),
        ],
        out_specs=pl.BlockSpec((tml, tn), lambda b, mi, j: (b * nm + mi, j)),
        out_shape=jax.ShapeDtypeStruct((n_batch * seq_len, d), BF16),
        compiler_params=_params(("arbitrary", "arbitrary", "arbitrary")),
        name="dft_sequence_%d" % seq_len,
    )(wm, y)


def _route_kernel(x_ref, mod_ref, g_ref, wr_ref, rb_ref, hp_ref, idx_ref, wt_ref, *, eps, n_exp, per_grp):
    h = _norm_mod(x_ref[...], g_ref[...], mod_ref[0, 3:4, :], mod_ref[0, 4:5, :], eps)
    tm = h.shape[0]
    n_chunk = h.shape[1] // LANES
    for c in range(n_chunk):
        hp_ref[pl.ds(c, tm, stride=n_chunk), :] = h[:, c * LANES:(c + 1) * LANES]

    logits = lax.dot_general(wr_ref[...], h, (((1,), (1,)), ((), ())),
                             precision=lax.Precision.HIGHEST, preferred_element_type=F32)
    scores = jax.nn.sigmoid(logits)
    sel = scores + rb_ref[...]
    n_grp = n_exp // per_grp
    best = None
    gi = None
    for g in range(n_grp):
        v = [sel[g * per_grp + k:g * per_grp + k + 1, :] for k in range(per_grp)]
        gs = None
        for a in range(per_grp):
            for b in range(a + 1, per_grp):
                ps = v[a] + v[b]
                gs = ps if gs is None else jnp.maximum(gs, ps)
        if best is None:
            best, gi = gs, jnp.zeros(gs.shape, jnp.int32)
        else:
            better = gs > best
            gi = jnp.where(better, g, gi)
            best = jnp.where(better, gs, best)
    row = lax.broadcasted_iota(jnp.int32, sel.shape, 0)
    masked = jnp.where(row // per_grp == gi, sel, -jnp.inf)
    m1 = jnp.max(masked, axis=0, keepdims=True)
    i1 = jnp.min(jnp.where(masked == m1, row, n_exp), axis=0, keepdims=True)
    masked2 = jnp.where(row == i1, -jnp.inf, masked)
    m2 = jnp.max(masked2, axis=0, keepdims=True)
    i2 = jnp.min(jnp.where(masked2 == m2, row, n_exp), axis=0, keepdims=True)
    w1 = jnp.sum(jnp.where(row == i1, scores, 0.0), axis=0, keepdims=True)
    w2 = jnp.sum(jnp.where(row == i2, scores, 0.0), axis=0, keepdims=True)
    inv = 1.0 / (w1 + w2)
    idx_ref[...] = jnp.concatenate([i1, i2], axis=0)
    wt_ref[...] = jnp.concatenate([w1 * inv, w2 * inv], axis=0)


def _route(cfg, x, mod_l, gain, w_router_t, router_bias):
    d = cfg.d_model
    tm = cfg.tm
    t = cfg.n_tok
    mi = _merged_mod_index(cfg, tm)
    kern = functools.partial(_route_kernel, eps=cfg.norm_eps, n_exp=cfg.n_experts,
                             per_grp=cfg.n_experts // cfg.n_groups)
    return pl.pallas_call(
        kern,
        grid=(t // tm,),
        in_specs=[
            pl.BlockSpec((tm, d), lambda i: (i, 0)),
            pl.BlockSpec((1, 8, d), lambda i: (mi(i), 0, 0)),
            pl.BlockSpec((1, d), lambda i: (0, 0)),
            pl.BlockSpec((cfg.n_experts, d), lambda i: (0, 0)),
            pl.BlockSpec((cfg.n_experts, 1), lambda i: (0, 0)),
        ],
        out_specs=[
            pl.BlockSpec((tm * (d // LANES), LANES), lambda i: (i, 0)),
            pl.BlockSpec((2, tm), lambda i: (0, i)),
            pl.BlockSpec((2, tm), lambda i: (0, i)),
        ],
        out_shape=[
            jax.ShapeDtypeStruct((t * (d // LANES), LANES), F32),
            jax.ShapeDtypeStruct((2, t), jnp.int32),
            jax.ShapeDtypeStruct((2, t), F32),
        ],
        compiler_params=_params(("arbitrary",)),
        name="moe_route",
    )(x, mod_l, gain, w_router_t, router_bias.reshape(cfg.n_experts, 1))


def _plan(cfg, idx):
    t = cfg.n_tok
    eb = cfg.e_block
    ne = cfg.n_experts
    n_blocks = -(-(2 * t + ne * (eb - 1)) // eb)
    e_flat = idx.reshape(-1)
    onehot = (e_flat[:, None] == jnp.arange(ne, dtype=jnp.int32)[None, :]).astype(jnp.int32)
    csum = jnp.cumsum(onehot, axis=0)
    rank = jnp.sum(onehot * (csum - 1), axis=1)
    counts = csum[-1]
    padded = (counts + eb - 1) // eb * eb
    pad_end = jnp.cumsum(padded)
    pad_start = pad_end - padded
    pos = (pad_start[e_flat] + rank).astype(jnp.int32).reshape(2, t)
    n_used = (pad_end[-1] // eb).astype(jnp.int32)
    blk = jnp.arange(n_blocks, dtype=jnp.int32)
    blk = jnp.minimum(blk, n_used - 1)
    first_e = jnp.sum((pad_end[None, :] <= (blk * eb)[:, None]).astype(jnp.int32), axis=1)
    block_e = jnp.minimum(first_e, ne - 1).astype(jnp.int32)
    return pos, block_e, n_used.reshape(1), pad_end.astype(jnp.int32), n_blocks


def _dispatch_kernel(pend_ref, pos_ref, hp_ref, xr_ref, zbuf, sem, zsem, *, n_exp, eb, spt):
    i = pl.program_id(0)

    @pl.when(i == 0)
    def _():
        zbuf[...] = jnp.zeros(zbuf.shape, zbuf.dtype)
        for e in range(n_exp):
            start = pl.multiple_of(jnp.maximum(pend_ref[e] - eb, 0) * spt, eb * spt)
            cp = pltpu.make_async_copy(zbuf, xr_ref.at[pl.ds(start, eb * spt)], zsem)
            cp.start()
            cp.wait()
        n_used = pend_ref[n_exp - 1] // eb
        n_blocks = xr_ref.shape[0] // (eb * spt)
        for e in range(n_exp):
            @pl.when(n_used + e < n_blocks)
            def _():
                start = pl.multiple_of((n_used + e) * (eb * spt), eb * spt)
                cp = pltpu.make_async_copy(zbuf, xr_ref.at[pl.ds(start, eb * spt)], zsem)
                cp.start()
                cp.wait()

    rows = hp_ref.shape[0] // spt

    def row_copy(k, r):
        src = hp_ref.at[pl.ds(pl.multiple_of(r * spt, spt), spt)]
        dst = xr_ref.at[pl.ds(pl.multiple_of(pos_ref[0, k, r] * spt, spt), spt)]
        return pltpu.make_async_copy(src, dst, sem.at[k])

    for k in range(2):
        def start(r, c, k=k):
            row_copy(k, r).start()
            return c
        lax.fori_loop(0, rows, start, 0)
    for k in range(2):
        def wait(r, c, k=k):
            row_copy(k, r).wait()
            return c
        lax.fori_loop(0, rows, wait, 0)


def _dispatch(cfg, hp, pos_blocks, pad_end, n_rows):
    tb = pos_blocks.shape[2]
    spt = cfg.d_model // LANES
    kern = functools.partial(_dispatch_kernel, n_exp=cfg.n_experts, eb=cfg.e_block, spt=spt)
    return pl.pallas_call(
        kern,
        grid_spec=pltpu.PrefetchScalarGridSpec(
            num_scalar_prefetch=1,
            grid=(cfg.n_tok // tb,),
            in_specs=[
                pl.BlockSpec((1, 2, tb), lambda i, pe: (i, 0, 0), memory_space=pltpu.SMEM),
                pl.BlockSpec((tb * spt, LANES), lambda i, pe: (i, 0)),
            ],
            out_specs=pl.BlockSpec(memory_space=pl.ANY),
            scratch_shapes=[
                pltpu.VMEM((cfg.e_block * spt, LANES), F32),
                pltpu.SemaphoreType.DMA((2,)),
                pltpu.SemaphoreType.DMA(()),
            ],
        ),
        out_shape=jax.ShapeDtypeStruct((n_rows * spt, LANES), F32),
        compiler_params=_params(("arbitrary",)),
        name="moe_dispatch",
    )(pad_end, pos_blocks, hp)


def _expert_kernel(be_ref, nu_ref, xp_ref, wg_ref, wu_ref, wd_ref, y_ref):
    b = pl.program_id(0)

    @pl.when(b < nu_ref[0])
    def _():
        d = wg_ref.shape[1]
        n_chunk = d // LANES
        eb = xp_ref.shape[0] // n_chunk
        x = jnp.concatenate([xp_ref[pl.ds(c, eb, stride=n_chunk), :].astype(BF16) for c in range(n_chunk)],
                            axis=1)
        g = jnp.dot(x, wg_ref[0], preferred_element_type=F32)
        u = jnp.dot(x, wu_ref[0], preferred_element_type=F32)
        a = (_silu(g) * u).astype(BF16)
        y = jnp.dot(a, wd_ref[0], preferred_element_type=F32)
        n_chunk = d // LANES
        for c in range(n_chunk):
            y_ref[pl.ds(c, eb, stride=n_chunk), :] = y[:, c * LANES:(c + 1) * LANES]

    @pl.when(b >= nu_ref[0])
    def _():
        y_ref[...] = jnp.zeros(y_ref.shape, y_ref.dtype)


def _experts(cfg, x_rows, block_e, n_used, wg, wu, wd, n_blocks):
    d = cfg.d_model
    f = cfg.d_expert
    eb = cfg.e_block
    n_chunk = d // LANES
    spt = n_chunk
    return pl.pallas_call(
        _expert_kernel,
        grid_spec=pltpu.PrefetchScalarGridSpec(
            num_scalar_prefetch=2,
            grid=(n_blocks,),
            in_specs=[
                pl.BlockSpec((eb * spt, LANES), lambda b, be, nu: (jnp.minimum(b, nu[0] - 1), 0)),
                pl.BlockSpec((1, d, f), lambda b, be, nu: (be[b], 0, 0)),
                pl.BlockSpec((1, d, f), lambda b, be, nu: (be[b], 0, 0)),
                pl.BlockSpec((1, f, d), lambda b, be, nu: (be[b], 0, 0)),
            ],
            out_specs=pl.BlockSpec((eb * n_chunk, LANES), lambda b, be, nu: (b, 0)),
        ),
        out_shape=jax.ShapeDtypeStruct((n_blocks * eb * n_chunk, LANES), F32),
        compiler_params=_params(("arbitrary",)),
        name="moe_experts",
    )(block_e, n_used, x_rows, wg, wu, wd)


def _combine_kernel(pos_ref, y_ref, x_ref, mod_ref, wt_ref, fg_ref, o_ref, ybuf, sem, *, final, eps):
    rows, d = x_ref.shape
    n_chunk = d // LANES

    def row_copy(k, r):
        src = y_ref.at[pl.ds(pl.multiple_of(pos_ref[0, k, r] * n_chunk, n_chunk), n_chunk)]
        dst = ybuf.at[k, pl.ds(pl.multiple_of(r * n_chunk, n_chunk), n_chunk)]
        return pltpu.make_async_copy(src, dst, sem.at[k])

    for k in range(2):
        def start(r, c, k=k):
            row_copy(k, r).start()
            return c
        lax.fori_loop(0, rows, start, 0)
    for k in range(2):
        def wait(r, c, k=k):
            row_copy(k, r).wait()
            return c
        lax.fori_loop(0, rows, wait, 0)

    w = wt_ref[...]
    w0, w1 = w[:, 0:1], w[:, 1:2]
    sumsq = jnp.zeros((rows, 1), F32)
    for c in range(n_chunk):
        cols = slice(c * LANES, (c + 1) * LANES)
        moe = (w0 * ybuf[0, pl.ds(c, rows, stride=n_chunk), :]
               + w1 * ybuf[1, pl.ds(c, rows, stride=n_chunk), :])
        xc = x_ref[:, cols] + mod_ref[0, 5:6, cols] * moe
        o_ref[:, cols] = xc
        sumsq = sumsq + jnp.sum(xc * xc, axis=-1, keepdims=True)
    if final:
        o_ref[...] = (o_ref[...] * lax.rsqrt(sumsq * (1.0 / d) + eps)) * fg_ref[...]


def _combine(cfg, y, pos_blocks, wts_t, x, mod_l, final_gain, *, row0, n_rows, final):
    d = cfg.d_model
    tb = pos_blocks.shape[2]
    b0 = row0 // tb
    mi = _merged_mod_index(cfg, tb)
    kern = functools.partial(_combine_kernel, final=final, eps=cfg.norm_eps)
    return pl.pallas_call(
        kern,
        grid=(n_rows // tb,),
        in_specs=[
            pl.BlockSpec((1, 2, tb), lambda i: (b0 + i, 0, 0), memory_space=pltpu.SMEM),
            pl.BlockSpec(memory_space=pl.ANY),
            pl.BlockSpec((tb, d), lambda i: (b0 + i, 0)),
            pl.BlockSpec((1, 8, d), lambda i: (mi(b0 + i), 0, 0)),
            pl.BlockSpec((tb, 2), lambda i: (b0 + i, 0)),
            pl.BlockSpec((1, d), lambda i: (0, 0)),
        ],
        out_specs=pl.BlockSpec((tb, d), lambda i: (i, 0)),
        out_shape=jax.ShapeDtypeStruct((n_rows, d), F32),
        scratch_shapes=[pltpu.VMEM((2, tb * (d // LANES), LANES), F32), pltpu.SemaphoreType.DMA((2,))],
        compiler_params=_params(("arbitrary",)),
        name="moe_combine_final" if final else "moe_combine",
    )(pos_blocks, y, x, mod_l, wts_t, final_gain)


def _moe(cfg, x, mod_l, gain, w_router_t, router_bias, wg, wu, wd, final_gain, final):
    tb = min(256, cfg.tm)
    hp, idx, wts = _route(cfg, x, mod_l, gain, w_router_t, router_bias)
    pos, block_e, n_used, pad_end, n_blocks = _plan(cfg, idx)
    pos_blocks = pos.reshape(2, cfg.n_tok // tb, tb).transpose(1, 0, 2)
    x_rows = _dispatch(cfg, hp, pos_blocks, pad_end, n_blocks * cfg.e_block)
    y = _experts(cfg, x_rows, block_e, n_used, wg, wu, wd, n_blocks)
    wts_t = wts.T
    comb = functools.partial(_combine, cfg, y, pos_blocks, wts_t, x, mod_l, final_gain, final=final)
    if final:
        return comb(row0=0, n_rows=cfg.n_p), comb(row0=cfg.n_p, n_rows=cfg.n_s)
    return comb(row0=0, n_rows=cfg.n_tok)


def _forward(cfg, x_prompt, x_sample, cache_a_k, cache_a_v, cache_b_k, cache_b_v, c, c_ctx, w_ada, b_ada,
             norm1, norm2, final_norm, w_in, w_mix_out, lambda_q1, lambda_k1, lambda_q2, lambda_k2,
             subln_gain, na_rel_bias, w_fourier_out, w_router, router_bias, w_exp_gate, w_exp_up,
             w_exp_down):
    d = cfg.d_model
    xp = x_prompt.reshape(cfg.n_p, d)
    xs = x_sample.reshape(cfg.n_s, d)

    n_cond = -(-cfg.n_mod // 8) * 8
    cond = jnp.concatenate([c_ctx[None, :], c, jnp.zeros((n_cond - cfg.n_mod, d), F32)], axis=0)
    mod = _modulation(cfg, cond, w_ada, b_ada)
    mod = mod.reshape(cfg.depth, n_cond, 6, d)[:, :cfg.n_mod]
    mod = jnp.pad(mod, ((0, 0), (0, 0), (0, 2), (0, 0)))

    w_router_t = w_router.T
    fgain = final_norm.reshape(1, d)
    x = (xp, xs)
    kv_cache = None
    for l in range(cfg.depth):
        j = l // 2
        g1 = norm1[l].reshape(1, d)
        if l % 2 == 0:
            lam_init = 0.8 - 0.6 * math.exp(-0.3 * l)
            lam_pack = jnp.zeros((8, LANES), F32).at[:4, :cfg.a_qk].set(
                jnp.stack([lambda_q1[j], lambda_k1[j], lambda_q2[j], lambda_k2[j]]))
            sgain = subln_gain[j].reshape(1, cfg.a_dim)
            w_in_b = w_in[j].astype(BF16)
            if isinstance(x, tuple):
                x_p, x_s = x
            else:
                x_p, x_s = x[:cfg.n_p], x[cfg.n_p:]
            proj_p = _projection(cfg, x_p, mod[l], g1, w_in_b, latent=False, out_dtype=F32)
            proj_s = _projection(cfg, x_s, mod[l], g1, w_in_b, latent=True, out_dtype=BF16)
            if kv_cache is None:
                kv_cache = []
            aw, bw = cfg.a_width, cfg.b_width
            kv_cache.append((proj_p[:, aw:2 * aw], proj_p[:, 2 * aw:3 * aw],
                             proj_p[:, 3 * aw + bw:3 * aw + 2 * bw], proj_p[:, 3 * aw + 2 * bw:]))
            cak = cache_a_k[:, j].reshape(cfg.dec_batch, cfg.past_len, aw)
            cav = cache_a_v[:, j].reshape(cfg.dec_batch, cfg.past_len, aw)
            cbk = cache_b_k[:, j].reshape(cfg.dec_batch, cfg.past_len, bw)
            cbv = cache_b_v[:, j].reshape(cfg.dec_batch, cfg.past_len, bw)
            a_p = _diff_attention(cfg, proj_p, lam_pack, sgain, cfg.batch, cfg.seq, lam_init)
            b_p = _soft_attention(cfg, proj_p, cfg.batch, cfg.seq)
            a_s = _diff_attention(cfg, proj_s, lam_pack, sgain, cfg.dec_batch, cfg.dec_seq, lam_init,
                                  ctx=(cak, cav))
            b_s = _na_attention(cfg, proj_s, cbk, cbv, na_rel_bias[j])
            x = _linear_residual(cfg, [(a_p, a_s), (b_p, b_s)], w_mix_out[j].astype(BF16), x, mod[l])
        else:
            if isinstance(x, tuple):
                x = jnp.concatenate(x, axis=0)
            y = _dft_channels(cfg, x, mod[l], g1)
            f_p = _dft_sequence(cfg, y, cfg.batch, cfg.seq, 0)
            f_s = _dft_sequence(cfg, y, cfg.dec_batch, cfg.dec_seq, cfg.n_p // cfg.dec_seq)
            x = _linear_residual(cfg, [(f_p, f_s)], w_fourier_out[j].astype(BF16), x, mod[l])
        x = _moe(cfg, x, mod[l], norm2[l].reshape(1, d), w_router_t, router_bias,
                 w_exp_gate[l].astype(BF16), w_exp_up[l].astype(BF16), w_exp_down[l].astype(BF16),
                 fgain, final=(l == cfg.depth - 1))
    y_p, y_s = x
    n_even = (cfg.depth + 1) // 2
    outs = [y_p.reshape(cfg.batch, cfg.seq, d), y_s.reshape(cfg.dec_batch, cfg.dec_seq, d)]
    for t in range(4):
        heads, hd = (cfg.a_heads, cfg.a_dim) if t < 2 else (cfg.b_heads, cfg.b_dim)
        stacked = jnp.stack([kv_cache[jj][t].reshape(cfg.batch, cfg.seq, heads, hd)
                             for jj in range(n_even)], axis=1)
        outs.append(stacked)
    return tuple(outs)


def kernel(x_prompt, x_sample, cache_a_k, cache_a_v, cache_b_k, cache_b_v, c, c_ctx, w_ada, b_ada, norm1, norm2, final_norm, w_in, w_mix_out, lambda_q1, lambda_k1, lambda_q2, lambda_k2, subln_gain, na_rel_bias, w_fourier_out, w_router, router_bias, w_exp_gate, w_exp_up, w_exp_down):
    return _forward(Cfg(), x_prompt, x_sample, cache_a_k, cache_a_v, cache_b_k, cache_b_v, c, c_ctx, w_ada,
                    b_ada, norm1, norm2, final_norm, w_in, w_mix_out, lambda_q1, lambda_k1, lambda_q2,
                    lambda_k2, subln_gain, na_rel_bias, w_fourier_out, w_router, router_bias, w_exp_gate,
                    w_exp_up, w_exp_down)
```

```python
import functools
import math
from typing import NamedTuple

import numpy as np
import jax
import jax.numpy as jnp
from jax import lax
from jax.experimental import pallas as pl
from jax.experimental.pallas import tpu as pltpu

F32 = jnp.float32
BF16 = jnp.bfloat16

LANES = 128
NEG_BIG = -1e30
VMEM_LIMIT = 56 * 1024 * 1024


class Cfg(NamedTuple):
    d_model: int = 2048
    batch: int = 16
    seq: int = 256
    depth: int = 2
    dec_batch: int = 8
    dec_seq: int = 2048
    past_len: int = 256
    grid_w: int = 64
    a_heads: int = 8
    a_qk: int = 64
    b_heads: int = 8
    b_dim: int = 128
    na_win_r: int = 8
    na_win_c: int = 16
    f_groups: int = 4
    n_experts: int = 16
    n_groups: int = 4
    d_expert: int = 1408
    rope_theta: float = 10000.0
    norm_eps: float = 1e-6
    subln_eps: float = 1e-5
    row_tile: int = 512
    q_tile: int = 256
    e_block: int = 256

    @property
    def a_dim(self):
        return 2 * self.a_qk

    @property
    def a_width(self):
        return self.a_heads * self.a_dim

    @property
    def b_width(self):
        return self.b_heads * self.b_dim

    @property
    def n_p(self):
        return self.batch * self.seq

    @property
    def n_s(self):
        return self.dec_batch * self.dec_seq

    @property
    def n_tok(self):
        return self.n_p + self.n_s

    @property
    def tm(self):
        return min(self.row_tile, self.n_p, self.dec_seq)

    @property
    def n_mod(self):
        return 1 + self.dec_batch


def _params(sem):
    return pltpu.CompilerParams(dimension_semantics=sem, vmem_limit_bytes=VMEM_LIMIT)


def _norm_mod(x, g, sh, sc, eps):
    ms = jnp.mean(x * x, axis=-1, keepdims=True)
    y = x * lax.rsqrt(ms + eps)
    return (y * g) * (1.0 + sc) + sh


def _silu(x):
    return x * jax.nn.sigmoid(x)


def _merged_mod_index(cfg, tm):
    def f(i):
        r = i * tm
        return jnp.where(r < cfg.n_p, 0, 1 + (r - cfg.n_p) // cfg.dec_seq)
    return f


def _ada_kernel(cond_ref, w_ref, b_ref, o_ref):
    s = _silu(cond_ref[...])
    s_hi = s.astype(BF16)
    s_lo = (s - s_hi.astype(F32)).astype(BF16)
    lhs = jnp.concatenate([s_hi, s_lo], axis=0)
    r = jnp.dot(lhs, w_ref[0].astype(BF16), preferred_element_type=F32)
    n = s.shape[0]
    o_ref[0] = r[:n] + r[n:] + b_ref[0]


def _modulation(cfg, cond, w_ada, b_ada):
    d = cfg.d_model
    r = cond.shape[0]
    tn = math.gcd(1024, 6 * d)
    return pl.pallas_call(
        _ada_kernel,
        grid=(cfg.depth, 6 * d // tn),
        in_specs=[
            pl.BlockSpec((r, d), lambda l, j: (0, 0)),
            pl.BlockSpec((1, d, tn), lambda l, j: (l, 0, j)),
            pl.BlockSpec((1, 1, tn), lambda l, j: (l, 0, j)),
        ],
        out_specs=pl.BlockSpec((1, r, tn), lambda l, j: (l, 0, j)),
        out_shape=jax.ShapeDtypeStruct((cfg.depth, r, 6 * d), F32),
        compiler_params=_params(("arbitrary", "arbitrary")),
        name="ada_modulation",
    )(cond, w_ada, b_ada.reshape(cfg.depth, 1, 6 * d))


def _proj_kernel(*refs, rope, n_rope_blocks, eps):
    if rope:
        x_ref, mod_ref, g_ref, w_ref, cos_ref, sa_ref, sb_ref, o_ref, h_ref = refs
    else:
        x_ref, mod_ref, g_ref, w_ref, o_ref, h_ref = refs
    j = pl.program_id(1)

    @pl.when(j == 0)
    def _():
        h = _norm_mod(x_ref[...], g_ref[...], mod_ref[0, 0:1, :], mod_ref[0, 1:2, :], eps)
        h_ref[...] = h.astype(BF16)

    acc = jnp.dot(h_ref[...], w_ref[...], preferred_element_type=F32)
    if not rope:
        o_ref[...] = acc.astype(o_ref.dtype)
        return

    @pl.when(j < n_rope_blocks)
    def _():
        cos, sa, sb = cos_ref[...], sa_ref[...], sb_ref[...]
        for c in range(acc.shape[1] // LANES):
            xa = acc[:, c * LANES:(c + 1) * LANES]
            up = pltpu.roll(xa, LANES - 16, 1)
            dn = pltpu.roll(xa, 16, 1)
            o_ref[:, c * LANES:(c + 1) * LANES] = (xa * cos + up * sa + dn * sb).astype(o_ref.dtype)

    @pl.when(j >= n_rope_blocks)
    def _():
        o_ref[...] = acc.astype(o_ref.dtype)


def _rope_tables(cfg):
    t = np.arange(cfg.dec_seq)
    row = (t // cfg.grid_w).astype(np.float64)
    col = (t % cfg.grid_w).astype(np.float64)
    lane = np.arange(LANES)
    l64 = lane % cfg.a_qk
    half = cfg.a_qk // 4
    freq = cfg.rope_theta ** (-(lane % half).astype(np.float64) / half)
    pos = np.where((l64 < cfg.a_qk // 2)[None, :], row[:, None], col[:, None])
    ang = pos * freq[None, :]
    first = (lane % (2 * half)) < half
    cos = np.cos(ang)
    sin = np.sin(ang)
    sa = np.where(first[None, :], -sin, 0.0)
    sb = np.where(first[None, :], 0.0, sin)
    return tuple(jnp.asarray(a, dtype=F32) for a in (cos, sa, sb))


def _projection(cfg, x, mod_l, gain, w, *, latent, out_dtype):
    m, d = x.shape
    n = w.shape[1]
    tm = min(cfg.row_tile, m)
    tn = min(1024, cfg.a_width)
    per_seq = cfg.dec_seq // tm if latent else 1
    mod_idx = (lambda i, j: (1 + i // per_seq, 0, 0)) if latent else (lambda i, j: (0, 0, 0))
    in_specs = [
        pl.BlockSpec((tm, d), lambda i, j: (i, 0)),
        pl.BlockSpec((1, 8, d), mod_idx),
        pl.BlockSpec((1, d), lambda i, j: (0, 0)),
        pl.BlockSpec((d, tn), lambda i, j: (0, j)),
    ]
    args = [x, mod_l, gain, w]
    if latent:
        tab_spec = pl.BlockSpec((tm, LANES), lambda i, j: (i % per_seq, 0))
        in_specs += [tab_spec, tab_spec, tab_spec]
        args += list(_rope_tables(cfg))
    kern = functools.partial(_proj_kernel, rope=latent, n_rope_blocks=2 * cfg.a_width // tn,
                             eps=cfg.norm_eps)
    return pl.pallas_call(
        kern,
        grid=(m // tm, n // tn),
        in_specs=in_specs,
        out_specs=pl.BlockSpec((tm, tn), lambda i, j: (i, j)),
        out_shape=jax.ShapeDtypeStruct((m, n), out_dtype),
        scratch_shapes=[pltpu.VMEM((tm, d), BF16)],
        compiler_params=_params(("arbitrary", "arbitrary")),
        name="qkv_projection_latent" if latent else "qkv_projection_context",
    )(*args)


def _nt_dot(a, b):
    return lax.dot_general(a, b, (((1,), (1,)), ((), ())), preferred_element_type=F32)


def _diff_attn_kernel(*refs, has_ctx, lam_init, eps, qk):
    if has_ctx:
        lam_ref, gain_ref, q_ref, k_ref, v_ref, ck_ref, cv_ref, o_ref = refs
    else:
        lam_ref, gain_ref, q_ref, k_ref, v_ref, o_ref = refs
    lv = lam_ref[...]
    s1 = jnp.sum(lv[0:1] * lv[1:2], axis=-1, keepdims=True)
    s2 = jnp.sum(lv[2:3] * lv[3:4], axis=-1, keepdims=True)
    lam = jnp.exp(s1) - jnp.exp(s2) + lam_init

    q = q_ref[...].astype(F32) * (qk ** -0.5)
    tq = q.shape[0]
    lane = lax.broadcasted_iota(jnp.int32, q.shape, 1)
    qs = jnp.concatenate([jnp.where(lane < qk, q, 0.0), jnp.where(lane >= qk, q, 0.0)],
                         axis=0).astype(BF16)
    s_new = _nt_dot(qs, k_ref[...].astype(BF16))
    m = jnp.max(s_new, axis=-1, keepdims=True)
    if has_ctx:
        s_ctx = _nt_dot(qs, ck_ref[0].astype(BF16))
        m = jnp.maximum(m, jnp.max(s_ctx, axis=-1, keepdims=True))
    p_new = jnp.exp(s_new - m)
    den = jnp.sum(p_new, axis=-1, keepdims=True)
    if has_ctx:
        p_ctx = jnp.exp(s_ctx - m)
        den = den + jnp.sum(p_ctx, axis=-1, keepdims=True)
    inv = 1.0 / den
    c1 = inv[:tq]
    c2 = lam * inv[tq:]
    pc = (p_new[:tq] * c1 - p_new[tq:] * c2).astype(BF16)
    o = jnp.dot(pc, v_ref[...].astype(BF16), preferred_element_type=F32)
    if has_ctx:
        pcc = (p_ctx[:tq] * c1 - p_ctx[tq:] * c2).astype(BF16)
        o = o + jnp.dot(pcc, cv_ref[0].astype(BF16), preferred_element_type=F32)
    ms = jnp.mean(o * o, axis=-1, keepdims=True)
    o = (o * lax.rsqrt(ms + eps)) * gain_ref[...]
    o_ref[...] = (o * (1.0 - lam_init)).astype(o_ref.dtype)


def _diff_attention(cfg, proj, lam_pack, gain, n_batch, seq_len, lam_init, ctx=None):
    hd = cfg.a_dim
    nh = cfg.a_heads
    tq = min(cfg.q_tile, seq_len)
    nq = seq_len // tq
    in_specs = [
        pl.BlockSpec((8, LANES), lambda b, h, qi: (0, 0)),
        pl.BlockSpec((1, hd), lambda b, h, qi: (0, 0)),
        pl.BlockSpec((tq, hd), lambda b, h, qi: (b * nq + qi, h)),
        pl.BlockSpec((seq_len, hd), lambda b, h, qi: (b, nh + h)),
        pl.BlockSpec((seq_len, hd), lambda b, h, qi: (b, 2 * nh + h)),
    ]
    args = [lam_pack, gain, proj, proj, proj]
    if ctx is not None:
        ck, cv = ctx
        past = ck.shape[1]
        cspec = pl.BlockSpec((1, past, hd), lambda b, h, qi: (b, 0, h))
        in_specs += [cspec, cspec]
        args += [ck, cv]
    kern = functools.partial(_diff_attn_kernel, has_ctx=ctx is not None, lam_init=lam_init,
                             eps=cfg.subln_eps, qk=cfg.a_qk)
    return pl.pallas_call(
        kern,
        grid=(n_batch, nh, nq),
        in_specs=in_specs,
        out_specs=pl.BlockSpec((tq, hd), lambda b, h, qi: (b * nq + qi, h)),
        out_shape=jax.ShapeDtypeStruct((n_batch * seq_len, cfg.a_width), BF16),
        compiler_params=_params(("arbitrary", "arbitrary", "arbitrary")),
        name="diff_attention_latent" if ctx is not None else "diff_attention_context",
    )(*args)


def _soft_attn_kernel(q_ref, k_ref, v_ref, o_ref, *, scale):
    s = _nt_dot(q_ref[...].astype(BF16), k_ref[...].astype(BF16)) * scale
    m = jnp.max(s, axis=-1, keepdims=True)
    p = jnp.exp(s - m)
    p = (p * (1.0 / jnp.sum(p, axis=-1, keepdims=True))).astype(BF16)
    o_ref[...] = jnp.dot(p, v_ref[...].astype(BF16), preferred_element_type=F32).astype(o_ref.dtype)


def _soft_attention(cfg, proj, n_batch, seq_len):
    hd = cfg.b_dim
    nh = cfg.b_heads
    base = 3 * cfg.a_width // hd
    return pl.pallas_call(
        functools.partial(_soft_attn_kernel, scale=hd ** -0.5),
        grid=(n_batch, nh),
        in_specs=[
            pl.BlockSpec((seq_len, hd), lambda b, h: (b, base + h)),
            pl.BlockSpec((seq_len, hd), lambda b, h: (b, base + nh + h)),
            pl.BlockSpec((seq_len, hd), lambda b, h: (b, base + 2 * nh + h)),
        ],
        out_specs=pl.BlockSpec((seq_len, hd), lambda b, h: (b, h)),
        out_shape=jax.ShapeDtypeStruct((n_batch * seq_len, cfg.b_width), BF16),
        compiler_params=_params(("arbitrary", "arbitrary")),
        name="softmax_attention_context",
    )(proj, proj, proj)


NA_GROUP_ROWS = 4


def _na_geometry(cfg):
    rows = cfg.dec_seq // cfg.grid_w
    kr = min(cfg.na_win_r, rows)
    grp = min(NA_GROUP_ROWS, rows)
    union = min(rows, kr + grp - 1 + (kr + grp - 1) % 2)
    starts = []
    for g in range(rows // grp):
        rs0 = min(max(g * grp - kr // 2, 0), rows - kr)
        starts.append(min(rs0, rows - union))
    return rows, kr, grp, union, starts


def _na_kernel(ws_ref, q_ref, k_ref, v_ref, ck_ref, cv_ref, bias_ref, o_ref, sctx, pctx, oacc, *,
               n_groups, gq, uk, gw, scale):
    ck = ck_ref[0].astype(BF16)
    cv = cv_ref[0].astype(BF16)
    sctx[...] = _nt_dot(q_ref[...], ck) * scale

    def body(g, carry):
        q0 = pl.multiple_of(g * gq, gq)
        k0 = pl.multiple_of(ws_ref[g] * gw, gw)
        q = q_ref[pl.ds(q0, gq), :]
        s = _nt_dot(q, k_ref[pl.ds(k0, uk), :]) * scale + bias_ref[g, 0]
        sc = sctx[pl.ds(q0, gq), :]
        m = jnp.maximum(jnp.max(s, axis=-1, keepdims=True), jnp.max(sc, axis=-1, keepdims=True))
        p = jnp.exp(s - m)
        pc = jnp.exp(sc - m)
        inv = 1.0 / (jnp.sum(p, axis=-1, keepdims=True) + jnp.sum(pc, axis=-1, keepdims=True))
        oacc[pl.ds(q0, gq), :] = jnp.dot((p * inv).astype(BF16), v_ref[pl.ds(k0, uk), :],
                                         preferred_element_type=F32)
        pctx[pl.ds(q0, gq), :] = (pc * inv).astype(BF16)
        return carry

    lax.fori_loop(0, n_groups, body, 0, unroll=2)
    o_ref[...] = (oacc[...] + jnp.dot(pctx[...], cv, preferred_element_type=F32)).astype(o_ref.dtype)


def _na_bias_table(cfg, rpb):
    w = cfg.grid_w
    rows, kr, grp, union, starts = _na_geometry(cfg)
    qc = np.arange(w)
    kc = np.arange(w)
    cs = np.clip(qc - cfg.na_win_c // 2, 0, w - cfg.na_win_c)
    col_mask = (kc[None, :] >= cs[:, None]) & (kc[None, :] < cs[:, None] + cfg.na_win_c)
    col_idx = np.clip(kc[None, :] - qc[:, None] + cfg.na_win_c - 1, 0, 2 * cfg.na_win_c - 2)
    n_c = 2 * cfg.na_win_c - 1
    onehot = (col_idx[None] == np.arange(n_c)[:, None, None]) & col_mask[None]
    t = jnp.einsum('hrc,cqk->hrqk', rpb.astype(F32), jnp.asarray(onehot, dtype=F32),
                   precision=lax.Precision.HIGHEST)
    t = jnp.where(jnp.asarray(col_mask)[None, None], t, NEG_BIG)
    n_r = 2 * cfg.na_win_r - 1
    t = jnp.concatenate([t, jnp.full((cfg.b_heads, 1, w, w), NEG_BIG, F32)], axis=1)
    plane = np.full((len(starts), grp, union), n_r, np.int32)
    for g, ws in enumerate(starts):
        for a in range(grp):
            r = g * grp + a
            rs = min(max(r - kr // 2, 0), rows - kr)
            for j in range(union):
                if rs <= ws + j < rs + kr:
                    plane[g, a, j] = ws + j - r + cfg.na_win_r - 1
    t = jnp.take(t, jnp.asarray(plane.reshape(-1)), axis=1)
    t = t.reshape(cfg.b_heads, len(starts), grp, union, w, w)
    t = jnp.transpose(t, (1, 0, 2, 4, 3, 5))
    return t.reshape(len(starts), cfg.b_heads, grp * w, union * w)


def _na_attention(cfg, proj, ck, cv, rpb):
    hd = cfg.b_dim
    nh = cfg.b_heads
    n = cfg.dec_seq
    rows, kr, grp, union, starts = _na_geometry(cfg)
    n_groups = len(starts)
    gq, uk = grp * cfg.grid_w, union * cfg.grid_w
    base = 3 * cfg.a_width // hd
    past = ck.shape[1]
    bias = _na_bias_table(cfg, rpb)
    kern = functools.partial(_na_kernel, n_groups=n_groups, gq=gq, uk=uk, gw=cfg.grid_w, scale=hd ** -0.5)
    cspec = pl.BlockSpec((1, past, hd), lambda h, b, ws: (b, 0, h))
    return pl.pallas_call(
        kern,
        grid_spec=pltpu.PrefetchScalarGridSpec(
            num_scalar_prefetch=1,
            grid=(nh, cfg.dec_batch),
            in_specs=[
                pl.BlockSpec((n, hd), lambda h, b, ws: (b, base + h)),
                pl.BlockSpec((n, hd), lambda h, b, ws: (b, base + nh + h)),
                pl.BlockSpec((n, hd), lambda h, b, ws: (b, base + 2 * nh + h)),
                cspec, cspec,
                pl.BlockSpec((n_groups, 1, gq, uk), lambda h, b, ws: (0, h, 0, 0)),
            ],
            out_specs=pl.BlockSpec((n, hd), lambda h, b, ws: (b, h)),
            scratch_shapes=[pltpu.VMEM((n, past), F32), pltpu.VMEM((n, past), BF16), pltpu.VMEM((n, hd), F32)],
        ),
        out_shape=jax.ShapeDtypeStruct((cfg.n_s, cfg.b_width), BF16),
        compiler_params=_params(("arbitrary", "arbitrary")),
        name="neighbourhood_attention",
    )(jnp.asarray(np.asarray(starts, np.int32)), proj, proj, proj, ck, cv, bias)


def _linres_kernel(*refs, n_parts, n_pb, x_split):
    i = pl.program_id(0)
    is_p = i < n_pb
    pos = 0
    acc = None
    w_ref = refs[2 * n_parts]
    k0 = 0
    for p in range(n_parts):
        a_p, a_s = refs[2 * p], refs[2 * p + 1]
        a = jnp.where(is_p, a_p[...], a_s[...])
        kk = a.shape[1]
        part = jnp.dot(a, w_ref[k0:k0 + kk, :], preferred_element_type=F32)
        acc = part if acc is None else acc + part
        k0 += kk
    pos = 2 * n_parts + 1
    if x_split:
        x = jnp.where(is_p, refs[pos][...], refs[pos + 1][...])
        pos += 2
    else:
        x = refs[pos][...]
        pos += 1
    mod_ref, o_ref = refs[pos], refs[pos + 1]
    o_ref[...] = x + mod_ref[0, 2:3, :] * acc


def _linear_residual(cfg, parts, w, x, mod_l):
    d = cfg.d_model
    tm = min(256, cfg.tm)
    n_pb = cfg.n_p // tm
    n_sb = cfg.n_s // tm
    p_idx = lambda i: (jnp.minimum(i, n_pb - 1), 0)
    s_idx = lambda i: (jnp.maximum(i - n_pb, 0), 0)
    in_specs, args = [], []
    for a_p, a_s in parts:
        kk = a_p.shape[1]
        in_specs += [pl.BlockSpec((tm, kk), p_idx), pl.BlockSpec((tm, kk), s_idx)]
        args += [a_p, a_s]
    in_specs.append(pl.BlockSpec(w.shape, lambda i: (0, 0)))
    args.append(w)
    x_split = isinstance(x, tuple)
    if x_split:
        in_specs += [pl.BlockSpec((tm, d), p_idx), pl.BlockSpec((tm, d), s_idx)]
        args += list(x)
    else:
        in_specs.append(pl.BlockSpec((tm, d), lambda i: (i, 0)))
        args.append(x)
    mi = _merged_mod_index(cfg, tm)
    in_specs.append(pl.BlockSpec((1, 8, d), lambda i: (mi(i), 0, 0)))
    args.append(mod_l)
    kern = functools.partial(_linres_kernel, n_parts=len(parts), n_pb=n_pb, x_split=x_split)
    return pl.pallas_call(
        kern,
        grid=(n_pb + n_sb,),
        in_specs=in_specs,
        out_specs=pl.BlockSpec((tm, d), lambda i: (i, 0)),
        out_shape=jax.ShapeDtypeStruct((cfg.n_tok, d), F32),
        compiler_params=_params(("arbitrary",)),
        name="linear_gated_residual",
    )(*args)


def _dft_chan_kernel(x_ref, mod_ref, g_ref, cs_ref, y_ref, *, groups, eps):
    h = _norm_mod(x_ref[...], g_ref[...], mod_ref[0, 0:1, :], mod_ref[0, 1:2, :], eps).astype(BF16)
    gd = h.shape[1] // groups
    for g in range(groups):
        r = jnp.dot(h[:, g * gd:(g + 1) * gd], cs_ref[...], preferred_element_type=F32)
        y_ref[0, :, g * gd:(g + 1) * gd] = r[:, :gd].astype(BF16)
        y_ref[1, :, g * gd:(g + 1) * gd] = r[:, gd:].astype(BF16)


def _dft_mats(n):
    k = np.arange(n)
    ang = 2.0 * np.pi * ((k[:, None] * k[None, :]) % n) / n
    return np.cos(ang), np.sin(ang)


def _dft_channels(cfg, x, mod_l, gain):
    d = cfg.d_model
    gd = d // cfg.f_groups
    tm = cfg.tm
    c, s = _dft_mats(gd)
    cs = jnp.asarray(np.concatenate([c, s], axis=1), dtype=F32).astype(BF16)
    mi = _merged_mod_index(cfg, tm)
    return pl.pallas_call(
        functools.partial(_dft_chan_kernel, groups=cfg.f_groups, eps=cfg.norm_eps),
        grid=(cfg.n_tok // tm,),
        in_specs=[
            pl.BlockSpec((tm, d), lambda i: (i, 0)),
            pl.BlockSpec((1, 8, d), lambda i: (mi(i), 0, 0)),
            pl.BlockSpec((1, d), lambda i: (0, 0)),
            pl.BlockSpec((gd, 2 * gd), lambda i: (0, 0)),
        ],
        out_specs=pl.BlockSpec((2, tm, d), lambda i: (0, i, 0)),
        out_shape=jax.ShapeDtypeStruct((2, cfg.n_tok, d), BF16),
        compiler_params=_params(("arbitrary",)),
        name="dft_channels",
    )(x, mod_l, gain, cs)


def _dft_seq_kernel(w_ref, y_ref, o_ref, *, scale):
    acc = (jnp.dot(w_ref[0], y_ref[0], preferred_element_type=F32)
           + jnp.dot(w_ref[1], y_ref[1], preferred_element_type=F32))
    o_ref[...] = (acc * scale).astype(o_ref.dtype)


def _dft_sequence(cfg, y, n_batch, seq_len, first_block):
    d = cfg.d_model
    c, s = _dft_mats(seq_len)
    wm = jnp.asarray(np.stack([c, -s]), dtype=F32).astype(BF16)
    tml = min(1024, seq_len)
    tn = min(512, d)
    nm = seq_len // tml
    scale = 1.0 / math.sqrt(seq_len * (d // cfg.f_groups))
    return pl.pallas_call(
        functools.partial(_dft_seq_kernel, scale=scale),
        grid=(n_batch, nm, d // tn),
        in_specs=[
            pl.BlockSpec((2, tml, seq_len), lambda b, mi, j: (0, mi, 0)),
            pl.BlockSpec((2, seq_len, tn), lambda b, mi, j: (0, first_block + b, j)),
        ],
        out_specs=pl.BlockSpec((tml, tn), lambda b, mi, j: (b * nm + mi, j)),
        out_shape=jax.ShapeDtypeStruct((n_batch * seq_len, d), BF16),
        compiler_params=_params(("arbitrary", "arbitrary", "arbitrary")),
        name="dft_sequence_%d" % seq_len,
    )(wm, y)


def _route_kernel(x_ref, mod_ref, g_ref, wr_ref, rb_ref, hp_ref, idx_ref, wt_ref, *, eps, n_exp, per_grp):
    h = _norm_mod(x_ref[...], g_ref[...], mod_ref[0, 3:4, :], mod_ref[0, 4:5, :], eps)
    tm = h.shape[0]
    n_chunk = h.shape[1] // LANES
    for c in range(n_chunk):
        hp_ref[pl.ds(c, tm, stride=n_chunk), :] = h[:, c * LANES:(c + 1) * LANES]

    logits = lax.dot_general(wr_ref[...], h, (((1,), (1,)), ((), ())),
                             precision=lax.Precision.HIGHEST, preferred_element_type=F32)
    scores = jax.nn.sigmoid(logits)
    sel = scores + rb_ref[...]
    n_grp = n_exp // per_grp
    best = None
    gi = None
    for g in range(n_grp):
        v = [sel[g * per_grp + k:g * per_grp + k + 1, :] for k in range(per_grp)]
        gs = None
        for a in range(per_grp):
            for b in range(a + 1, per_grp):
                ps = v[a] + v[b]
                gs = ps if gs is None else jnp.maximum(gs, ps)
        if best is None:
            best, gi = gs, jnp.zeros(gs.shape, jnp.int32)
        else:
            better = gs > best
            gi = jnp.where(better, g, gi)
            best = jnp.where(better, gs, best)
    row = lax.broadcasted_iota(jnp.int32, sel.shape, 0)
    masked = jnp.where(row // per_grp == gi, sel, -jnp.inf)
    m1 = jnp.max(masked, axis=0, keepdims=True)
    i1 = jnp.min(jnp.where(masked == m1, row, n_exp), axis=0, keepdims=True)
    masked2 = jnp.where(row == i1, -jnp.inf, masked)
    m2 = jnp.max(masked2, axis=0, keepdims=True)
    i2 = jnp.min(jnp.where(masked2 == m2, row, n_exp), axis=0, keepdims=True)
    w1 = jnp.sum(jnp.where(row == i1, scores, 0.0), axis=0, keepdims=True)
    w2 = jnp.sum(jnp.where(row == i2, scores, 0.0), axis=0, keepdims=True)
    inv = 1.0 / (w1 + w2)
    idx_ref[...] = jnp.concatenate([i1, i2], axis=0)
    wt_ref[...] = jnp.concatenate([w1 * inv, w2 * inv], axis=0)


def _route(cfg, x, mod_l, gain, w_router_t, router_bias):
    d = cfg.d_model
    tm = cfg.tm
    t = cfg.n_tok
    mi = _merged_mod_index(cfg, tm)
    kern = functools.partial(_route_kernel, eps=cfg.norm_eps, n_exp=cfg.n_experts,
                             per_grp=cfg.n_experts // cfg.n_groups)
    return pl.pallas_call(
        kern,
        grid=(t // tm,),
        in_specs=[
            pl.BlockSpec((tm, d), lambda i: (i, 0)),
            pl.BlockSpec((1, 8, d), lambda i: (mi(i), 0, 0)),
            pl.BlockSpec((1, d), lambda i: (0, 0)),
            pl.BlockSpec((cfg.n_experts, d), lambda i: (0, 0)),
            pl.BlockSpec((cfg.n_experts, 1), lambda i: (0, 0)),
        ],
        out_specs=[
            pl.BlockSpec((tm * (d // LANES), LANES), lambda i: (i, 0)),
            pl.BlockSpec((2, tm), lambda i: (0, i)),
            pl.BlockSpec((2, tm), lambda i: (0, i)),
        ],
        out_shape=[
            jax.ShapeDtypeStruct((t * (d // LANES), LANES), F32),
            jax.ShapeDtypeStruct((2, t), jnp.int32),
            jax.ShapeDtypeStruct((2, t), F32),
        ],
        compiler_params=_params(("arbitrary",)),
        name="moe_route",
    )(x, mod_l, gain, w_router_t, router_bias.reshape(cfg.n_experts, 1))


def _plan(cfg, idx):
    t = cfg.n_tok
    eb = cfg.e_block
    ne = cfg.n_experts
    n_blocks = -(-(2 * t + ne * (eb - 1)) // eb)
    e_flat = idx.reshape(-1)
    onehot = (e_flat[:, None] == jnp.arange(ne, dtype=jnp.int32)[None, :]).astype(jnp.int32)
    csum = jnp.cumsum(onehot, axis=0)
    rank = jnp.sum(onehot * (csum - 1), axis=1)
    counts = csum[-1]
    padded = (counts + eb - 1) // eb * eb
    pad_end = jnp.cumsum(padded)
    pad_start = pad_end - padded
    pos = (pad_start[e_flat] + rank).astype(jnp.int32).reshape(2, t)
    n_used = (pad_end[-1] // eb).astype(jnp.int32)
    blk = jnp.arange(n_blocks, dtype=jnp.int32)
    blk = jnp.minimum(blk, n_used - 1)
    first_e = jnp.sum((pad_end[None, :] <= (blk * eb)[:, None]).astype(jnp.int32), axis=1)
    block_e = jnp.minimum(first_e, ne - 1).astype(jnp.int32)
    return pos, block_e, n_used.reshape(1), pad_end.astype(jnp.int32), n_blocks


def _dispatch_kernel(pend_ref, pos_ref, hp_ref, xr_ref, zbuf, sem, zsem, *, n_exp, eb, spt):
    i = pl.program_id(0)

    @pl.when(i == 0)
    def _():
        zbuf[...] = jnp.zeros(zbuf.shape, zbuf.dtype)
        for e in range(n_exp):
            start = pl.multiple_of(jnp.maximum(pend_ref[e] - eb, 0) * spt, eb * spt)
            cp = pltpu.make_async_copy(zbuf, xr_ref.at[pl.ds(start, eb * spt)], zsem)
            cp.start()
            cp.wait()
        n_used = pend_ref[n_exp - 1] // eb
        n_blocks = xr_ref.shape[0] // (eb * spt)
        for e in range(n_exp):
            @pl.when(n_used + e < n_blocks)
            def _():
                start = pl.multiple_of((n_used + e) * (eb * spt), eb * spt)
                cp = pltpu.make_async_copy(zbuf, xr_ref.at[pl.ds(start, eb * spt)], zsem)
                cp.start()
                cp.wait()

    rows = pos_ref.shape[2]
    base = i * rows

    def start(r, c):
        src = hp_ref.at[pl.ds(pl.multiple_of((base + r) * spt, spt), spt)]
        for k in range(2):
            dst = xr_ref.at[pl.ds(pl.multiple_of(pos_ref[0, k, r] * spt, spt), spt)]
            pltpu.make_async_copy(src, dst, sem.at[k]).start()
        return c

    lax.fori_loop(0, rows, start, 0, unroll=8)

    def wait_step():
        for k in range(2):
            pltpu.make_async_copy(hp_ref.at[pl.ds(0, rows * spt)], xr_ref.at[pl.ds(0, rows * spt)],
                                  sem.at[k]).wait()

    @pl.when(i > 0)
    def _():
        wait_step()

    @pl.when(i == pl.num_programs(0) - 1)
    def _():
        wait_step()


def _dispatch(cfg, hp, pos_blocks, pad_end, n_rows):
    tb = pos_blocks.shape[2]
    spt = cfg.d_model // LANES
    kern = functools.partial(_dispatch_kernel, n_exp=cfg.n_experts, eb=cfg.e_block, spt=spt)
    return pl.pallas_call(
        kern,
        grid_spec=pltpu.PrefetchScalarGridSpec(
            num_scalar_prefetch=1,
            grid=(cfg.n_tok // tb,),
            in_specs=[
                pl.BlockSpec((1, 2, tb), lambda i, pe: (i, 0, 0), memory_space=pltpu.SMEM),
                pl.BlockSpec(memory_space=pl.ANY),
            ],
            out_specs=pl.BlockSpec(memory_space=pl.ANY),
            scratch_shapes=[
                pltpu.VMEM((cfg.e_block * spt, LANES), F32),
                pltpu.SemaphoreType.DMA((2,)),
                pltpu.SemaphoreType.DMA(()),
            ],
        ),
        out_shape=jax.ShapeDtypeStruct((n_rows * spt, LANES), F32),
        compiler_params=_params(("arbitrary",)),
        name="moe_dispatch",
    )(pad_end, pos_blocks, hp)


def _expert_kernel(be_ref, nu_ref, xp_ref, wg_ref, wu_ref, wd_ref, y_ref):
    b = pl.program_id(0)

    @pl.when(b < nu_ref[0])
    def _():
        d = wg_ref.shape[1]
        n_chunk = d // LANES
        eb = xp_ref.shape[0] // n_chunk
        x = jnp.concatenate([xp_ref[pl.ds(c, eb, stride=n_chunk), :].astype(BF16) for c in range(n_chunk)],
                            axis=1)
        g = jnp.dot(x, wg_ref[0], preferred_element_type=F32)
        u = jnp.dot(x, wu_ref[0], preferred_element_type=F32)
        a = (_silu(g) * u).astype(BF16)
        y = jnp.dot(a, wd_ref[0], preferred_element_type=F32)
        n_chunk = d // LANES
        for c in range(n_chunk):
            y_ref[pl.ds(c, eb, stride=n_chunk), :] = y[:, c * LANES:(c + 1) * LANES]

    @pl.when(b >= nu_ref[0])
    def _():
        y_ref[...] = jnp.zeros(y_ref.shape, y_ref.dtype)


def _experts(cfg, x_rows, block_e, n_used, wg, wu, wd, n_blocks):
    d = cfg.d_model
    f = cfg.d_expert
    eb = cfg.e_block
    n_chunk = d // LANES
    spt = n_chunk
    return pl.pallas_call(
        _expert_kernel,
        grid_spec=pltpu.PrefetchScalarGridSpec(
            num_scalar_prefetch=2,
            grid=(n_blocks,),
            in_specs=[
                pl.BlockSpec((eb * spt, LANES), lambda b, be, nu: (jnp.minimum(b, nu[0] - 1), 0)),
                pl.BlockSpec((1, d, f), lambda b, be, nu: (be[b], 0, 0)),
                pl.BlockSpec((1, d, f), lambda b, be, nu: (be[b], 0, 0)),
                pl.BlockSpec((1, f, d), lambda b, be, nu: (be[b], 0, 0)),
            ],
            out_specs=pl.BlockSpec((eb * n_chunk, LANES), lambda b, be, nu: (b, 0)),
        ),
        out_shape=jax.ShapeDtypeStruct((n_blocks * eb * n_chunk, LANES), F32),
        compiler_params=_params(("arbitrary",)),
        name="moe_experts",
    )(block_e, n_used, x_rows, wg, wu, wd)


def _combine_kernel(pos_ref, nxt_ref, y_ref, x_ref, mod_ref, wt_ref, fg_ref, o_ref, ybuf, sem, *, final, eps):
    i = pl.program_id(0)
    n_steps = pl.num_programs(0)
    rows, d = x_ref.shape
    n_chunk = d // LANES
    slot = i % 2
    per_choice = rows * n_chunk

    def gather(p_ref, s):
        def start(r, c):
            for k in range(2):
                src = y_ref.at[pl.ds(pl.multiple_of(p_ref[0, k, r] * n_chunk, n_chunk), n_chunk)]
                dst = ybuf.at[s, pl.ds(pl.multiple_of(k * per_choice + r * n_chunk, n_chunk), n_chunk)]
                pltpu.make_async_copy(src, dst, sem.at[s]).start()
            return c
        lax.fori_loop(0, rows, start, 0, unroll=8)

    @pl.when(i == 0)
    def _():
        gather(pos_ref, 0)

    @pl.when(i + 1 < n_steps)
    def _():
        gather(nxt_ref, 1 - slot)

    pltpu.make_async_copy(y_ref.at[pl.ds(0, 2 * per_choice)], ybuf.at[slot], sem.at[slot]).wait()

    w = wt_ref[...]
    w0, w1 = w[:, 0:1], w[:, 1:2]
    sumsq = jnp.zeros((rows, 1), F32)
    for c in range(n_chunk):
        cols = slice(c * LANES, (c + 1) * LANES)
        moe = (w0 * ybuf[slot, pl.ds(c, rows, stride=n_chunk), :]
               + w1 * ybuf[slot, pl.ds(per_choice + c, rows, stride=n_chunk), :])
        xc = x_ref[:, cols] + mod_ref[0, 5:6, cols] * moe
        o_ref[:, cols] = xc
        sumsq = sumsq + jnp.sum(xc * xc, axis=-1, keepdims=True)
    if final:
        o_ref[...] = (o_ref[...] * lax.rsqrt(sumsq * (1.0 / d) + eps)) * fg_ref[...]


def _combine(cfg, y, pos_blocks, wts_t, x, mod_l, final_gain, *, row0, n_rows, final):
    d = cfg.d_model
    tb = pos_blocks.shape[2]
    b0 = row0 // tb
    mi = _merged_mod_index(cfg, tb)
    kern = functools.partial(_combine_kernel, final=final, eps=cfg.norm_eps)
    n_steps = n_rows // tb
    return pl.pallas_call(
        kern,
        grid=(n_steps,),
        in_specs=[
            pl.BlockSpec((1, 2, tb), lambda i: (b0 + i, 0, 0), memory_space=pltpu.SMEM),
            pl.BlockSpec((1, 2, tb), lambda i: (b0 + jnp.minimum(i + 1, n_steps - 1), 0, 0),
                         memory_space=pltpu.SMEM),
            pl.BlockSpec(memory_space=pl.ANY),
            pl.BlockSpec((tb, d), lambda i: (b0 + i, 0)),
            pl.BlockSpec((1, 8, d), lambda i: (mi(b0 + i), 0, 0)),
            pl.BlockSpec((tb, 2), lambda i: (b0 + i, 0)),
            pl.BlockSpec((1, d), lambda i: (0, 0)),
        ],
        out_specs=pl.BlockSpec((tb, d), lambda i: (i, 0)),
        out_shape=jax.ShapeDtypeStruct((n_rows, d), F32),
        scratch_shapes=[pltpu.VMEM((2, 2 * tb * (d // LANES), LANES), F32), pltpu.SemaphoreType.DMA((2,))],
        compiler_params=_params(("arbitrary",)),
        name="moe_combine_final" if final else "moe_combine",
    )(pos_blocks, pos_blocks, y, x, mod_l, wts_t, final_gain)


def _moe(cfg, x, mod_l, gain, w_router_t, router_bias, wg, wu, wd, final_gain, final):
    tb = min(256, cfg.tm)
    hp, idx, wts = _route(cfg, x, mod_l, gain, w_router_t, router_bias)
    pos, block_e, n_used, pad_end, n_blocks = _plan(cfg, idx)
    pos_blocks = pos.reshape(2, cfg.n_tok // tb, tb).transpose(1, 0, 2)
    x_rows = _dispatch(cfg, hp, pos_blocks, pad_end, n_blocks * cfg.e_block)
    y = _experts(cfg, x_rows, block_e, n_used, wg, wu, wd, n_blocks)
    wts_t = wts.T
    comb = functools.partial(_combine, cfg, y, pos_blocks, wts_t, x, mod_l, final_gain, final=final)
    if final:
        return comb(row0=0, n_rows=cfg.n_p), comb(row0=cfg.n_p, n_rows=cfg.n_s)
    return comb(row0=0, n_rows=cfg.n_tok)


def _forward(cfg, x_prompt, x_sample, cache_a_k, cache_a_v, cache_b_k, cache_b_v, c, c_ctx, w_ada, b_ada,
             norm1, norm2, final_norm, w_in, w_mix_out, lambda_q1, lambda_k1, lambda_q2, lambda_k2,
             subln_gain, na_rel_bias, w_fourier_out, w_router, router_bias, w_exp_gate, w_exp_up,
             w_exp_down):
    d = cfg.d_model
    xp = x_prompt.reshape(cfg.n_p, d)
    xs = x_sample.reshape(cfg.n_s, d)

    n_cond = -(-cfg.n_mod // 8) * 8
    cond = jnp.concatenate([c_ctx[None, :], c, jnp.zeros((n_cond - cfg.n_mod, d), F32)], axis=0)
    mod = _modulation(cfg, cond, w_ada, b_ada)
    mod = mod.reshape(cfg.depth, n_cond, 6, d)[:, :cfg.n_mod]
    mod = jnp.pad(mod, ((0, 0), (0, 0), (0, 2), (0, 0)))

    w_router_t = w_router.T
    fgain = final_norm.reshape(1, d)
    x = (xp, xs)
    kv_cache = None
    for l in range(cfg.depth):
        j = l // 2
        g1 = norm1[l].reshape(1, d)
        if l % 2 == 0:
            lam_init = 0.8 - 0.6 * math.exp(-0.3 * l)
            lam_pack = jnp.zeros((8, LANES), F32).at[:4, :cfg.a_qk].set(
                jnp.stack([lambda_q1[j], lambda_k1[j], lambda_q2[j], lambda_k2[j]]))
            sgain = subln_gain[j].reshape(1, cfg.a_dim)
            w_in_b = w_in[j].astype(BF16)
            if isinstance(x, tuple):
                x_p, x_s = x
            else:
                x_p, x_s = x[:cfg.n_p], x[cfg.n_p:]
            proj_p = _projection(cfg, x_p, mod[l], g1, w_in_b, latent=False, out_dtype=F32)
            proj_s = _projection(cfg, x_s, mod[l], g1, w_in_b, latent=True, out_dtype=BF16)
            if kv_cache is None:
                kv_cache = []
            aw, bw = cfg.a_width, cfg.b_width
            kv_cache.append((proj_p[:, aw:2 * aw], proj_p[:, 2 * aw:3 * aw],
                             proj_p[:, 3 * aw + bw:3 * aw + 2 * bw], proj_p[:, 3 * aw + 2 * bw:]))
            cak = cache_a_k[:, j].reshape(cfg.dec_batch, cfg.past_len, aw)
            cav = cache_a_v[:, j].reshape(cfg.dec_batch, cfg.past_len, aw)
            cbk = cache_b_k[:, j].reshape(cfg.dec_batch, cfg.past_len, bw)
            cbv = cache_b_v[:, j].reshape(cfg.dec_batch, cfg.past_len, bw)
            a_p = _diff_attention(cfg, proj_p, lam_pack, sgain, cfg.batch, cfg.seq, lam_init)
            b_p = _soft_attention(cfg, proj_p, cfg.batch, cfg.seq)
            a_s = _diff_attention(cfg, proj_s, lam_pack, sgain, cfg.dec_batch, cfg.dec_seq, lam_init,
                                  ctx=(cak, cav))
            b_s = _na_attention(cfg, proj_s, cbk, cbv, na_rel_bias[j])
            x = _linear_residual(cfg, [(a_p, a_s), (b_p, b_s)], w_mix_out[j].astype(BF16), x, mod[l])
        else:
            if isinstance(x, tuple):
                x = jnp.concatenate(x, axis=0)
            y = _dft_channels(cfg, x, mod[l], g1)
            f_p = _dft_sequence(cfg, y, cfg.batch, cfg.seq, 0)
            f_s = _dft_sequence(cfg, y, cfg.dec_batch, cfg.dec_seq, cfg.n_p // cfg.dec_seq)
            x = _linear_residual(cfg, [(f_p, f_s)], w_fourier_out[j].astype(BF16), x, mod[l])
        x = _moe(cfg, x, mod[l], norm2[l].reshape(1, d), w_router_t, router_bias,
                 w_exp_gate[l].astype(BF16), w_exp_up[l].astype(BF16), w_exp_down[l].astype(BF16),
                 fgain, final=(l == cfg.depth - 1))
    y_p, y_s = x
    n_even = (cfg.depth + 1) // 2
    outs = [y_p.reshape(cfg.batch, cfg.seq, d), y_s.reshape(cfg.dec_batch, cfg.dec_seq, d)]
    for t in range(4):
        heads, hd = (cfg.a_heads, cfg.a_dim) if t < 2 else (cfg.b_heads, cfg.b_dim)
        stacked = jnp.stack([kv_cache[jj][t].reshape(cfg.batch, cfg.seq, heads, hd)
                             for jj in range(n_even)], axis=1)
        outs.append(stacked)
    return tuple(outs)


def kernel(x_prompt, x_sample, cache_a_k, cache_a_v, cache_b_k, cache_b_v, c, c_ctx, w_ada, b_ada, norm1, norm2, final_norm, w_in, w_mix_out, lambda_q1, lambda_k1, lambda_q2, lambda_k2, subln_gain, na_rel_bias, w_fourier_out, w_router, router_bias, w_exp_gate, w_exp_up, w_exp_down):
    return _forward(Cfg(), x_prompt, x_sample, cache_a_k, cache_a_v, cache_b_k, cache_b_v, c, c_ctx, w_ada,
                    b_ada, norm1, norm2, final_norm, w_in, w_mix_out, lambda_q1, lambda_k1, lambda_q2,
                    lambda_k2, subln_gain, na_rel_bias, w_fourier_out, w_router, router_bias, w_exp_gate,
                    w_exp_up, w_exp_down)
```

```python
import functools
import math
from typing import NamedTuple

import numpy as np
import jax
import jax.numpy as jnp
from jax import lax
from jax.experimental import pallas as pl
from jax.experimental.pallas import tpu as pltpu

F32 = jnp.float32
BF16 = jnp.bfloat16

LANES = 128
NEG_BIG = -1e30
VMEM_LIMIT = 56 * 1024 * 1024


class Cfg(NamedTuple):
    d_model: int = 2048
    batch: int = 16
    seq: int = 256
    depth: int = 2
    dec_batch: int = 8
    dec_seq: int = 2048
    past_len: int = 256
    grid_w: int = 64
    a_heads: int = 8
    a_qk: int = 64
    b_heads: int = 8
    b_dim: int = 128
    na_win_r: int = 8
    na_win_c: int = 16
    f_groups: int = 4
    n_experts: int = 16
    n_groups: int = 4
    d_expert: int = 1408
    rope_theta: float = 10000.0
    norm_eps: float = 1e-6
    subln_eps: float = 1e-5
    row_tile: int = 512
    q_tile: int = 512
    e_block: int = 256

    @property
    def a_dim(self):
        return 2 * self.a_qk

    @property
    def a_width(self):
        return self.a_heads * self.a_dim

    @property
    def b_width(self):
        return self.b_heads * self.b_dim

    @property
    def n_p(self):
        return self.batch * self.seq

    @property
    def n_s(self):
        return self.dec_batch * self.dec_seq

    @property
    def n_tok(self):
        return self.n_p + self.n_s

    @property
    def tm(self):
        return min(self.row_tile, self.n_p, self.dec_seq)

    @property
    def n_mod(self):
        return 1 + self.dec_batch


def _params(sem):
    return pltpu.CompilerParams(dimension_semantics=sem, vmem_limit_bytes=VMEM_LIMIT)


def _norm_mod(x, g, sh, sc, eps):
    ms = jnp.mean(x * x, axis=-1, keepdims=True)
    y = x * lax.rsqrt(ms + eps)
    return (y * g) * (1.0 + sc) + sh


def _silu(x):
    return x * jax.nn.sigmoid(x)


def _merged_mod_index(cfg, tm):
    def f(i):
        r = i * tm
        return jnp.where(r < cfg.n_p, 0, 1 + (r - cfg.n_p) // cfg.dec_seq)
    return f


def _ada_kernel(cond_ref, w_ref, b_ref, o_ref):
    s = _silu(cond_ref[...])
    s_hi = s.astype(BF16)
    s_lo = (s - s_hi.astype(F32)).astype(BF16)
    lhs = jnp.concatenate([s_hi, s_lo], axis=0)
    r = jnp.dot(lhs, w_ref[0].astype(BF16), preferred_element_type=F32)
    n = s.shape[0]
    o_ref[0] = r[:n] + r[n:] + b_ref[0]


def _modulation(cfg, cond, w_ada, b_ada):
    d = cfg.d_model
    r = cond.shape[0]
    tn = math.gcd(1024, 6 * d)
    return pl.pallas_call(
        _ada_kernel,
        grid=(cfg.depth, 6 * d // tn),
        in_specs=[
            pl.BlockSpec((r, d), lambda l, j: (0, 0)),
            pl.BlockSpec((1, d, tn), lambda l, j: (l, 0, j)),
            pl.BlockSpec((1, 1, tn), lambda l, j: (l, 0, j)),
        ],
        out_specs=pl.BlockSpec((1, r, tn), lambda l, j: (l, 0, j)),
        out_shape=jax.ShapeDtypeStruct((cfg.depth, r, 6 * d), F32),
        compiler_params=_params(("arbitrary", "arbitrary")),
        name="ada_modulation",
    )(cond, w_ada, b_ada.reshape(cfg.depth, 1, 6 * d))


def _proj_kernel(*refs, rope, n_rope_blocks, eps):
    if rope:
        x_ref, mod_ref, g_ref, w_ref, cos_ref, sa_ref, sb_ref, o_ref, h_ref = refs
    else:
        x_ref, mod_ref, g_ref, w_ref, o_ref, h_ref = refs
    j = pl.program_id(1)

    @pl.when(j == 0)
    def _():
        h = _norm_mod(x_ref[...], g_ref[...], mod_ref[0, 0:1, :], mod_ref[0, 1:2, :], eps)
        h_ref[...] = h.astype(BF16)

    acc = jnp.dot(h_ref[...], w_ref[...], preferred_element_type=F32)
    if not rope:
        o_ref[...] = acc.astype(o_ref.dtype)
        return

    @pl.when(j < n_rope_blocks)
    def _():
        cos, sa, sb = cos_ref[...], sa_ref[...], sb_ref[...]
        for c in range(acc.shape[1] // LANES):
            xa = acc[:, c * LANES:(c + 1) * LANES]
            up = pltpu.roll(xa, LANES - 16, 1)
            dn = pltpu.roll(xa, 16, 1)
            o_ref[:, c * LANES:(c + 1) * LANES] = (xa * cos + up * sa + dn * sb).astype(o_ref.dtype)

    @pl.when(j >= n_rope_blocks)
    def _():
        o_ref[...] = acc.astype(o_ref.dtype)


def _rope_tables(cfg):
    t = np.arange(cfg.dec_seq)
    row = (t // cfg.grid_w).astype(np.float64)
    col = (t % cfg.grid_w).astype(np.float64)
    lane = np.arange(LANES)
    l64 = lane % cfg.a_qk
    half = cfg.a_qk // 4
    freq = cfg.rope_theta ** (-(lane % half).astype(np.float64) / half)
    pos = np.where((l64 < cfg.a_qk // 2)[None, :], row[:, None], col[:, None])
    ang = pos * freq[None, :]
    first = (lane % (2 * half)) < half
    cos = np.cos(ang)
    sin = np.sin(ang)
    sa = np.where(first[None, :], -sin, 0.0)
    sb = np.where(first[None, :], 0.0, sin)
    return tuple(jnp.asarray(a, dtype=F32) for a in (cos, sa, sb))


def _projection(cfg, x, mod_l, gain, w, *, latent, out_dtype):
    m, d = x.shape
    n = w.shape[1]
    tm = min(cfg.row_tile, m)
    tn = min(1024, cfg.a_width)
    per_seq = cfg.dec_seq // tm if latent else 1
    mod_idx = (lambda i, j: (1 + i // per_seq, 0, 0)) if latent else (lambda i, j: (0, 0, 0))
    in_specs = [
        pl.BlockSpec((tm, d), lambda i, j: (i, 0)),
        pl.BlockSpec((1, 8, d), mod_idx),
        pl.BlockSpec((1, d), lambda i, j: (0, 0)),
        pl.BlockSpec((d, tn), lambda i, j: (0, j)),
    ]
    args = [x, mod_l, gain, w]
    if latent:
        tab_spec = pl.BlockSpec((tm, LANES), lambda i, j: (i % per_seq, 0))
        in_specs += [tab_spec, tab_spec, tab_spec]
        args += list(_rope_tables(cfg))
    kern = functools.partial(_proj_kernel, rope=latent, n_rope_blocks=2 * cfg.a_width // tn,
                             eps=cfg.norm_eps)
    return pl.pallas_call(
        kern,
        grid=(m // tm, n // tn),
        in_specs=in_specs,
        out_specs=pl.BlockSpec((tm, tn), lambda i, j: (i, j)),
        out_shape=jax.ShapeDtypeStruct((m, n), out_dtype),
        scratch_shapes=[pltpu.VMEM((tm, d), BF16)],
        compiler_params=_params(("arbitrary", "arbitrary")),
        name="qkv_projection_latent" if latent else "qkv_projection_context",
    )(*args)


def _nt_dot(a, b):
    return lax.dot_general(a, b, (((1,), (1,)), ((), ())), preferred_element_type=F32)


DIFF_SUB_TILE = 128


def _diff_attn_kernel(*refs, has_ctx, lam_init, eps, qk):
    if has_ctx:
        lam_ref, gain_ref, q_ref, k_ref, v_ref, ck_ref, cv_ref, o_ref = refs
    else:
        lam_ref, gain_ref, q_ref, k_ref, v_ref, o_ref = refs
    lv = lam_ref[...]
    s1 = jnp.sum(lv[0:1] * lv[1:2], axis=-1, keepdims=True)
    s2 = jnp.sum(lv[2:3] * lv[3:4], axis=-1, keepdims=True)
    lam = jnp.exp(s1) - jnp.exp(s2) + lam_init

    k = k_ref[...].astype(BF16)
    v = v_ref[...].astype(BF16)
    if has_ctx:
        ck = ck_ref[0].astype(BF16)
        cv = cv_ref[0].astype(BF16)
    tq = q_ref.shape[0]
    ts = min(tq, DIFF_SUB_TILE)
    for t in range(tq // ts):
        q = q_ref[t * ts:(t + 1) * ts, :].astype(F32) * (qk ** -0.5 * math.log2(math.e))
        lane = lax.broadcasted_iota(jnp.int32, q.shape, 1)
        qs = jnp.concatenate([jnp.where(lane < qk, q, 0.0), jnp.where(lane >= qk, q, 0.0)],
                             axis=0).astype(BF16)
        s_new = _nt_dot(qs, k)
        m = jnp.max(s_new, axis=-1, keepdims=True)
        if has_ctx:
            s_ctx = _nt_dot(qs, ck)
            m = jnp.maximum(m, jnp.max(s_ctx, axis=-1, keepdims=True))
        p_new = jnp.exp2(s_new - m)
        den = jnp.sum(p_new, axis=-1, keepdims=True)
        o2 = jnp.dot(p_new.astype(BF16), v, preferred_element_type=F32)
        if has_ctx:
            p_ctx = jnp.exp2(s_ctx - m)
            den = den + jnp.sum(p_ctx, axis=-1, keepdims=True)
            o2 = o2 + jnp.dot(p_ctx.astype(BF16), cv, preferred_element_type=F32)
        inv = 1.0 / den
        o = o2[:ts] * inv[:ts] - o2[ts:] * (lam * inv[ts:])
        ms = jnp.mean(o * o, axis=-1, keepdims=True)
        o = (o * lax.rsqrt(ms + eps)) * gain_ref[...]
        o_ref[t * ts:(t + 1) * ts, :] = (o * (1.0 - lam_init)).astype(o_ref.dtype)


def _diff_attention(cfg, proj, lam_pack, gain, n_batch, seq_len, lam_init, ctx=None):
    hd = cfg.a_dim
    nh = cfg.a_heads
    tq = min(cfg.q_tile, seq_len)
    nq = seq_len // tq
    in_specs = [
        pl.BlockSpec((8, LANES), lambda b, h, qi: (0, 0)),
        pl.BlockSpec((1, hd), lambda b, h, qi: (0, 0)),
        pl.BlockSpec((tq, hd), lambda b, h, qi: (b * nq + qi, h)),
        pl.BlockSpec((seq_len, hd), lambda b, h, qi: (b, nh + h)),
        pl.BlockSpec((seq_len, hd), lambda b, h, qi: (b, 2 * nh + h)),
    ]
    args = [lam_pack, gain, proj, proj, proj]
    if ctx is not None:
        ck, cv = ctx
        past = ck.shape[1]
        cspec = pl.BlockSpec((1, past, hd), lambda b, h, qi: (b, 0, h))
        in_specs += [cspec, cspec]
        args += [ck, cv]
    kern = functools.partial(_diff_attn_kernel, has_ctx=ctx is not None, lam_init=lam_init,
                             eps=cfg.subln_eps, qk=cfg.a_qk)
    return pl.pallas_call(
        kern,
        grid=(n_batch, nh, nq),
        in_specs=in_specs,
        out_specs=pl.BlockSpec((tq, hd), lambda b, h, qi: (b * nq + qi, h)),
        out_shape=jax.ShapeDtypeStruct((n_batch * seq_len, cfg.a_width), BF16),
        compiler_params=_params(("arbitrary", "arbitrary", "arbitrary")),
        name="diff_attention_latent" if ctx is not None else "diff_attention_context",
    )(*args)


def _soft_attn_kernel(q_ref, k_ref, v_ref, o_ref, *, scale):
    s = _nt_dot(q_ref[...].astype(BF16), k_ref[...].astype(BF16)) * (scale * math.log2(math.e))
    m = jnp.max(s, axis=-1, keepdims=True)
    p = jnp.exp2(s - m)
    inv = 1.0 / jnp.sum(p, axis=-1, keepdims=True)
    o = jnp.dot(p.astype(BF16), v_ref[...].astype(BF16), preferred_element_type=F32) * inv
    o_ref[...] = o.astype(o_ref.dtype)


def _soft_attention(cfg, proj, n_batch, seq_len):
    hd = cfg.b_dim
    nh = cfg.b_heads
    base = 3 * cfg.a_width // hd
    return pl.pallas_call(
        functools.partial(_soft_attn_kernel, scale=hd ** -0.5),
        grid=(n_batch, nh),
        in_specs=[
            pl.BlockSpec((seq_len, hd), lambda b, h: (b, base + h)),
            pl.BlockSpec((seq_len, hd), lambda b, h: (b, base + nh + h)),
            pl.BlockSpec((seq_len, hd), lambda b, h: (b, base + 2 * nh + h)),
        ],
        out_specs=pl.BlockSpec((seq_len, hd), lambda b, h: (b, h)),
        out_shape=jax.ShapeDtypeStruct((n_batch * seq_len, cfg.b_width), BF16),
        compiler_params=_params(("arbitrary", "arbitrary")),
        name="softmax_attention_context",
    )(proj, proj, proj)


NA_GROUP_ROWS = 4


def _na_geometry(cfg):
    rows = cfg.dec_seq // cfg.grid_w
    kr = min(cfg.na_win_r, rows)
    grp = min(NA_GROUP_ROWS, rows)
    union = min(rows, kr + grp - 1 + (kr + grp - 1) % 2)
    starts = []
    for g in range(rows // grp):
        rs0 = min(max(g * grp - kr // 2, 0), rows - kr)
        starts.append(min(rs0, rows - union))
    return rows, kr, grp, union, starts


def _na_kernel(ws_ref, q_ref, k_ref, v_ref, ck_ref, cv_ref, bias_ref, o_ref, sctx, pctx, oacc, *,
               n_groups, gq, uk, gw, scale):
    ck = ck_ref[0].astype(BF16)
    cv = cv_ref[0].astype(BF16)
    sctx[...] = _nt_dot(q_ref[...], ck) * scale

    def body(g, carry):
        q0 = pl.multiple_of(g * gq, gq)
        k0 = pl.multiple_of(ws_ref[g] * gw, gw)
        q = q_ref[pl.ds(q0, gq), :]
        s = _nt_dot(q, k_ref[pl.ds(k0, uk), :]) * scale + bias_ref[g, 0]
        sc = sctx[pl.ds(q0, gq), :]
        m = jnp.maximum(jnp.max(s, axis=-1, keepdims=True), jnp.max(sc, axis=-1, keepdims=True))
        p = jnp.exp2(s - m)
        pc = jnp.exp2(sc - m)
        inv = 1.0 / (jnp.sum(p, axis=-1, keepdims=True) + jnp.sum(pc, axis=-1, keepdims=True))
        oacc[pl.ds(q0, gq), :] = jnp.dot(p.astype(BF16), v_ref[pl.ds(k0, uk), :],
                                         preferred_element_type=F32) * inv
        pctx[pl.ds(q0, gq), :] = (pc * inv).astype(BF16)
        return carry

    lax.fori_loop(0, n_groups, body, 0, unroll=2)
    o_ref[...] = (oacc[...] + jnp.dot(pctx[...], cv, preferred_element_type=F32)).astype(o_ref.dtype)


def _na_bias_table(cfg, rpb):
    w = cfg.grid_w
    rows, kr, grp, union, starts = _na_geometry(cfg)
    qc = np.arange(w)
    kc = np.arange(w)
    cs = np.clip(qc - cfg.na_win_c // 2, 0, w - cfg.na_win_c)
    col_mask = (kc[None, :] >= cs[:, None]) & (kc[None, :] < cs[:, None] + cfg.na_win_c)
    col_idx = np.clip(kc[None, :] - qc[:, None] + cfg.na_win_c - 1, 0, 2 * cfg.na_win_c - 2)
    n_c = 2 * cfg.na_win_c - 1
    onehot = (col_idx[None] == np.arange(n_c)[:, None, None]) & col_mask[None]
    t = jnp.einsum('hrc,cqk->hrqk', rpb.astype(F32), jnp.asarray(onehot, dtype=F32),
                   precision=lax.Precision.HIGHEST)
    t = jnp.where(jnp.asarray(col_mask)[None, None], t * math.log2(math.e), NEG_BIG)
    n_r = 2 * cfg.na_win_r - 1
    t = jnp.concatenate([t, jnp.full((cfg.b_heads, 1, w, w), NEG_BIG, F32)], axis=1)
    plane = np.full((len(starts), grp, union), n_r, np.int32)
    for g, ws in enumerate(starts):
        for a in range(grp):
            r = g * grp + a
            rs = min(max(r - kr // 2, 0), rows - kr)
            for j in range(union):
                if rs <= ws + j < rs + kr:
                    plane[g, a, j] = ws + j - r + cfg.na_win_r - 1
    t = jnp.take(t, jnp.asarray(plane.reshape(-1)), axis=1)
    t = t.reshape(cfg.b_heads, len(starts), grp, union, w, w)
    t = jnp.transpose(t, (1, 0, 2, 4, 3, 5))
    return t.reshape(len(starts), cfg.b_heads, grp * w, union * w)


def _na_attention(cfg, proj, ck, cv, rpb):
    hd = cfg.b_dim
    nh = cfg.b_heads
    n = cfg.dec_seq
    rows, kr, grp, union, starts = _na_geometry(cfg)
    n_groups = len(starts)
    gq, uk = grp * cfg.grid_w, union * cfg.grid_w
    base = 3 * cfg.a_width // hd
    past = ck.shape[1]
    bias = _na_bias_table(cfg, rpb)
    kern = functools.partial(_na_kernel, n_groups=n_groups, gq=gq, uk=uk, gw=cfg.grid_w,
                             scale=hd ** -0.5 * math.log2(math.e))
    cspec = pl.BlockSpec((1, past, hd), lambda h, b, ws: (b, 0, h))
    return pl.pallas_call(
        kern,
        grid_spec=pltpu.PrefetchScalarGridSpec(
            num_scalar_prefetch=1,
            grid=(nh, cfg.dec_batch),
            in_specs=[
                pl.BlockSpec((n, hd), lambda h, b, ws: (b, base + h)),
                pl.BlockSpec((n, hd), lambda h, b, ws: (b, base + nh + h)),
                pl.BlockSpec((n, hd), lambda h, b, ws: (b, base + 2 * nh + h)),
                cspec, cspec,
                pl.BlockSpec((n_groups, 1, gq, uk), lambda h, b, ws: (0, h, 0, 0)),
            ],
            out_specs=pl.BlockSpec((n, hd), lambda h, b, ws: (b, h)),
            scratch_shapes=[pltpu.VMEM((n, past), F32), pltpu.VMEM((n, past), BF16), pltpu.VMEM((n, hd), F32)],
        ),
        out_shape=jax.ShapeDtypeStruct((cfg.n_s, cfg.b_width), BF16),
        compiler_params=_params(("arbitrary", "arbitrary")),
        name="neighbourhood_attention",
    )(jnp.asarray(np.asarray(starts, np.int32)), proj, proj, proj, ck, cv, bias)


def _linres_kernel(*refs, n_parts, n_pb, x_split):
    i = pl.program_id(0)
    is_p = i < n_pb
    pos = 0
    acc = None
    w_ref = refs[2 * n_parts]
    k0 = 0
    for p in range(n_parts):
        a_p, a_s = refs[2 * p], refs[2 * p + 1]
        a = jnp.where(is_p, a_p[...], a_s[...])
        kk = a.shape[1]
        part = jnp.dot(a, w_ref[k0:k0 + kk, :], preferred_element_type=F32)
        acc = part if acc is None else acc + part
        k0 += kk
    pos = 2 * n_parts + 1
    if x_split:
        x = jnp.where(is_p, refs[pos][...], refs[pos + 1][...])
        pos += 2
    else:
        x = refs[pos][...]
        pos += 1
    mod_ref, o_ref = refs[pos], refs[pos + 1]
    o_ref[...] = x + mod_ref[0, 2:3, :] * acc


def _linear_residual(cfg, parts, w, x, mod_l):
    d = cfg.d_model
    tm = min(256, cfg.tm)
    n_pb = cfg.n_p // tm
    n_sb = cfg.n_s // tm
    p_idx = lambda i: (jnp.minimum(i, n_pb - 1), 0)
    s_idx = lambda i: (jnp.maximum(i - n_pb, 0), 0)
    in_specs, args = [], []
    for a_p, a_s in parts:
        kk = a_p.shape[1]
        in_specs += [pl.BlockSpec((tm, kk), p_idx), pl.BlockSpec((tm, kk), s_idx)]
        args += [a_p, a_s]
    in_specs.append(pl.BlockSpec(w.shape, lambda i: (0, 0)))
    args.append(w)
    x_split = isinstance(x, tuple)
    if x_split:
        in_specs += [pl.BlockSpec((tm, d), p_idx), pl.BlockSpec((tm, d), s_idx)]
        args += list(x)
    else:
        in_specs.append(pl.BlockSpec((tm, d), lambda i: (i, 0)))
        args.append(x)
    mi = _merged_mod_index(cfg, tm)
    in_specs.append(pl.BlockSpec((1, 8, d), lambda i: (mi(i), 0, 0)))
    args.append(mod_l)
    kern = functools.partial(_linres_kernel, n_parts=len(parts), n_pb=n_pb, x_split=x_split)
    return pl.pallas_call(
        kern,
        grid=(n_pb + n_sb,),
        in_specs=in_specs,
        out_specs=pl.BlockSpec((tm, d), lambda i: (i, 0)),
        out_shape=jax.ShapeDtypeStruct((cfg.n_tok, d), F32),
        compiler_params=_params(("arbitrary",)),
        name="linear_gated_residual",
    )(*args)


def _dft_chan_kernel(x_ref, mod_ref, g_ref, cs_ref, y_ref, *, groups, eps):
    h = _norm_mod(x_ref[...], g_ref[...], mod_ref[0, 0:1, :], mod_ref[0, 1:2, :], eps).astype(BF16)
    gd = h.shape[1] // groups
    for g in range(groups):
        r = jnp.dot(h[:, g * gd:(g + 1) * gd], cs_ref[...], preferred_element_type=F32)
        y_ref[0, :, g * gd:(g + 1) * gd] = r[:, :gd].astype(BF16)
        y_ref[1, :, g * gd:(g + 1) * gd] = r[:, gd:].astype(BF16)


def _dft_mats(n):
    k = np.arange(n)
    ang = 2.0 * np.pi * ((k[:, None] * k[None, :]) % n) / n
    return np.cos(ang), np.sin(ang)


def _dft_channels(cfg, x, mod_l, gain):
    d = cfg.d_model
    gd = d // cfg.f_groups
    tm = cfg.tm
    c, s = _dft_mats(gd)
    cs = jnp.asarray(np.concatenate([c, s], axis=1), dtype=F32).astype(BF16)
    mi = _merged_mod_index(cfg, tm)
    return pl.pallas_call(
        functools.partial(_dft_chan_kernel, groups=cfg.f_groups, eps=cfg.norm_eps),
        grid=(cfg.n_tok // tm,),
        in_specs=[
            pl.BlockSpec((tm, d), lambda i: (i, 0)),
            pl.BlockSpec((1, 8, d), lambda i: (mi(i), 0, 0)),
            pl.BlockSpec((1, d), lambda i: (0, 0)),
            pl.BlockSpec((gd, 2 * gd), lambda i: (0, 0)),
        ],
        out_specs=pl.BlockSpec((2, tm, d), lambda i: (0, i, 0)),
        out_shape=jax.ShapeDtypeStruct((2, cfg.n_tok, d), BF16),
        compiler_params=_params(("arbitrary",)),
        name="dft_channels",
    )(x, mod_l, gain, cs)


def _dft_seq_kernel(w_ref, y_ref, o_ref, *, scale):
    acc = (jnp.dot(w_ref[0], y_ref[0], preferred_element_type=F32)
           + jnp.dot(w_ref[1], y_ref[1], preferred_element_type=F32))
    o_ref[...] = (acc * scale).astype(o_ref.dtype)


def _dft_sequence(cfg, y, n_batch, seq_len, first_block):
    d = cfg.d_model
    c, s = _dft_mats(seq_len)
    wm = jnp.asarray(np.stack([c, -s]), dtype=F32).astype(BF16)
    tml = min(1024, seq_len)
    tn = min(512, d)
    nm = seq_len // tml
    scale = 1.0 / math.sqrt(seq_len * (d // cfg.f_groups))
    return pl.pallas_call(
        functools.partial(_dft_seq_kernel, scale=scale),
        grid=(n_batch, nm, d // tn),
        in_specs=[
            pl.BlockSpec((2, tml, seq_len), lambda b, mi, j: (0, mi, 0)),
            pl.BlockSpec((2, seq_len, tn), lambda b, mi, j: (0, first_block + b, j)),
        ],
        out_specs=pl.BlockSpec((tml, tn), lambda b, mi, j: (b * nm + mi, j)),
        out_shape=jax.ShapeDtypeStruct((n_batch * seq_len, d), BF16),
        compiler_params=_params(("arbitrary", "arbitrary", "arbitrary")),
        name="dft_sequence_%d" % seq_len,
    )(wm, y)


def _route_kernel(x_ref, mod_ref, g_ref, wr_ref, rb_ref, hp_ref, idx_ref, wt_ref, *, eps, n_exp, per_grp):
    h = _norm_mod(x_ref[...], g_ref[...], mod_ref[0, 3:4, :], mod_ref[0, 4:5, :], eps)
    tm = h.shape[0]
    n_chunk = h.shape[1] // LANES
    for c in range(n_chunk):
        hp_ref[pl.ds(c, tm, stride=n_chunk), :] = h[:, c * LANES:(c + 1) * LANES]

    logits = lax.dot_general(wr_ref[...], h, (((1,), (1,)), ((), ())),
                             precision=lax.Precision.HIGHEST, preferred_element_type=F32)
    scores = jax.nn.sigmoid(logits)
    sel = scores + rb_ref[...]
    n_grp = n_exp // per_grp
    best = None
    gi = None
    for g in range(n_grp):
        v = [sel[g * per_grp + k:g * per_grp + k + 1, :] for k in range(per_grp)]
        gs = None
        for a in range(per_grp):
            for b in range(a + 1, per_grp):
                ps = v[a] + v[b]
                gs = ps if gs is None else jnp.maximum(gs, ps)
        if best is None:
            best, gi = gs, jnp.zeros(gs.shape, jnp.int32)
        else:
            better = gs > best
            gi = jnp.where(better, g, gi)
            best = jnp.where(better, gs, best)
    row = lax.broadcasted_iota(jnp.int32, sel.shape, 0)
    masked = jnp.where(row // per_grp == gi, sel, -jnp.inf)
    m1 = jnp.max(masked, axis=0, keepdims=True)
    i1 = jnp.min(jnp.where(masked == m1, row, n_exp), axis=0, keepdims=True)
    masked2 = jnp.where(row == i1, -jnp.inf, masked)
    m2 = jnp.max(masked2, axis=0, keepdims=True)
    i2 = jnp.min(jnp.where(masked2 == m2, row, n_exp), axis=0, keepdims=True)
    w1 = jnp.sum(jnp.where(row == i1, scores, 0.0), axis=0, keepdims=True)
    w2 = jnp.sum(jnp.where(row == i2, scores, 0.0), axis=0, keepdims=True)
    inv = 1.0 / (w1 + w2)
    idx_ref[...] = jnp.concatenate([i1, i2], axis=0)
    wt_ref[...] = jnp.concatenate([w1 * inv, w2 * inv], axis=0)


def _route(cfg, x, mod_l, gain, w_router_t, router_bias):
    d = cfg.d_model
    tm = cfg.tm
    t = cfg.n_tok
    mi = _merged_mod_index(cfg, tm)
    kern = functools.partial(_route_kernel, eps=cfg.norm_eps, n_exp=cfg.n_experts,
                             per_grp=cfg.n_experts // cfg.n_groups)
    return pl.pallas_call(
        kern,
        grid=(t // tm,),
        in_specs=[
            pl.BlockSpec((tm, d), lambda i: (i, 0)),
            pl.BlockSpec((1, 8, d), lambda i: (mi(i), 0, 0)),
            pl.BlockSpec((1, d), lambda i: (0, 0)),
            pl.BlockSpec((cfg.n_experts, d), lambda i: (0, 0)),
            pl.BlockSpec((cfg.n_experts, 1), lambda i: (0, 0)),
        ],
        out_specs=[
            pl.BlockSpec((tm * (d // LANES), LANES), lambda i: (i, 0)),
            pl.BlockSpec((2, tm), lambda i: (0, i)),
            pl.BlockSpec((2, tm), lambda i: (0, i)),
        ],
        out_shape=[
            jax.ShapeDtypeStruct((t * (d // LANES), LANES), F32),
            jax.ShapeDtypeStruct((2, t), jnp.int32),
            jax.ShapeDtypeStruct((2, t), F32),
        ],
        compiler_params=_params(("arbitrary",)),
        name="moe_route",
    )(x, mod_l, gain, w_router_t, router_bias.reshape(cfg.n_experts, 1))


def _plan(cfg, idx):
    t = cfg.n_tok
    eb = cfg.e_block
    ne = cfg.n_experts
    n_blocks = -(-(2 * t + ne * (eb - 1)) // eb)
    e_flat = idx.reshape(-1)
    onehot = (e_flat[:, None] == jnp.arange(ne, dtype=jnp.int32)[None, :]).astype(jnp.int32)
    csum = jnp.cumsum(onehot, axis=0)
    rank = jnp.sum(onehot * (csum - 1), axis=1)
    counts = csum[-1]
    padded = (counts + eb - 1) // eb * eb
    pad_end = jnp.cumsum(padded)
    pad_start = pad_end - padded
    pos = (pad_start[e_flat] + rank).astype(jnp.int32).reshape(2, t)
    n_used = (pad_end[-1] // eb).astype(jnp.int32)
    blk = jnp.arange(n_blocks, dtype=jnp.int32)
    blk = jnp.minimum(blk, n_used - 1)
    first_e = jnp.sum((pad_end[None, :] <= (blk * eb)[:, None]).astype(jnp.int32), axis=1)
    block_e = jnp.minimum(first_e, ne - 1).astype(jnp.int32)
    return pos, block_e, n_used.reshape(1), pad_end.astype(jnp.int32), n_blocks


def _dispatch_kernel(pend_ref, pos_ref, hp_ref, xr_ref, zbuf, sem, zsem, *, n_exp, eb, spt):
    i = pl.program_id(0)

    @pl.when(i == 0)
    def _():
        zbuf[...] = jnp.zeros(zbuf.shape, zbuf.dtype)
        for e in range(n_exp):
            start = pl.multiple_of(jnp.maximum(pend_ref[e] - eb, 0) * spt, eb * spt)
            cp = pltpu.make_async_copy(zbuf, xr_ref.at[pl.ds(start, eb * spt)], zsem)
            cp.start()
            cp.wait()
        n_used = pend_ref[n_exp - 1] // eb
        n_blocks = xr_ref.shape[0] // (eb * spt)
        for e in range(n_exp):
            @pl.when(n_used + e < n_blocks)
            def _():
                start = pl.multiple_of((n_used + e) * (eb * spt), eb * spt)
                cp = pltpu.make_async_copy(zbuf, xr_ref.at[pl.ds(start, eb * spt)], zsem)
                cp.start()
                cp.wait()

    rows = pos_ref.shape[2]

    def start(r, c):
        src = hp_ref.at[pl.ds(pl.multiple_of(r * spt, spt), spt)]
        for k in range(2):
            dst = xr_ref.at[pl.ds(pl.multiple_of(pos_ref[0, k, r] * spt, spt), spt)]
            pltpu.make_async_copy(src, dst, sem.at[k]).start()
        return c

    lax.fori_loop(0, rows, start, 0, unroll=8)
    for k in range(2):
        pltpu.make_async_copy(hp_ref, xr_ref.at[pl.ds(0, rows * spt)], sem.at[k]).wait()


def _dispatch(cfg, hp, pos_blocks, pad_end, n_rows):
    tb = pos_blocks.shape[2]
    spt = cfg.d_model // LANES
    kern = functools.partial(_dispatch_kernel, n_exp=cfg.n_experts, eb=cfg.e_block, spt=spt)
    return pl.pallas_call(
        kern,
        grid_spec=pltpu.PrefetchScalarGridSpec(
            num_scalar_prefetch=1,
            grid=(cfg.n_tok // tb,),
            in_specs=[
                pl.BlockSpec((1, 2, tb), lambda i, pe: (i, 0, 0), memory_space=pltpu.SMEM),
                pl.BlockSpec((tb * spt, LANES), lambda i, pe: (i, 0)),
            ],
            out_specs=pl.BlockSpec(memory_space=pl.ANY),
            scratch_shapes=[
                pltpu.VMEM((cfg.e_block * spt, LANES), F32),
                pltpu.SemaphoreType.DMA((2,)),
                pltpu.SemaphoreType.DMA(()),
            ],
        ),
        out_shape=jax.ShapeDtypeStruct((n_rows * spt, LANES), F32),
        compiler_params=_params(("arbitrary",)),
        name="moe_dispatch",
    )(pad_end, pos_blocks, hp)


def _expert_kernel(be_ref, nu_ref, xp_ref, wg_ref, wu_ref, wd_ref, y_ref):
    b = pl.program_id(0)

    @pl.when(b < nu_ref[0])
    def _():
        d = wg_ref.shape[1]
        n_chunk = d // LANES
        eb = xp_ref.shape[0] // n_chunk
        x = jnp.concatenate([xp_ref[pl.ds(c, eb, stride=n_chunk), :].astype(BF16) for c in range(n_chunk)],
                            axis=1)
        g = jnp.dot(x, wg_ref[0], preferred_element_type=F32)
        u = jnp.dot(x, wu_ref[0], preferred_element_type=F32)
        a = (_silu(g) * u).astype(BF16)
        y = jnp.dot(a, wd_ref[0], preferred_element_type=F32)
        n_chunk = d // LANES
        for c in range(n_chunk):
            y_ref[pl.ds(c, eb, stride=n_chunk), :] = y[:, c * LANES:(c + 1) * LANES]

    @pl.when(b >= nu_ref[0])
    def _():
        y_ref[...] = jnp.zeros(y_ref.shape, y_ref.dtype)


def _experts(cfg, x_rows, block_e, n_used, wg, wu, wd, n_blocks):
    d = cfg.d_model
    f = cfg.d_expert
    eb = cfg.e_block
    n_chunk = d // LANES
    spt = n_chunk
    return pl.pallas_call(
        _expert_kernel,
        grid_spec=pltpu.PrefetchScalarGridSpec(
            num_scalar_prefetch=2,
            grid=(n_blocks,),
            in_specs=[
                pl.BlockSpec((eb * spt, LANES), lambda b, be, nu: (jnp.minimum(b, nu[0] - 1), 0)),
                pl.BlockSpec((1, d, f), lambda b, be, nu: (be[b], 0, 0)),
                pl.BlockSpec((1, d, f), lambda b, be, nu: (be[b], 0, 0)),
                pl.BlockSpec((1, f, d), lambda b, be, nu: (be[b], 0, 0)),
            ],
            out_specs=pl.BlockSpec((eb * n_chunk, LANES), lambda b, be, nu: (b, 0)),
        ),
        out_shape=jax.ShapeDtypeStruct((n_blocks * eb * n_chunk, LANES), F32),
        compiler_params=_params(("arbitrary",)),
        name="moe_experts",
    )(block_e, n_used, x_rows, wg, wu, wd)


def _combine_kernel(pos_ref, nxt_ref, y_ref, x_ref, mod_ref, wt_ref, fg_ref, o_ref, ybuf, sem, *, final, eps):
    i = pl.program_id(0)
    n_steps = pl.num_programs(0)
    rows, d = x_ref.shape
    n_chunk = d // LANES
    slot = i % 2
    per_choice = rows * n_chunk

    def gather(p_ref, s):
        def start(r, c):
            for k in range(2):
                src = y_ref.at[pl.ds(pl.multiple_of(p_ref[0, k, r] * n_chunk, n_chunk), n_chunk)]
                dst = ybuf.at[s, pl.ds(pl.multiple_of(k * per_choice + r * n_chunk, n_chunk), n_chunk)]
                pltpu.make_async_copy(src, dst, sem.at[s]).start()
            return c
        lax.fori_loop(0, rows, start, 0, unroll=8)

    @pl.when(i == 0)
    def _():
        gather(pos_ref, 0)

    @pl.when(i + 1 < n_steps)
    def _():
        gather(nxt_ref, 1 - slot)

    pltpu.make_async_copy(y_ref.at[pl.ds(0, 2 * per_choice)], ybuf.at[slot], sem.at[slot]).wait()

    w = wt_ref[...]
    w0, w1 = w[:, 0:1], w[:, 1:2]
    sumsq = jnp.zeros((rows, 1), F32)
    for c in range(n_chunk):
        cols = slice(c * LANES, (c + 1) * LANES)
        moe = (w0 * ybuf[slot, pl.ds(c, rows, stride=n_chunk), :]
               + w1 * ybuf[slot, pl.ds(per_choice + c, rows, stride=n_chunk), :])
        xc = x_ref[:, cols] + mod_ref[0, 5:6, cols] * moe
        o_ref[:, cols] = xc
        sumsq = sumsq + jnp.sum(xc * xc, axis=-1, keepdims=True)
    if final:
        o_ref[...] = (o_ref[...] * lax.rsqrt(sumsq * (1.0 / d) + eps)) * fg_ref[...]


def _combine(cfg, y, pos_blocks, wts_t, x, mod_l, final_gain, *, row0, n_rows, final):
    d = cfg.d_model
    tb = pos_blocks.shape[2]
    b0 = row0 // tb
    mi = _merged_mod_index(cfg, tb)
    kern = functools.partial(_combine_kernel, final=final, eps=cfg.norm_eps)
    n_steps = n_rows // tb
    return pl.pallas_call(
        kern,
        grid=(n_steps,),
        in_specs=[
            pl.BlockSpec((1, 2, tb), lambda i: (b0 + i, 0, 0), memory_space=pltpu.SMEM),
            pl.BlockSpec((1, 2, tb), lambda i: (b0 + jnp.minimum(i + 1, n_steps - 1), 0, 0),
                         memory_space=pltpu.SMEM),
            pl.BlockSpec(memory_space=pl.ANY),
            pl.BlockSpec((tb, d), lambda i: (b0 + i, 0)),
            pl.BlockSpec((1, 8, d), lambda i: (mi(b0 + i), 0, 0)),
            pl.BlockSpec((tb, 2), lambda i: (b0 + i, 0)),
            pl.BlockSpec((1, d), lambda i: (0, 0)),
        ],
        out_specs=pl.BlockSpec((tb, d), lambda i: (i, 0)),
        out_shape=jax.ShapeDtypeStruct((n_rows, d), F32),
        scratch_shapes=[pltpu.VMEM((2, 2 * tb * (d // LANES), LANES), F32), pltpu.SemaphoreType.DMA((2,))],
        compiler_params=_params(("arbitrary",)),
        name="moe_combine_final" if final else "moe_combine",
    )(pos_blocks, pos_blocks, y, x, mod_l, wts_t, final_gain)


def _moe(cfg, x, mod_l, gain, w_router_t, router_bias, wg, wu, wd, final_gain, final):
    tb = min(256, cfg.tm)
    hp, idx, wts = _route(cfg, x, mod_l, gain, w_router_t, router_bias)
    pos, block_e, n_used, pad_end, n_blocks = _plan(cfg, idx)
    pos_blocks = pos.reshape(2, cfg.n_tok // tb, tb).transpose(1, 0, 2)
    td = cfg.tm
    x_rows = _dispatch(cfg, hp, pos.reshape(2, cfg.n_tok // td, td).transpose(1, 0, 2), pad_end,
                       n_blocks * cfg.e_block)
    y = _experts(cfg, x_rows, block_e, n_used, wg, wu, wd, n_blocks)
    wts_t = wts.T
    comb = functools.partial(_combine, cfg, y, pos_blocks, wts_t, x, mod_l, final_gain, final=final)
    if final:
        return comb(row0=0, n_rows=cfg.n_p), comb(row0=cfg.n_p, n_rows=cfg.n_s)
    return comb(row0=0, n_rows=cfg.n_tok)


def _forward(cfg, x_prompt, x_sample, cache_a_k, cache_a_v, cache_b_k, cache_b_v, c, c_ctx, w_ada, b_ada,
             norm1, norm2, final_norm, w_in, w_mix_out, lambda_q1, lambda_k1, lambda_q2, lambda_k2,
             subln_gain, na_rel_bias, w_fourier_out, w_router, router_bias, w_exp_gate, w_exp_up,
             w_exp_down):
    d = cfg.d_model
    xp = x_prompt.reshape(cfg.n_p, d)
    xs = x_sample.reshape(cfg.n_s, d)

    n_cond = -(-cfg.n_mod // 8) * 8
    cond = jnp.concatenate([c_ctx[None, :], c, jnp.zeros((n_cond - cfg.n_mod, d), F32)], axis=0)
    mod = _modulation(cfg, cond, w_ada, b_ada)
    mod = mod.reshape(cfg.depth, n_cond, 6, d)[:, :cfg.n_mod]
    mod = jnp.pad(mod, ((0, 0), (0, 0), (0, 2), (0, 0)))

    w_router_t = w_router.T
    fgain = final_norm.reshape(1, d)
    x = (xp, xs)
    kv_cache = None
    for l in range(cfg.depth):
        j = l // 2
        g1 = norm1[l].reshape(1, d)
        if l % 2 == 0:
            lam_init = 0.8 - 0.6 * math.exp(-0.3 * l)
            lam_pack = jnp.zeros((8, LANES), F32).at[:4, :cfg.a_qk].set(
                jnp.stack([lambda_q1[j], lambda_k1[j], lambda_q2[j], lambda_k2[j]]))
            sgain = subln_gain[j].reshape(1, cfg.a_dim)
            w_in_b = w_in[j].astype(BF16)
            if isinstance(x, tuple):
                x_p, x_s = x
            else:
                x_p, x_s = x[:cfg.n_p], x[cfg.n_p:]
            proj_p = _projection(cfg, x_p, mod[l], g1, w_in_b, latent=False, out_dtype=F32)
            proj_s = _projection(cfg, x_s, mod[l], g1, w_in_b, latent=True, out_dtype=BF16)
            if kv_cache is None:
                kv_cache = []
            aw, bw = cfg.a_width, cfg.b_width
            kv_cache.append((proj_p[:, aw:2 * aw], proj_p[:, 2 * aw:3 * aw],
                             proj_p[:, 3 * aw + bw:3 * aw + 2 * bw], proj_p[:, 3 * aw + 2 * bw:]))
            cak = cache_a_k[:, j].reshape(cfg.dec_batch, cfg.past_len, aw)
            cav = cache_a_v[:, j].reshape(cfg.dec_batch, cfg.past_len, aw)
            cbk = cache_b_k[:, j].reshape(cfg.dec_batch, cfg.past_len, bw)
            cbv = cache_b_v[:, j].reshape(cfg.dec_batch, cfg.past_len, bw)
            a_p = _diff_attention(cfg, proj_p, lam_pack, sgain, cfg.batch, cfg.seq, lam_init)
            b_p = _soft_attention(cfg, proj_p, cfg.batch, cfg.seq)
            a_s = _diff_attention(cfg, proj_s, lam_pack, sgain, cfg.dec_batch, cfg.dec_seq, lam_init,
                                  ctx=(cak, cav))
            b_s = _na_attention(cfg, proj_s, cbk, cbv, na_rel_bias[j])
            x = _linear_residual(cfg, [(a_p, a_s), (b_p, b_s)], w_mix_out[j].astype(BF16), x, mod[l])
        else:
            if isinstance(x, tuple):
                x = jnp.concatenate(x, axis=0)
            y = _dft_channels(cfg, x, mod[l], g1)
            f_p = _dft_sequence(cfg, y, cfg.batch, cfg.seq, 0)
            f_s = _dft_sequence(cfg, y, cfg.dec_batch, cfg.dec_seq, cfg.n_p // cfg.dec_seq)
            x = _linear_residual(cfg, [(f_p, f_s)], w_fourier_out[j].astype(BF16), x, mod[l])
        x = _moe(cfg, x, mod[l], norm2[l].reshape(1, d), w_router_t, router_bias,
                 w_exp_gate[l].astype(BF16), w_exp_up[l].astype(BF16), w_exp_down[l].astype(BF16),
                 fgain, final=(l == cfg.depth - 1))
    y_p, y_s = x
    n_even = (cfg.depth + 1) // 2
    outs = [y_p.reshape(cfg.batch, cfg.seq, d), y_s.reshape(cfg.dec_batch, cfg.dec_seq, d)]
    for t in range(4):
        heads, hd = (cfg.a_heads, cfg.a_dim) if t < 2 else (cfg.b_heads, cfg.b_dim)
        stacked = jnp.stack([kv_cache[jj][t].reshape(cfg.batch, cfg.seq, heads, hd)
                             for jj in range(n_even)], axis=1)
        outs.append(stacked)
    return tuple(outs)


def kernel(x_prompt, x_sample, cache_a_k, cache_a_v, cache_b_k, cache_b_v, c, c_ctx, w_ada, b_ada, norm1, norm2, final_norm, w_in, w_mix_out, lambda_q1, lambda_k1, lambda_q2, lambda_k2, subln_gain, na_rel_bias, w_fourier_out, w_router, router_bias, w_exp_gate, w_exp_up, w_exp_down):
    return _forward(Cfg(), x_prompt, x_sample, cache_a_k, cache_a_v, cache_b_k, cache_b_v, c, c_ctx, w_ada,
                    b_ada, norm1, norm2, final_norm, w_in, w_mix_out, lambda_q1, lambda_k1, lambda_q2,
                    lambda_k2, subln_gain, na_rel_bias, w_fourier_out, w_router, router_bias, w_exp_gate,
                    w_exp_up, w_exp_down)
```

```python
import functools
import math
from typing import NamedTuple

import numpy as np
import jax
import jax.numpy as jnp
from jax import lax
from jax.experimental import pallas as pl
from jax.experimental.pallas import tpu as pltpu

F32 = jnp.float32
BF16 = jnp.bfloat16

LANES = 128
NEG_BIG = -1e30
VMEM_LIMIT = 56 * 1024 * 1024


class Cfg(NamedTuple):
    d_model: int = 2048
    batch: int = 16
    seq: int = 256
    depth: int = 2
    dec_batch: int = 8
    dec_seq: int = 2048
    past_len: int = 256
    grid_w: int = 64
    a_heads: int = 8
    a_qk: int = 64
    b_heads: int = 8
    b_dim: int = 128
    na_win_r: int = 8
    na_win_c: int = 16
    f_groups: int = 4
    n_experts: int = 16
    n_groups: int = 4
    d_expert: int = 1408
    rope_theta: float = 10000.0
    norm_eps: float = 1e-6
    subln_eps: float = 1e-5
    row_tile: int = 512
    q_tile: int = 512
    e_block: int = 256

    @property
    def a_dim(self):
        return 2 * self.a_qk

    @property
    def a_width(self):
        return self.a_heads * self.a_dim

    @property
    def b_width(self):
        return self.b_heads * self.b_dim

    @property
    def n_p(self):
        return self.batch * self.seq

    @property
    def n_s(self):
        return self.dec_batch * self.dec_seq

    @property
    def n_tok(self):
        return self.n_p + self.n_s

    @property
    def tm(self):
        return min(self.row_tile, self.n_p, self.dec_seq)

    @property
    def n_mod(self):
        return 1 + self.dec_batch


def _params(sem):
    return pltpu.CompilerParams(dimension_semantics=sem, vmem_limit_bytes=VMEM_LIMIT)


def _norm_mod(x, g, sh, sc, eps):
    ms = jnp.mean(x * x, axis=-1, keepdims=True)
    y = x * lax.rsqrt(ms + eps)
    return (y * g) * (1.0 + sc) + sh


def _silu(x):
    return x * jax.nn.sigmoid(x)


def _merged_mod_index(cfg, tm):
    def f(i):
        r = i * tm
        return jnp.where(r < cfg.n_p, 0, 1 + (r - cfg.n_p) // cfg.dec_seq)
    return f


def _ada_kernel(cond_ref, w_ref, b_ref, o_ref):
    s = _silu(cond_ref[...])
    s_hi = s.astype(BF16)
    s_lo = (s - s_hi.astype(F32)).astype(BF16)
    lhs = jnp.concatenate([s_hi, s_lo], axis=0)
    r = jnp.dot(lhs, w_ref[0].astype(BF16), preferred_element_type=F32)
    n = s.shape[0]
    o_ref[0] = r[:n] + r[n:] + b_ref[0]


def _modulation(cfg, cond, w_ada, b_ada):
    d = cfg.d_model
    r = cond.shape[0]
    tn = math.gcd(1024, 6 * d)
    return pl.pallas_call(
        _ada_kernel,
        grid=(cfg.depth, 6 * d // tn),
        in_specs=[
            pl.BlockSpec((r, d), lambda l, j: (0, 0)),
            pl.BlockSpec((1, d, tn), lambda l, j: (l, 0, j)),
            pl.BlockSpec((1, 1, tn), lambda l, j: (l, 0, j)),
        ],
        out_specs=pl.BlockSpec((1, r, tn), lambda l, j: (l, 0, j)),
        out_shape=jax.ShapeDtypeStruct((cfg.depth, r, 6 * d), F32),
        compiler_params=_params(("arbitrary", "arbitrary")),
        name="ada_modulation",
    )(cond, w_ada, b_ada.reshape(cfg.depth, 1, 6 * d))


PROJ_CHUNK = 256


def _proj_kernel(*refs, rope, n_rope_blocks, eps):
    if rope:
        x_ref, mod_ref, g_ref, w_ref, cos_ref, sa_ref, sb_ref, o_ref, h_ref = refs
    else:
        x_ref, mod_ref, g_ref, w_ref, o_ref, h_ref = refs
    j = pl.program_id(1)

    @pl.when(j == 0)
    def _():
        h = _norm_mod(x_ref[...], g_ref[...], mod_ref[0, 0:1, :], mod_ref[0, 1:2, :], eps)
        h_ref[...] = h.astype(BF16)

    if not rope:
        o_ref[...] = jnp.dot(h_ref[...], w_ref[...], preferred_element_type=F32).astype(o_ref.dtype)
        return

    @pl.when(j < n_rope_blocks)
    def _():
        cos, sa, sb = cos_ref[...], sa_ref[...], sb_ref[...]
        h = h_ref[...]
        tn = w_ref.shape[1]
        cw = min(PROJ_CHUNK, tn)
        for c0 in range(0, tn, cw):
            acc = jnp.dot(h, w_ref[:, c0:c0 + cw], preferred_element_type=F32)
            for c in range(cw // LANES):
                xa = acc[:, c * LANES:(c + 1) * LANES]
                up = pltpu.roll(xa, LANES - 16, 1)
                dn = pltpu.roll(xa, 16, 1)
                lo = c0 + c * LANES
                o_ref[:, lo:lo + LANES] = (xa * cos + up * sa + dn * sb).astype(o_ref.dtype)

    @pl.when(j >= n_rope_blocks)
    def _():
        o_ref[...] = jnp.dot(h_ref[...], w_ref[...], preferred_element_type=F32).astype(o_ref.dtype)


def _rope_tables(cfg):
    t = np.arange(cfg.dec_seq)
    row = (t // cfg.grid_w).astype(np.float64)
    col = (t % cfg.grid_w).astype(np.float64)
    lane = np.arange(LANES)
    l64 = lane % cfg.a_qk
    half = cfg.a_qk // 4
    freq = cfg.rope_theta ** (-(lane % half).astype(np.float64) / half)
    pos = np.where((l64 < cfg.a_qk // 2)[None, :], row[:, None], col[:, None])
    ang = pos * freq[None, :]
    first = (lane % (2 * half)) < half
    cos = np.cos(ang)
    sin = np.sin(ang)
    sa = np.where(first[None, :], -sin, 0.0)
    sb = np.where(first[None, :], 0.0, sin)
    return tuple(jnp.asarray(a, dtype=F32) for a in (cos, sa, sb))


def _projection(cfg, x, mod_l, gain, w, *, latent, out_dtype):
    m, d = x.shape
    n = w.shape[1]
    tm = min(cfg.row_tile, m)
    tn = min(1024, cfg.a_width)
    per_seq = cfg.dec_seq // tm if latent else 1
    mod_idx = (lambda i, j: (1 + i // per_seq, 0, 0)) if latent else (lambda i, j: (0, 0, 0))
    in_specs = [
        pl.BlockSpec((tm, d), lambda i, j: (i, 0)),
        pl.BlockSpec((1, 8, d), mod_idx),
        pl.BlockSpec((1, d), lambda i, j: (0, 0)),
        pl.BlockSpec((d, tn), lambda i, j: (0, j)),
    ]
    args = [x, mod_l, gain, w]
    if latent:
        tab_spec = pl.BlockSpec((tm, LANES), lambda i, j: (i % per_seq, 0))
        in_specs += [tab_spec, tab_spec, tab_spec]
        args += list(_rope_tables(cfg))
    kern = functools.partial(_proj_kernel, rope=latent, n_rope_blocks=2 * cfg.a_width // tn,
                             eps=cfg.norm_eps)
    return pl.pallas_call(
        kern,
        grid=(m // tm, n // tn),
        in_specs=in_specs,
        out_specs=pl.BlockSpec((tm, tn), lambda i, j: (i, j)),
        out_shape=jax.ShapeDtypeStruct((m, n), out_dtype),
        scratch_shapes=[pltpu.VMEM((tm, d), BF16)],
        compiler_params=_params(("arbitrary", "arbitrary")),
        name="qkv_projection_latent" if latent else "qkv_projection_context",
    )(*args)


def _nt_dot(a, b):
    return lax.dot_general(a, b, (((1,), (1,)), ((), ())), preferred_element_type=F32)


DIFF_SUB_TILE = 128


def _diff_attn_kernel(*refs, has_ctx, lam_init, eps, qk):
    if has_ctx:
        lam_ref, gain_ref, q_ref, k_ref, v_ref, ck_ref, cv_ref, o_ref = refs
    else:
        lam_ref, gain_ref, q_ref, k_ref, v_ref, o_ref = refs
    lv = lam_ref[...]
    s1 = jnp.sum(lv[0:1] * lv[1:2], axis=-1, keepdims=True)
    s2 = jnp.sum(lv[2:3] * lv[3:4], axis=-1, keepdims=True)
    lam = jnp.exp(s1) - jnp.exp(s2) + lam_init

    k = k_ref[...].astype(BF16)
    v = v_ref[...].astype(BF16)
    if has_ctx:
        ck = ck_ref[0].astype(BF16)
        cv = cv_ref[0].astype(BF16)
    tq = q_ref.shape[0]
    ts = min(tq, DIFF_SUB_TILE)
    for t in range(tq // ts):
        q = q_ref[t * ts:(t + 1) * ts, :].astype(F32) * (qk ** -0.5 * math.log2(math.e))
        lane = lax.broadcasted_iota(jnp.int32, q.shape, 1)
        qs = jnp.concatenate([jnp.where(lane < qk, q, 0.0), jnp.where(lane >= qk, q, 0.0)],
                             axis=0).astype(BF16)
        s_new = _nt_dot(qs, k)
        m = jnp.max(s_new, axis=-1, keepdims=True)
        if has_ctx:
            s_ctx = _nt_dot(qs, ck)
            m = jnp.maximum(m, jnp.max(s_ctx, axis=-1, keepdims=True))
        p_new = jnp.exp2(s_new - m)
        den = jnp.sum(p_new, axis=-1, keepdims=True)
        o2 = jnp.dot(p_new.astype(BF16), v, preferred_element_type=F32)
        if has_ctx:
            p_ctx = jnp.exp2(s_ctx - m)
            den = den + jnp.sum(p_ctx, axis=-1, keepdims=True)
            o2 = o2 + jnp.dot(p_ctx.astype(BF16), cv, preferred_element_type=F32)
        inv = 1.0 / den
        o = o2[:ts] * inv[:ts] - o2[ts:] * (lam * inv[ts:])
        ms = jnp.mean(o * o, axis=-1, keepdims=True)
        o = (o * lax.rsqrt(ms + eps)) * gain_ref[...]
        o_ref[t * ts:(t + 1) * ts, :] = (o * (1.0 - lam_init)).astype(o_ref.dtype)


def _diff_attention(cfg, proj, lam_pack, gain, n_batch, seq_len, lam_init, ctx=None):
    hd = cfg.a_dim
    nh = cfg.a_heads
    tq = min(cfg.q_tile, seq_len)
    nq = seq_len // tq
    in_specs = [
        pl.BlockSpec((8, LANES), lambda b, h, qi: (0, 0)),
        pl.BlockSpec((1, hd), lambda b, h, qi: (0, 0)),
        pl.BlockSpec((tq, hd), lambda b, h, qi: (b * nq + qi, h)),
        pl.BlockSpec((seq_len, hd), lambda b, h, qi: (b, nh + h)),
        pl.BlockSpec((seq_len, hd), lambda b, h, qi: (b, 2 * nh + h)),
    ]
    args = [lam_pack, gain, proj, proj, proj]
    if ctx is not None:
        ck, cv = ctx
        past = ck.shape[1]
        cspec = pl.BlockSpec((1, past, hd), lambda b, h, qi: (b, 0, h))
        in_specs += [cspec, cspec]
        args += [ck, cv]
    kern = functools.partial(_diff_attn_kernel, has_ctx=ctx is not None, lam_init=lam_init,
                             eps=cfg.subln_eps, qk=cfg.a_qk)
    return pl.pallas_call(
        kern,
        grid=(n_batch, nh, nq),
        in_specs=in_specs,
        out_specs=pl.BlockSpec((tq, hd), lambda b, h, qi: (b * nq + qi, h)),
        out_shape=jax.ShapeDtypeStruct((n_batch * seq_len, cfg.a_width), BF16),
        compiler_params=_params(("arbitrary", "arbitrary", "arbitrary")),
        name="diff_attention_latent" if ctx is not None else "diff_attention_context",
    )(*args)


def _soft_attn_kernel(q_ref, k_ref, v_ref, o_ref, *, scale):
    s = _nt_dot(q_ref[...].astype(BF16), k_ref[...].astype(BF16)) * (scale * math.log2(math.e))
    m = jnp.max(s, axis=-1, keepdims=True)
    p = jnp.exp2(s - m)
    inv = 1.0 / jnp.sum(p, axis=-1, keepdims=True)
    o = jnp.dot(p.astype(BF16), v_ref[...].astype(BF16), preferred_element_type=F32) * inv
    o_ref[...] = o.astype(o_ref.dtype)


def _soft_attention(cfg, proj, n_batch, seq_len):
    hd = cfg.b_dim
    nh = cfg.b_heads
    base = 3 * cfg.a_width // hd
    return pl.pallas_call(
        functools.partial(_soft_attn_kernel, scale=hd ** -0.5),
        grid=(n_batch, nh),
        in_specs=[
            pl.BlockSpec((seq_len, hd), lambda b, h: (b, base + h)),
            pl.BlockSpec((seq_len, hd), lambda b, h: (b, base + nh + h)),
            pl.BlockSpec((seq_len, hd), lambda b, h: (b, base + 2 * nh + h)),
        ],
        out_specs=pl.BlockSpec((seq_len, hd), lambda b, h: (b, h)),
        out_shape=jax.ShapeDtypeStruct((n_batch * seq_len, cfg.b_width), BF16),
        compiler_params=_params(("arbitrary", "arbitrary")),
        name="softmax_attention_context",
    )(proj, proj, proj)


NA_GROUP_ROWS = 4


def _na_geometry(cfg):
    rows = cfg.dec_seq // cfg.grid_w
    kr = min(cfg.na_win_r, rows)
    grp = min(NA_GROUP_ROWS, rows)
    union = min(rows, kr + grp - 1 + (kr + grp - 1) % 2)
    starts = []
    for g in range(rows // grp):
        rs0 = min(max(g * grp - kr // 2, 0), rows - kr)
        starts.append(min(rs0, rows - union))
    return rows, kr, grp, union, starts


def _na_kernel(ws_ref, q_ref, k_ref, v_ref, ck_ref, cv_ref, bias_ref, o_ref, sctx, pctx, oacc, *,
               n_groups, gq, uk, gw, scale):
    ck = ck_ref[0].astype(BF16)
    cv = cv_ref[0].astype(BF16)
    sctx[...] = _nt_dot(q_ref[...], ck) * scale

    def body(g, carry):
        q0 = pl.multiple_of(g * gq, gq)
        k0 = pl.multiple_of(ws_ref[g] * gw, gw)
        q = q_ref[pl.ds(q0, gq), :]
        s = _nt_dot(q, k_ref[pl.ds(k0, uk), :]) * scale + bias_ref[g, 0]
        sc = sctx[pl.ds(q0, gq), :]
        m = jnp.maximum(jnp.max(s, axis=-1, keepdims=True), jnp.max(sc, axis=-1, keepdims=True))
        p = jnp.exp2(s - m)
        pc = jnp.exp2(sc - m)
        inv = 1.0 / (jnp.sum(p, axis=-1, keepdims=True) + jnp.sum(pc, axis=-1, keepdims=True))
        oacc[pl.ds(q0, gq), :] = jnp.dot(p.astype(BF16), v_ref[pl.ds(k0, uk), :],
                                         preferred_element_type=F32) * inv
        pctx[pl.ds(q0, gq), :] = (pc * inv).astype(BF16)
        return carry

    lax.fori_loop(0, n_groups, body, 0, unroll=2)
    o_ref[...] = (oacc[...] + jnp.dot(pctx[...], cv, preferred_element_type=F32)).astype(o_ref.dtype)


def _na_bias_table(cfg, rpb):
    w = cfg.grid_w
    rows, kr, grp, union, starts = _na_geometry(cfg)
    qc = np.arange(w)
    kc = np.arange(w)
    cs = np.clip(qc - cfg.na_win_c // 2, 0, w - cfg.na_win_c)
    col_mask = (kc[None, :] >= cs[:, None]) & (kc[None, :] < cs[:, None] + cfg.na_win_c)
    col_idx = np.clip(kc[None, :] - qc[:, None] + cfg.na_win_c - 1, 0, 2 * cfg.na_win_c - 2)
    n_c = 2 * cfg.na_win_c - 1
    onehot = (col_idx[None] == np.arange(n_c)[:, None, None]) & col_mask[None]
    t = jnp.einsum('hrc,cqk->hrqk', rpb.astype(F32), jnp.asarray(onehot, dtype=F32),
                   precision=lax.Precision.HIGHEST)
    t = jnp.where(jnp.asarray(col_mask)[None, None], t * math.log2(math.e), NEG_BIG)
    n_r = 2 * cfg.na_win_r - 1
    t = jnp.concatenate([t, jnp.full((cfg.b_heads, 1, w, w), NEG_BIG, F32)], axis=1)
    plane = np.full((len(starts), grp, union), n_r, np.int32)
    for g, ws in enumerate(starts):
        for a in range(grp):
            r = g * grp + a
            rs = min(max(r - kr // 2, 0), rows - kr)
            for j in range(union):
                if rs <= ws + j < rs + kr:
                    plane[g, a, j] = ws + j - r + cfg.na_win_r - 1
    t = jnp.take(t, jnp.asarray(plane.reshape(-1)), axis=1)
    t = t.reshape(cfg.b_heads, len(starts), grp, union, w, w)
    t = jnp.transpose(t, (1, 0, 2, 4, 3, 5))
    return t.reshape(len(starts), cfg.b_heads, grp * w, union * w)


def _na_attention(cfg, proj, ck, cv, rpb):
    hd = cfg.b_dim
    nh = cfg.b_heads
    n = cfg.dec_seq
    rows, kr, grp, union, starts = _na_geometry(cfg)
    n_groups = len(starts)
    gq, uk = grp * cfg.grid_w, union * cfg.grid_w
    base = 3 * cfg.a_width // hd
    past = ck.shape[1]
    bias = _na_bias_table(cfg, rpb)
    kern = functools.partial(_na_kernel, n_groups=n_groups, gq=gq, uk=uk, gw=cfg.grid_w,
                             scale=hd ** -0.5 * math.log2(math.e))
    cspec = pl.BlockSpec((1, past, hd), lambda h, b, ws: (b, 0, h))
    return pl.pallas_call(
        kern,
        grid_spec=pltpu.PrefetchScalarGridSpec(
            num_scalar_prefetch=1,
            grid=(nh, cfg.dec_batch),
            in_specs=[
                pl.BlockSpec((n, hd), lambda h, b, ws: (b, base + h)),
                pl.BlockSpec((n, hd), lambda h, b, ws: (b, base + nh + h)),
                pl.BlockSpec((n, hd), lambda h, b, ws: (b, base + 2 * nh + h)),
                cspec, cspec,
                pl.BlockSpec((n_groups, 1, gq, uk), lambda h, b, ws: (0, h, 0, 0)),
            ],
            out_specs=pl.BlockSpec((n, hd), lambda h, b, ws: (b, h)),
            scratch_shapes=[pltpu.VMEM((n, past), F32), pltpu.VMEM((n, past), BF16), pltpu.VMEM((n, hd), F32)],
        ),
        out_shape=jax.ShapeDtypeStruct((cfg.n_s, cfg.b_width), BF16),
        compiler_params=_params(("arbitrary", "arbitrary")),
        name="neighbourhood_attention",
    )(jnp.asarray(np.asarray(starts, np.int32)), proj, proj, proj, ck, cv, bias)


def _linres_kernel(*refs, n_parts, n_pb, x_split):
    i = pl.program_id(0)
    is_p = i < n_pb
    pos = 0
    acc = None
    w_ref = refs[2 * n_parts]
    k0 = 0
    for p in range(n_parts):
        a_p, a_s = refs[2 * p], refs[2 * p + 1]
        a = jnp.where(is_p, a_p[...], a_s[...])
        kk = a.shape[1]
        part = jnp.dot(a, w_ref[k0:k0 + kk, :], preferred_element_type=F32)
        acc = part if acc is None else acc + part
        k0 += kk
    pos = 2 * n_parts + 1
    if x_split:
        x = jnp.where(is_p, refs[pos][...], refs[pos + 1][...])
        pos += 2
    else:
        x = refs[pos][...]
        pos += 1
    mod_ref, o_ref = refs[pos], refs[pos + 1]
    o_ref[...] = x + mod_ref[0, 2:3, :] * acc


def _linear_residual(cfg, parts, w, x, mod_l):
    d = cfg.d_model
    tm = min(256, cfg.tm)
    n_pb = cfg.n_p // tm
    n_sb = cfg.n_s // tm
    p_idx = lambda i: (jnp.minimum(i, n_pb - 1), 0)
    s_idx = lambda i: (jnp.maximum(i - n_pb, 0), 0)
    in_specs, args = [], []
    for a_p, a_s in parts:
        kk = a_p.shape[1]
        in_specs += [pl.BlockSpec((tm, kk), p_idx), pl.BlockSpec((tm, kk), s_idx)]
        args += [a_p, a_s]
    in_specs.append(pl.BlockSpec(w.shape, lambda i: (0, 0)))
    args.append(w)
    x_split = isinstance(x, tuple)
    if x_split:
        in_specs += [pl.BlockSpec((tm, d), p_idx), pl.BlockSpec((tm, d), s_idx)]
        args += list(x)
    else:
        in_specs.append(pl.BlockSpec((tm, d), lambda i: (i, 0)))
        args.append(x)
    mi = _merged_mod_index(cfg, tm)
    in_specs.append(pl.BlockSpec((1, 8, d), lambda i: (mi(i), 0, 0)))
    args.append(mod_l)
    kern = functools.partial(_linres_kernel, n_parts=len(parts), n_pb=n_pb, x_split=x_split)
    return pl.pallas_call(
        kern,
        grid=(n_pb + n_sb,),
        in_specs=in_specs,
        out_specs=pl.BlockSpec((tm, d), lambda i: (i, 0)),
        out_shape=jax.ShapeDtypeStruct((cfg.n_tok, d), F32),
        compiler_params=_params(("arbitrary",)),
        name="linear_gated_residual",
    )(*args)


def _dft_chan_kernel(x_ref, mod_ref, g_ref, cs_ref, y_ref, *, groups, eps):
    h = _norm_mod(x_ref[...], g_ref[...], mod_ref[0, 0:1, :], mod_ref[0, 1:2, :], eps).astype(BF16)
    gd = h.shape[1] // groups
    for g in range(groups):
        r = jnp.dot(h[:, g * gd:(g + 1) * gd], cs_ref[...], preferred_element_type=F32)
        y_ref[0, :, g * gd:(g + 1) * gd] = r[:, :gd].astype(BF16)
        y_ref[1, :, g * gd:(g + 1) * gd] = r[:, gd:].astype(BF16)


def _dft_mats(n):
    k = np.arange(n)
    ang = 2.0 * np.pi * ((k[:, None] * k[None, :]) % n) / n
    return np.cos(ang), np.sin(ang)


def _dft_channels(cfg, x, mod_l, gain):
    d = cfg.d_model
    gd = d // cfg.f_groups
    tm = cfg.tm
    c, s = _dft_mats(gd)
    cs = jnp.asarray(np.concatenate([c, s], axis=1), dtype=F32).astype(BF16)
    mi = _merged_mod_index(cfg, tm)
    return pl.pallas_call(
        functools.partial(_dft_chan_kernel, groups=cfg.f_groups, eps=cfg.norm_eps),
        grid=(cfg.n_tok // tm,),
        in_specs=[
            pl.BlockSpec((tm, d), lambda i: (i, 0)),
            pl.BlockSpec((1, 8, d), lambda i: (mi(i), 0, 0)),
            pl.BlockSpec((1, d), lambda i: (0, 0)),
            pl.BlockSpec((gd, 2 * gd), lambda i: (0, 0)),
        ],
        out_specs=pl.BlockSpec((2, tm, d), lambda i: (0, i, 0)),
        out_shape=jax.ShapeDtypeStruct((2, cfg.n_tok, d), BF16),
        compiler_params=_params(("arbitrary",)),
        name="dft_channels",
    )(x, mod_l, gain, cs)


def _dft_seq_kernel(w_ref, y_ref, o_ref, *, scale):
    acc = (jnp.dot(w_ref[0], y_ref[0], preferred_element_type=F32)
           + jnp.dot(w_ref[1], y_ref[1], preferred_element_type=F32))
    o_ref[...] = (acc * scale).astype(o_ref.dtype)


def _dft_sequence(cfg, y, n_batch, seq_len, first_block):
    d = cfg.d_model
    c, s = _dft_mats(seq_len)
    wm = jnp.asarray(np.stack([c, -s]), dtype=F32).astype(BF16)
    tml = min(1024, seq_len)
    tn = min(512, d)
    nm = seq_len // tml
    scale = 1.0 / math.sqrt(seq_len * (d // cfg.f_groups))
    return pl.pallas_call(
        functools.partial(_dft_seq_kernel, scale=scale),
        grid=(n_batch, nm, d // tn),
        in_specs=[
            pl.BlockSpec((2, tml, seq_len), lambda b, mi, j: (0, mi, 0)),
            pl.BlockSpec((2, seq_len, tn), lambda b, mi, j: (0, first_block + b, j)),
        ],
        out_specs=pl.BlockSpec((tml, tn), lambda b, mi, j: (b * nm + mi, j)),
        out_shape=jax.ShapeDtypeStruct((n_batch * seq_len, d), BF16),
        compiler_params=_params(("arbitrary", "arbitrary", "arbitrary")),
        name="dft_sequence_%d" % seq_len,
    )(wm, y)


def _route_kernel(x_ref, mod_ref, g_ref, wr_ref, rb_ref, hp_ref, idx_ref, wt_ref, *, eps, n_exp, per_grp):
    h = _norm_mod(x_ref[...], g_ref[...], mod_ref[0, 3:4, :], mod_ref[0, 4:5, :], eps)
    tm = h.shape[0]
    n_chunk = h.shape[1] // LANES
    for c in range(n_chunk):
        hp_ref[pl.ds(c, tm, stride=n_chunk), :] = h[:, c * LANES:(c + 1) * LANES]

    logits = lax.dot_general(wr_ref[...], h, (((1,), (1,)), ((), ())),
                             precision=lax.Precision.HIGHEST, preferred_element_type=F32)
    scores = jax.nn.sigmoid(logits)
    sel = scores + rb_ref[...]
    n_grp = n_exp // per_grp
    best = None
    gi = None
    for g in range(n_grp):
        v = [sel[g * per_grp + k:g * per_grp + k + 1, :] for k in range(per_grp)]
        gs = None
        for a in range(per_grp):
            for b in range(a + 1, per_grp):
                ps = v[a] + v[b]
                gs = ps if gs is None else jnp.maximum(gs, ps)
        if best is None:
            best, gi = gs, jnp.zeros(gs.shape, jnp.int32)
        else:
            better = gs > best
            gi = jnp.where(better, g, gi)
            best = jnp.where(better, gs, best)
    row = lax.broadcasted_iota(jnp.int32, sel.shape, 0)
    masked = jnp.where(row // per_grp == gi, sel, -jnp.inf)
    m1 = jnp.max(masked, axis=0, keepdims=True)
    i1 = jnp.min(jnp.where(masked == m1, row, n_exp), axis=0, keepdims=True)
    masked2 = jnp.where(row == i1, -jnp.inf, masked)
    m2 = jnp.max(masked2, axis=0, keepdims=True)
    i2 = jnp.min(jnp.where(masked2 == m2, row, n_exp), axis=0, keepdims=True)
    w1 = jnp.sum(jnp.where(row == i1, scores, 0.0), axis=0, keepdims=True)
    w2 = jnp.sum(jnp.where(row == i2, scores, 0.0), axis=0, keepdims=True)
    inv = 1.0 / (w1 + w2)
    idx_ref[...] = jnp.concatenate([i1, i2], axis=0)
    wt_ref[...] = jnp.concatenate([w1 * inv, w2 * inv], axis=0)


def _route(cfg, x, mod_l, gain, w_router_t, router_bias):
    d = cfg.d_model
    tm = cfg.tm
    t = cfg.n_tok
    mi = _merged_mod_index(cfg, tm)
    kern = functools.partial(_route_kernel, eps=cfg.norm_eps, n_exp=cfg.n_experts,
                             per_grp=cfg.n_experts // cfg.n_groups)
    return pl.pallas_call(
        kern,
        grid=(t // tm,),
        in_specs=[
            pl.BlockSpec((tm, d), lambda i: (i, 0)),
            pl.BlockSpec((1, 8, d), lambda i: (mi(i), 0, 0)),
            pl.BlockSpec((1, d), lambda i: (0, 0)),
            pl.BlockSpec((cfg.n_experts, d), lambda i: (0, 0)),
            pl.BlockSpec((cfg.n_experts, 1), lambda i: (0, 0)),
        ],
        out_specs=[
            pl.BlockSpec((tm * (d // LANES), LANES), lambda i: (i, 0)),
            pl.BlockSpec((2, tm), lambda i: (0, i)),
            pl.BlockSpec((2, tm), lambda i: (0, i)),
        ],
        out_shape=[
            jax.ShapeDtypeStruct((t * (d // LANES), LANES), F32),
            jax.ShapeDtypeStruct((2, t), jnp.int32),
            jax.ShapeDtypeStruct((2, t), F32),
        ],
        compiler_params=_params(("arbitrary",)),
        name="moe_route",
    )(x, mod_l, gain, w_router_t, router_bias.reshape(cfg.n_experts, 1))


def _plan(cfg, idx):
    t = cfg.n_tok
    eb = cfg.e_block
    ne = cfg.n_experts
    n_blocks = -(-(2 * t + ne * (eb - 1)) // eb)
    e_flat = idx.reshape(-1)
    onehot = (e_flat[:, None] == jnp.arange(ne, dtype=jnp.int32)[None, :]).astype(jnp.int32)
    csum = jnp.cumsum(onehot, axis=0)
    rank = jnp.sum(onehot * (csum - 1), axis=1)
    counts = csum[-1]
    padded = (counts + eb - 1) // eb * eb
    pad_end = jnp.cumsum(padded)
    pad_start = pad_end - padded
    pos = (pad_start[e_flat] + rank).astype(jnp.int32).reshape(2, t)
    n_used = (pad_end[-1] // eb).astype(jnp.int32)
    blk = jnp.arange(n_blocks, dtype=jnp.int32)
    blk = jnp.minimum(blk, n_used - 1)
    first_e = jnp.sum((pad_end[None, :] <= (blk * eb)[:, None]).astype(jnp.int32), axis=1)
    block_e = jnp.minimum(first_e, ne - 1).astype(jnp.int32)
    return pos, block_e, n_used.reshape(1), pad_end.astype(jnp.int32), n_blocks


def _weight_schedule(cfg, block_e, n_used, n_blocks):
    n_chunks = _expert_chunks(cfg)[2]
    big = n_blocks + 1
    idx = jnp.arange(n_blocks, dtype=jnp.int32)
    valid = idx < n_used[0]
    prev_e = jnp.concatenate([jnp.full((1,), -1, jnp.int32), block_e[:-1]])
    change = (block_e != prev_e) & valid
    earlier = (idx[None, :] <= idx[:, None]) & change[None, :]
    later = (idx[None, :] > idx[:, None]) & change[None, :]
    run_start = jnp.max(jnp.where(earlier, idx[None, :], 0), axis=1)
    next_start = jnp.minimum(jnp.min(jnp.where(later, idx[None, :], big), axis=1), n_used[0])
    slot = (jnp.sum(earlier.astype(jnp.int32), axis=1) - 1) % 2
    has_next = valid & (next_start < n_used[0])
    next_e = jnp.where(has_next, block_e[jnp.minimum(next_start, n_blocks - 1)], 0)
    run_len = jnp.maximum(next_start - run_start, 1)
    per_block = (n_chunks + run_len - 1) // run_len
    i = idx - run_start
    c0 = jnp.where(has_next, jnp.minimum(i * per_block, n_chunks), 0)
    c1 = jnp.where(has_next, jnp.minimum((i + 1) * per_block, n_chunks), 0)
    return tuple(a.astype(jnp.int32) for a in (slot, next_e, c0, c1))


def _dispatch_kernel(pend_ref, pos_ref, hp_ref, xr_ref, zbuf, sem, zsem, *, n_exp, eb, spt):
    i = pl.program_id(0)

    @pl.when(i == 0)
    def _():
        zbuf[...] = jnp.zeros(zbuf.shape, zbuf.dtype)
        for e in range(n_exp):
            start = pl.multiple_of(jnp.maximum(pend_ref[e] - eb, 0) * spt, eb * spt)
            cp = pltpu.make_async_copy(zbuf, xr_ref.at[pl.ds(start, eb * spt)], zsem)
            cp.start()
            cp.wait()
        n_used = pend_ref[n_exp - 1] // eb
        n_blocks = xr_ref.shape[0] // (eb * spt)
        for e in range(n_exp):
            @pl.when(n_used + e < n_blocks)
            def _():
                start = pl.multiple_of((n_used + e) * (eb * spt), eb * spt)
                cp = pltpu.make_async_copy(zbuf, xr_ref.at[pl.ds(start, eb * spt)], zsem)
                cp.start()
                cp.wait()

    rows = pos_ref.shape[2]

    def start(r, c):
        src = hp_ref.at[pl.ds(pl.multiple_of(r * spt, spt), spt)]
        for k in range(2):
            dst = xr_ref.at[pl.ds(pl.multiple_of(pos_ref[0, k, r] * spt, spt), spt)]
            pltpu.make_async_copy(src, dst, sem.at[k]).start()
        return c

    lax.fori_loop(0, rows, start, 0, unroll=8)
    for k in range(2):
        pltpu.make_async_copy(hp_ref, xr_ref.at[pl.ds(0, rows * spt)], sem.at[k]).wait()


def _dispatch(cfg, hp, pos_blocks, pad_end, n_rows):
    tb = pos_blocks.shape[2]
    spt = cfg.d_model // LANES
    kern = functools.partial(_dispatch_kernel, n_exp=cfg.n_experts, eb=cfg.e_block, spt=spt)
    return pl.pallas_call(
        kern,
        grid_spec=pltpu.PrefetchScalarGridSpec(
            num_scalar_prefetch=1,
            grid=(cfg.n_tok // tb,),
            in_specs=[
                pl.BlockSpec((1, 2, tb), lambda i, pe: (i, 0, 0), memory_space=pltpu.SMEM),
                pl.BlockSpec((tb * spt, LANES), lambda i, pe: (i, 0)),
            ],
            out_specs=pl.BlockSpec(memory_space=pl.ANY),
            scratch_shapes=[
                pltpu.VMEM((cfg.e_block * spt, LANES), F32),
                pltpu.SemaphoreType.DMA((2,)),
                pltpu.SemaphoreType.DMA(()),
            ],
        ),
        out_shape=jax.ShapeDtypeStruct((n_rows * spt, LANES), F32),
        compiler_params=_params(("arbitrary",)),
        name="moe_dispatch",
    )(pad_end, pos_blocks, hp)


EXPERT_CHUNK_IN = 128
EXPERT_CHUNK_OUT = 64
EXPERT_VMEM_LIMIT = 60 * 1024 * 1024


def _expert_chunks(cfg):
    n_in = cfg.d_model // EXPERT_CHUNK_IN
    n_out = cfg.d_expert // EXPERT_CHUNK_OUT
    return n_in, n_out, 2 * n_in + n_out


def _expert_kernel(nu_ref, e0_ref, slot_ref, ne_ref, c0_ref, c1_ref, xp_ref, wg_ref, wu_ref, wd_ref, y_ref,
                   wg_b, wu_b, wd_b, stg_in, stg_out, sem, *, n_in, n_out, layer):
    b = pl.program_id(0)
    n_chunks = 2 * n_in + n_out
    ca, cb = EXPERT_CHUNK_IN, EXPERT_CHUNK_OUT

    parts = [(0, n_in, ca, wg_ref, stg_in, wg_b), (n_in, n_in, ca, wu_ref, stg_in, wu_b),
             (2 * n_in, n_out, cb, wd_ref, stg_out, wd_b)]

    def for_chunk(c, e, fn):
        s = c % 2
        for first, count, rows, src, stg, dst in parts:
            @pl.when((c >= first) & (c < first + count))
            def _():
                row = pl.multiple_of((c - first) * rows, rows)
                cp = pltpu.make_async_copy(src.at[layer, e, pl.ds(row, rows)], stg.at[s], sem.at[s])
                fn(cp, dst, row, stg)

    def start_chunk(c, e):
        for_chunk(c, e, lambda cp, dst, row, stg: cp.start())

    def finish_chunk(c, e, dst_slot):
        def cast(cp, dst, row, stg):
            cp.wait()
            dst[dst_slot, pl.ds(row, stg.shape[1]), :] = stg[c % 2].astype(BF16)
        for_chunk(c, e, cast)

    def load_chunks(e, dst_slot, lo, hi):
        @pl.when((lo == 0) & (hi > 0))
        def _():
            start_chunk(lo, e)

        def body(c, carry):
            @pl.when(c + 1 < n_chunks)
            def _():
                start_chunk(c + 1, e)
            finish_chunk(c, e, dst_slot)
            return carry

        lax.fori_loop(lo, hi, body, 0)

    @pl.when(b == 0)
    def _():
        load_chunks(e0_ref[0], 0, 0, n_chunks)

    @pl.when(b < nu_ref[0])
    def _():
        slot = slot_ref[b]
        load_chunks(ne_ref[b], 1 - slot, c0_ref[b], c1_ref[b])
        d = wg_b.shape[1]
        n_chunk = d // LANES
        eb = xp_ref.shape[0] // n_chunk
        x = jnp.concatenate([xp_ref[pl.ds(c, eb, stride=n_chunk), :].astype(BF16) for c in range(n_chunk)],
                            axis=1)
        g = jnp.dot(x, wg_b[slot], preferred_element_type=F32)
        u = jnp.dot(x, wu_b[slot], preferred_element_type=F32)
        a = (_silu(g) * u).astype(BF16)
        y = jnp.dot(a, wd_b[slot], preferred_element_type=F32)
        for c in range(n_chunk):
            y_ref[pl.ds(c, eb, stride=n_chunk), :] = y[:, c * LANES:(c + 1) * LANES]

    @pl.when(b >= nu_ref[0])
    def _():
        y_ref[...] = jnp.zeros(y_ref.shape, y_ref.dtype)


def _experts(cfg, x_rows, block_e, n_used, wg, wu, wd, layer, n_blocks):
    d = cfg.d_model
    f = cfg.d_expert
    eb = cfg.e_block
    n_chunk = d // LANES
    n_in, n_out, _ = _expert_chunks(cfg)
    slot, next_e, c0, c1 = _weight_schedule(cfg, block_e, n_used, n_blocks)
    any_spec = pl.BlockSpec(memory_space=pl.ANY)
    x_idx = lambda b, nu, *_: (jnp.minimum(b, nu[0] - 1), 0)
    return pl.pallas_call(
        functools.partial(_expert_kernel, n_in=n_in, n_out=n_out, layer=layer),
        grid_spec=pltpu.PrefetchScalarGridSpec(
            num_scalar_prefetch=6,
            grid=(n_blocks,),
            in_specs=[pl.BlockSpec((eb * n_chunk, LANES), x_idx), any_spec, any_spec, any_spec],
            out_specs=pl.BlockSpec((eb * n_chunk, LANES), lambda b, *_: (b, 0)),
            scratch_shapes=[
                pltpu.VMEM((2, d, f), BF16), pltpu.VMEM((2, d, f), BF16), pltpu.VMEM((2, f, d), BF16),
                pltpu.VMEM((2, EXPERT_CHUNK_IN, f), F32), pltpu.VMEM((2, EXPERT_CHUNK_OUT, d), F32),
                pltpu.SemaphoreType.DMA((2,)),
            ],
        ),
        out_shape=jax.ShapeDtypeStruct((n_blocks * eb * n_chunk, LANES), F32),
        compiler_params=pltpu.CompilerParams(dimension_semantics=("arbitrary",),
                                             vmem_limit_bytes=EXPERT_VMEM_LIMIT),
        name="moe_experts",
    )(n_used, block_e[:1], slot, next_e, c0, c1, x_rows, wg, wu, wd)


def _combine_kernel(pos_ref, nxt_ref, y_ref, x_ref, mod_ref, wt_ref, fg_ref, o_ref, ybuf, sem, *, final, eps):
    i = pl.program_id(0)
    n_steps = pl.num_programs(0)
    rows, d = x_ref.shape
    n_chunk = d // LANES
    slot = i % 2
    per_choice = rows * n_chunk

    def gather(p_ref, s):
        def start(r, c):
            for k in range(2):
                src = y_ref.at[pl.ds(pl.multiple_of(p_ref[0, k, r] * n_chunk, n_chunk), n_chunk)]
                dst = ybuf.at[s, pl.ds(pl.multiple_of(k * per_choice + r * n_chunk, n_chunk), n_chunk)]
                pltpu.make_async_copy(src, dst, sem.at[s]).start()
            return c
        lax.fori_loop(0, rows, start, 0, unroll=8)

    @pl.when(i == 0)
    def _():
        gather(pos_ref, 0)

    @pl.when(i + 1 < n_steps)
    def _():
        gather(nxt_ref, 1 - slot)

    pltpu.make_async_copy(y_ref.at[pl.ds(0, 2 * per_choice)], ybuf.at[slot], sem.at[slot]).wait()

    w = wt_ref[...]
    w0, w1 = w[:, 0:1], w[:, 1:2]
    sumsq = jnp.zeros((rows, 1), F32)
    for c in range(n_chunk):
        cols = slice(c * LANES, (c + 1) * LANES)
        moe = (w0 * ybuf[slot, pl.ds(c, rows, stride=n_chunk), :]
               + w1 * ybuf[slot, pl.ds(per_choice + c, rows, stride=n_chunk), :])
        xc = x_ref[:, cols] + mod_ref[0, 5:6, cols] * moe
        o_ref[:, cols] = xc
        sumsq = sumsq + jnp.sum(xc * xc, axis=-1, keepdims=True)
    if final:
        o_ref[...] = (o_ref[...] * lax.rsqrt(sumsq * (1.0 / d) + eps)) * fg_ref[...]


def _combine(cfg, y, pos_blocks, wts_t, x, mod_l, final_gain, *, row0, n_rows, final):
    d = cfg.d_model
    tb = pos_blocks.shape[2]
    b0 = row0 // tb
    mi = _merged_mod_index(cfg, tb)
    kern = functools.partial(_combine_kernel, final=final, eps=cfg.norm_eps)
    n_steps = n_rows // tb
    return pl.pallas_call(
        kern,
        grid=(n_steps,),
        in_specs=[
            pl.BlockSpec((1, 2, tb), lambda i: (b0 + i, 0, 0), memory_space=pltpu.SMEM),
            pl.BlockSpec((1, 2, tb), lambda i: (b0 + jnp.minimum(i + 1, n_steps - 1), 0, 0),
                         memory_space=pltpu.SMEM),
            pl.BlockSpec(memory_space=pl.ANY),
            pl.BlockSpec((tb, d), lambda i: (b0 + i, 0)),
            pl.BlockSpec((1, 8, d), lambda i: (mi(b0 + i), 0, 0)),
            pl.BlockSpec((tb, 2), lambda i: (b0 + i, 0)),
            pl.BlockSpec((1, d), lambda i: (0, 0)),
        ],
        out_specs=pl.BlockSpec((tb, d), lambda i: (i, 0)),
        out_shape=jax.ShapeDtypeStruct((n_rows, d), F32),
        scratch_shapes=[pltpu.VMEM((2, 2 * tb * (d // LANES), LANES), F32), pltpu.SemaphoreType.DMA((2,))],
        compiler_params=_params(("arbitrary",)),
        name="moe_combine_final" if final else "moe_combine",
    )(pos_blocks, pos_blocks, y, x, mod_l, wts_t, final_gain)


def _moe(cfg, x, mod_l, gain, w_router_t, router_bias, wg, wu, wd, layer, final_gain, final):
    tb = min(256, cfg.tm)
    hp, idx, wts = _route(cfg, x, mod_l, gain, w_router_t, router_bias)
    pos, block_e, n_used, pad_end, n_blocks = _plan(cfg, idx)
    pos_blocks = pos.reshape(2, cfg.n_tok // tb, tb).transpose(1, 0, 2)
    td = cfg.tm
    x_rows = _dispatch(cfg, hp, pos.reshape(2, cfg.n_tok // td, td).transpose(1, 0, 2), pad_end,
                       n_blocks * cfg.e_block)
    y = _experts(cfg, x_rows, block_e, n_used, wg, wu, wd, layer, n_blocks)
    wts_t = wts.T
    comb = functools.partial(_combine, cfg, y, pos_blocks, wts_t, x, mod_l, final_gain, final=final)
    if final:
        return comb(row0=0, n_rows=cfg.n_p), comb(row0=cfg.n_p, n_rows=cfg.n_s)
    return comb(row0=0, n_rows=cfg.n_tok)


def _forward(cfg, x_prompt, x_sample, cache_a_k, cache_a_v, cache_b_k, cache_b_v, c, c_ctx, w_ada, b_ada,
             norm1, norm2, final_norm, w_in, w_mix_out, lambda_q1, lambda_k1, lambda_q2, lambda_k2,
             subln_gain, na_rel_bias, w_fourier_out, w_router, router_bias, w_exp_gate, w_exp_up,
             w_exp_down):
    d = cfg.d_model
    xp = x_prompt.reshape(cfg.n_p, d)
    xs = x_sample.reshape(cfg.n_s, d)

    n_cond = -(-cfg.n_mod // 8) * 8
    cond = jnp.concatenate([c_ctx[None, :], c, jnp.zeros((n_cond - cfg.n_mod, d), F32)], axis=0)
    mod = _modulation(cfg, cond, w_ada, b_ada)
    mod = mod.reshape(cfg.depth, n_cond, 6, d)[:, :cfg.n_mod]
    mod = jnp.pad(mod, ((0, 0), (0, 0), (0, 2), (0, 0)))

    w_router_t = w_router.T
    fgain = final_norm.reshape(1, d)
    x = (xp, xs)
    kv_cache = None
    for l in range(cfg.depth):
        j = l // 2
        g1 = norm1[l].reshape(1, d)
        if l % 2 == 0:
            lam_init = 0.8 - 0.6 * math.exp(-0.3 * l)
            lam_pack = jnp.zeros((8, LANES), F32).at[:4, :cfg.a_qk].set(
                jnp.stack([lambda_q1[j], lambda_k1[j], lambda_q2[j], lambda_k2[j]]))
            sgain = subln_gain[j].reshape(1, cfg.a_dim)
            w_in_b = w_in[j].astype(BF16)
            if isinstance(x, tuple):
                x_p, x_s = x
            else:
                x_p, x_s = x[:cfg.n_p], x[cfg.n_p:]
            proj_p = _projection(cfg, x_p, mod[l], g1, w_in_b, latent=False, out_dtype=F32)
            proj_s = _projection(cfg, x_s, mod[l], g1, w_in_b, latent=True, out_dtype=BF16)
            if kv_cache is None:
                kv_cache = []
            aw, bw = cfg.a_width, cfg.b_width
            kv_cache.append((proj_p[:, aw:2 * aw], proj_p[:, 2 * aw:3 * aw],
                             proj_p[:, 3 * aw + bw:3 * aw + 2 * bw], proj_p[:, 3 * aw + 2 * bw:]))
            cak = cache_a_k[:, j].reshape(cfg.dec_batch, cfg.past_len, aw)
            cav = cache_a_v[:, j].reshape(cfg.dec_batch, cfg.past_len, aw)
            cbk = cache_b_k[:, j].reshape(cfg.dec_batch, cfg.past_len, bw)
            cbv = cache_b_v[:, j].reshape(cfg.dec_batch, cfg.past_len, bw)
            a_p = _diff_attention(cfg, proj_p, lam_pack, sgain, cfg.batch, cfg.seq, lam_init)
            b_p = _soft_attention(cfg, proj_p, cfg.batch, cfg.seq)
            a_s = _diff_attention(cfg, proj_s, lam_pack, sgain, cfg.dec_batch, cfg.dec_seq, lam_init,
                                  ctx=(cak, cav))
            b_s = _na_attention(cfg, proj_s, cbk, cbv, na_rel_bias[j])
            x = _linear_residual(cfg, [(a_p, a_s), (b_p, b_s)], w_mix_out[j].astype(BF16), x, mod[l])
        else:
            if isinstance(x, tuple):
                x = jnp.concatenate(x, axis=0)
            y = _dft_channels(cfg, x, mod[l], g1)
            f_p = _dft_sequence(cfg, y, cfg.batch, cfg.seq, 0)
            f_s = _dft_sequence(cfg, y, cfg.dec_batch, cfg.dec_seq, cfg.n_p // cfg.dec_seq)
            x = _linear_residual(cfg, [(f_p, f_s)], w_fourier_out[j].astype(BF16), x, mod[l])
        x = _moe(cfg, x, mod[l], norm2[l].reshape(1, d), w_router_t, router_bias,
                 w_exp_gate, w_exp_up, w_exp_down, l, fgain, final=(l == cfg.depth - 1))
    y_p, y_s = x
    n_even = (cfg.depth + 1) // 2
    outs = [y_p.reshape(cfg.batch, cfg.seq, d), y_s.reshape(cfg.dec_batch, cfg.dec_seq, d)]
    for t in range(4):
        heads, hd = (cfg.a_heads, cfg.a_dim) if t < 2 else (cfg.b_heads, cfg.b_dim)
        stacked = jnp.stack([kv_cache[jj][t].reshape(cfg.batch, cfg.seq, heads, hd)
                             for jj in range(n_even)], axis=1)
        outs.append(stacked)
    return tuple(outs)


def kernel(x_prompt, x_sample, cache_a_k, cache_a_v, cache_b_k, cache_b_v, c, c_ctx, w_ada, b_ada, norm1, norm2, final_norm, w_in, w_mix_out, lambda_q1, lambda_k1, lambda_q2, lambda_k2, subln_gain, na_rel_bias, w_fourier_out, w_router, router_bias, w_exp_gate, w_exp_up, w_exp_down):
    return _forward(Cfg(), x_prompt, x_sample, cache_a_k, cache_a_v, cache_b_k, cache_b_v, c, c_ctx, w_ada,
                    b_ada, norm1, norm2, final_norm, w_in, w_mix_out, lambda_q1, lambda_k1, lambda_q2,
                    lambda_k2, subln_gain, na_rel_bias, w_fourier_out, w_router, router_bias, w_exp_gate,
                    w_exp_up, w_exp_down)
```

```python
import functools
import math
from typing import NamedTuple

import numpy as np
import jax
import jax.numpy as jnp
from jax import lax
from jax.experimental import pallas as pl
from jax.experimental.pallas import tpu as pltpu

F32 = jnp.float32
BF16 = jnp.bfloat16

LANES = 128
NEG_BIG = -1e30
VMEM_LIMIT = 56 * 1024 * 1024


class Cfg(NamedTuple):
    d_model: int = 2048
    batch: int = 16
    seq: int = 256
    depth: int = 2
    dec_batch: int = 8
    dec_seq: int = 2048
    past_len: int = 256
    grid_w: int = 64
    a_heads: int = 8
    a_qk: int = 64
    b_heads: int = 8
    b_dim: int = 128
    na_win_r: int = 8
    na_win_c: int = 16
    f_groups: int = 4
    n_experts: int = 16
    n_groups: int = 4
    d_expert: int = 1408
    rope_theta: float = 10000.0
    norm_eps: float = 1e-6
    subln_eps: float = 1e-5
    row_tile: int = 512
    q_tile: int = 512
    e_block: int = 256

    @property
    def a_dim(self):
        return 2 * self.a_qk

    @property
    def a_width(self):
        return self.a_heads * self.a_dim

    @property
    def b_width(self):
        return self.b_heads * self.b_dim

    @property
    def n_p(self):
        return self.batch * self.seq

    @property
    def n_s(self):
        return self.dec_batch * self.dec_seq

    @property
    def n_tok(self):
        return self.n_p + self.n_s

    @property
    def tm(self):
        return min(self.row_tile, self.n_p, self.dec_seq)

    @property
    def n_mod(self):
        return 1 + self.dec_batch


def _params(sem):
    return pltpu.CompilerParams(dimension_semantics=sem, vmem_limit_bytes=VMEM_LIMIT)


def _norm_mod(x, g, sh, sc, eps):
    ms = jnp.mean(x * x, axis=-1, keepdims=True)
    y = x * lax.rsqrt(ms + eps)
    return (y * g) * (1.0 + sc) + sh


def _silu(x):
    return x * jax.nn.sigmoid(x)


def _merged_mod_index(cfg, tm):
    def f(i):
        r = i * tm
        return jnp.where(r < cfg.n_p, 0, 1 + (r - cfg.n_p) // cfg.dec_seq)
    return f


def _ada_kernel(cond_ref, w_ref, b_ref, o_ref):
    s = _silu(cond_ref[...])
    s_hi = s.astype(BF16)
    s_lo = (s - s_hi.astype(F32)).astype(BF16)
    lhs = jnp.concatenate([s_hi, s_lo], axis=0)
    r = jnp.dot(lhs, w_ref[0].astype(BF16), preferred_element_type=F32)
    n = s.shape[0]
    o_ref[0] = r[:n] + r[n:] + b_ref[0]


def _modulation(cfg, cond, w_ada, b_ada):
    d = cfg.d_model
    r = cond.shape[0]
    tn = math.gcd(1024, 6 * d)
    return pl.pallas_call(
        _ada_kernel,
        grid=(cfg.depth, 6 * d // tn),
        in_specs=[
            pl.BlockSpec((r, d), lambda l, j: (0, 0)),
            pl.BlockSpec((1, d, tn), lambda l, j: (l, 0, j)),
            pl.BlockSpec((1, 1, tn), lambda l, j: (l, 0, j)),
        ],
        out_specs=pl.BlockSpec((1, r, tn), lambda l, j: (l, 0, j)),
        out_shape=jax.ShapeDtypeStruct((cfg.depth, r, 6 * d), F32),
        compiler_params=_params(("arbitrary", "arbitrary")),
        name="ada_modulation",
    )(cond, w_ada, b_ada.reshape(cfg.depth, 1, 6 * d))


PROJ_CHUNK = 256


def _proj_kernel(*refs, rope, n_rope_blocks, eps):
    if rope:
        x_ref, mod_ref, g_ref, w_ref, cos_ref, sa_ref, sb_ref, o_ref, h_ref = refs
    else:
        x_ref, mod_ref, g_ref, w_ref, o_ref, h_ref = refs
    j = pl.program_id(1)

    @pl.when(j == 0)
    def _():
        h = _norm_mod(x_ref[...], g_ref[...], mod_ref[0, 0:1, :], mod_ref[0, 1:2, :], eps)
        h_ref[...] = h.astype(BF16)

    if not rope:
        o_ref[...] = jnp.dot(h_ref[...], w_ref[...], preferred_element_type=F32).astype(o_ref.dtype)
        return

    @pl.when(j < n_rope_blocks)
    def _():
        cos, sa, sb = cos_ref[...], sa_ref[...], sb_ref[...]
        h = h_ref[...]
        tn = w_ref.shape[1]
        cw = min(PROJ_CHUNK, tn)
        for c0 in range(0, tn, cw):
            acc = jnp.dot(h, w_ref[:, c0:c0 + cw], preferred_element_type=F32)
            for c in range(cw // LANES):
                xa = acc[:, c * LANES:(c + 1) * LANES]
                up = pltpu.roll(xa, LANES - 16, 1)
                dn = pltpu.roll(xa, 16, 1)
                lo = c0 + c * LANES
                o_ref[:, lo:lo + LANES] = (xa * cos + up * sa + dn * sb).astype(o_ref.dtype)

    @pl.when(j >= n_rope_blocks)
    def _():
        o_ref[...] = jnp.dot(h_ref[...], w_ref[...], preferred_element_type=F32).astype(o_ref.dtype)


def _rope_tables(cfg):
    t = np.arange(cfg.dec_seq)
    row = (t // cfg.grid_w).astype(np.float64)
    col = (t % cfg.grid_w).astype(np.float64)
    lane = np.arange(LANES)
    l64 = lane % cfg.a_qk
    half = cfg.a_qk // 4
    freq = cfg.rope_theta ** (-(lane % half).astype(np.float64) / half)
    pos = np.where((l64 < cfg.a_qk // 2)[None, :], row[:, None], col[:, None])
    ang = pos * freq[None, :]
    first = (lane % (2 * half)) < half
    cos = np.cos(ang)
    sin = np.sin(ang)
    sa = np.where(first[None, :], -sin, 0.0)
    sb = np.where(first[None, :], 0.0, sin)
    return tuple(jnp.asarray(a, dtype=F32) for a in (cos, sa, sb))


def _projection(cfg, x, mod_l, gain, w, *, latent, out_dtype):
    m, d = x.shape
    n = w.shape[1]
    tm = min(cfg.row_tile, m)
    tn = min(1024, cfg.a_width)
    per_seq = cfg.dec_seq // tm if latent else 1
    mod_idx = (lambda i, j: (1 + i // per_seq, 0, 0)) if latent else (lambda i, j: (0, 0, 0))
    in_specs = [
        pl.BlockSpec((tm, d), lambda i, j: (i, 0)),
        pl.BlockSpec((1, 8, d), mod_idx),
        pl.BlockSpec((1, d), lambda i, j: (0, 0)),
        pl.BlockSpec((d, tn), lambda i, j: (0, j)),
    ]
    args = [x, mod_l, gain, w]
    if latent:
        tab_spec = pl.BlockSpec((tm, LANES), lambda i, j: (i % per_seq, 0))
        in_specs += [tab_spec, tab_spec, tab_spec]
        args += list(_rope_tables(cfg))
    kern = functools.partial(_proj_kernel, rope=latent, n_rope_blocks=2 * cfg.a_width // tn,
                             eps=cfg.norm_eps)
    return pl.pallas_call(
        kern,
        grid=(m // tm, n // tn),
        in_specs=in_specs,
        out_specs=pl.BlockSpec((tm, tn), lambda i, j: (i, j)),
        out_shape=jax.ShapeDtypeStruct((m, n), out_dtype),
        scratch_shapes=[pltpu.VMEM((tm, d), BF16)],
        compiler_params=_params(("arbitrary", "arbitrary")),
        name="qkv_projection_latent" if latent else "qkv_projection_context",
    )(*args)


def _nt_dot(a, b):
    return lax.dot_general(a, b, (((1,), (1,)), ((), ())), preferred_element_type=F32)


DIFF_SUB_TILE = 128


def _diff_attn_kernel(*refs, has_ctx, lam_init, eps, qk):
    if has_ctx:
        lam_ref, gain_ref, q_ref, k_ref, v_ref, ck_ref, cv_ref, o_ref = refs
    else:
        lam_ref, gain_ref, q_ref, k_ref, v_ref, o_ref = refs
    lv = lam_ref[...]
    s1 = jnp.sum(lv[0:1] * lv[1:2], axis=-1, keepdims=True)
    s2 = jnp.sum(lv[2:3] * lv[3:4], axis=-1, keepdims=True)
    lam = jnp.exp(s1) - jnp.exp(s2) + lam_init

    k = k_ref[...].astype(BF16)
    v = v_ref[...].astype(BF16)
    if has_ctx:
        ck = ck_ref[0].astype(BF16)
        cv = cv_ref[0].astype(BF16)
    tq = q_ref.shape[0]
    ts = min(tq, DIFF_SUB_TILE)
    for t in range(tq // ts):
        q = q_ref[t * ts:(t + 1) * ts, :].astype(F32) * (qk ** -0.5 * math.log2(math.e))
        lane = lax.broadcasted_iota(jnp.int32, q.shape, 1)
        qs = jnp.concatenate([jnp.where(lane < qk, q, 0.0), jnp.where(lane >= qk, q, 0.0)],
                             axis=0).astype(BF16)
        s_new = _nt_dot(qs, k)
        m = jnp.max(s_new, axis=-1, keepdims=True)
        if has_ctx:
            s_ctx = _nt_dot(qs, ck)
            m = jnp.maximum(m, jnp.max(s_ctx, axis=-1, keepdims=True))
        p_new = jnp.exp2(s_new - m)
        den = jnp.sum(p_new, axis=-1, keepdims=True)
        o2 = jnp.dot(p_new.astype(BF16), v, preferred_element_type=F32)
        if has_ctx:
            p_ctx = jnp.exp2(s_ctx - m)
            den = den + jnp.sum(p_ctx, axis=-1, keepdims=True)
            o2 = o2 + jnp.dot(p_ctx.astype(BF16), cv, preferred_element_type=F32)
        inv = 1.0 / den
        o = o2[:ts] * inv[:ts] - o2[ts:] * (lam * inv[ts:])
        ms = jnp.mean(o * o, axis=-1, keepdims=True)
        o = (o * lax.rsqrt(ms + eps)) * gain_ref[...]
        o_ref[t * ts:(t + 1) * ts, :] = (o * (1.0 - lam_init)).astype(o_ref.dtype)


def _diff_attention(cfg, proj, lam_pack, gain, n_batch, seq_len, lam_init, ctx=None):
    hd = cfg.a_dim
    nh = cfg.a_heads
    tq = min(cfg.q_tile, seq_len)
    nq = seq_len // tq
    in_specs = [
        pl.BlockSpec((8, LANES), lambda b, h, qi: (0, 0)),
        pl.BlockSpec((1, hd), lambda b, h, qi: (0, 0)),
        pl.BlockSpec((tq, hd), lambda b, h, qi: (b * nq + qi, h)),
        pl.BlockSpec((seq_len, hd), lambda b, h, qi: (b, nh + h)),
        pl.BlockSpec((seq_len, hd), lambda b, h, qi: (b, 2 * nh + h)),
    ]
    args = [lam_pack, gain, proj, proj, proj]
    if ctx is not None:
        ck, cv = ctx
        past = ck.shape[1]
        cspec = pl.BlockSpec((1, past, hd), lambda b, h, qi: (b, 0, h))
        in_specs += [cspec, cspec]
        args += [ck, cv]
    kern = functools.partial(_diff_attn_kernel, has_ctx=ctx is not None, lam_init=lam_init,
                             eps=cfg.subln_eps, qk=cfg.a_qk)
    return pl.pallas_call(
        kern,
        grid=(n_batch, nh, nq),
        in_specs=in_specs,
        out_specs=pl.BlockSpec((tq, hd), lambda b, h, qi: (b * nq + qi, h)),
        out_shape=jax.ShapeDtypeStruct((n_batch * seq_len, cfg.a_width), BF16),
        compiler_params=_params(("arbitrary", "arbitrary", "arbitrary")),
        name="diff_attention_latent" if ctx is not None else "diff_attention_context",
    )(*args)


def _soft_attn_kernel(q_ref, k_ref, v_ref, o_ref, *, scale):
    s = _nt_dot(q_ref[...].astype(BF16), k_ref[...].astype(BF16)) * (scale * math.log2(math.e))
    m = jnp.max(s, axis=-1, keepdims=True)
    p = jnp.exp2(s - m)
    inv = 1.0 / jnp.sum(p, axis=-1, keepdims=True)
    o = jnp.dot(p.astype(BF16), v_ref[...].astype(BF16), preferred_element_type=F32) * inv
    o_ref[...] = o.astype(o_ref.dtype)


def _soft_attention(cfg, proj, n_batch, seq_len):
    hd = cfg.b_dim
    nh = cfg.b_heads
    base = 3 * cfg.a_width // hd
    return pl.pallas_call(
        functools.partial(_soft_attn_kernel, scale=hd ** -0.5),
        grid=(n_batch, nh),
        in_specs=[
            pl.BlockSpec((seq_len, hd), lambda b, h: (b, base + h)),
            pl.BlockSpec((seq_len, hd), lambda b, h: (b, base + nh + h)),
            pl.BlockSpec((seq_len, hd), lambda b, h: (b, base + 2 * nh + h)),
        ],
        out_specs=pl.BlockSpec((seq_len, hd), lambda b, h: (b, h)),
        out_shape=jax.ShapeDtypeStruct((n_batch * seq_len, cfg.b_width), BF16),
        compiler_params=_params(("arbitrary", "arbitrary")),
        name="softmax_attention_context",
    )(proj, proj, proj)


NA_GROUP_ROWS = 4


def _na_geometry(cfg):
    rows = cfg.dec_seq // cfg.grid_w
    kr = min(cfg.na_win_r, rows)
    grp = min(NA_GROUP_ROWS, rows)
    union = min(rows, kr + grp - 1 + (kr + grp - 1) % 2)
    starts = []
    for g in range(rows // grp):
        rs0 = min(max(g * grp - kr // 2, 0), rows - kr)
        starts.append(min(rs0, rows - union))
    return rows, kr, grp, union, starts


def _na_kernel(ws_ref, plane_ref, q_ref, k_ref, v_ref, ck_ref, cv_ref, tab_ref, o_ref, sctx, pctx, oacc, *,
               n_groups, grp, union, gw, scale):
    gq, uk = grp * gw, union * gw
    ck = ck_ref[0].astype(BF16)
    cv = cv_ref[0].astype(BF16)
    sctx[...] = _nt_dot(q_ref[...], ck) * scale
    first_half = lax.broadcasted_iota(jnp.int32, (gw, LANES), 1) < gw

    def body(g, carry):
        q0 = pl.multiple_of(g * gq, gq)
        k0 = pl.multiple_of(ws_ref[g] * gw, gw)
        q = q_ref[pl.ds(q0, gq), :]
        bias_rows = []
        for a in range(grp):
            base = (g * grp + a) * union
            tiles = [jnp.where(first_half, tab_ref[0, plane_ref[base + 2 * jj]],
                               tab_ref[0, plane_ref[base + 2 * jj + 1]]) for jj in range(union // 2)]
            bias_rows.append(jnp.concatenate(tiles, axis=1))
        bias = jnp.concatenate(bias_rows, axis=0)
        s = _nt_dot(q, k_ref[pl.ds(k0, uk), :]) * scale + bias
        sc = sctx[pl.ds(q0, gq), :]
        m = jnp.maximum(jnp.max(s, axis=-1, keepdims=True), jnp.max(sc, axis=-1, keepdims=True))
        p = jnp.exp2(s - m)
        pc = jnp.exp2(sc - m)
        inv = 1.0 / (jnp.sum(p, axis=-1, keepdims=True) + jnp.sum(pc, axis=-1, keepdims=True))
        oacc[pl.ds(q0, gq), :] = jnp.dot(p.astype(BF16), v_ref[pl.ds(k0, uk), :],
                                         preferred_element_type=F32) * inv
        pctx[pl.ds(q0, gq), :] = (pc * inv).astype(BF16)
        return carry

    lax.fori_loop(0, n_groups, body, 0, unroll=2)
    o_ref[...] = (oacc[...] + jnp.dot(pctx[...], cv, preferred_element_type=F32)).astype(o_ref.dtype)


def _na_bias_table(cfg, rpb):
    w = cfg.grid_w
    assert LANES == 2 * w
    rows, kr, grp, union, starts = _na_geometry(cfg)
    qc = np.arange(w)
    kc = np.arange(w)
    cs = np.clip(qc - cfg.na_win_c // 2, 0, w - cfg.na_win_c)
    col_mask = (kc[None, :] >= cs[:, None]) & (kc[None, :] < cs[:, None] + cfg.na_win_c)
    col_idx = np.clip(kc[None, :] - qc[:, None] + cfg.na_win_c - 1, 0, 2 * cfg.na_win_c - 2)
    n_c = 2 * cfg.na_win_c - 1
    onehot = (col_idx[None] == np.arange(n_c)[:, None, None]) & col_mask[None]
    t = jnp.einsum('hrc,cqk->hrqk', rpb.astype(F32), jnp.asarray(onehot, dtype=F32),
                   precision=lax.Precision.HIGHEST)
    t = jnp.where(jnp.asarray(col_mask)[None, None], t * math.log2(math.e), NEG_BIG)
    n_r = 2 * cfg.na_win_r - 1
    t = jnp.concatenate([t, jnp.full((cfg.b_heads, 1, w, w), NEG_BIG, F32)], axis=1)
    plane = np.full((len(starts), grp, union), n_r, np.int32)
    for g, ws in enumerate(starts):
        for a in range(grp):
            r = g * grp + a
            rs = min(max(r - kr // 2, 0), rows - kr)
            for j in range(union):
                if rs <= ws + j < rs + kr:
                    plane[g, a, j] = ws + j - r + cfg.na_win_r - 1
    return jnp.concatenate([t, t], axis=-1), jnp.asarray(plane.reshape(-1))


def _na_attention(cfg, proj, ck, cv, rpb):
    hd = cfg.b_dim
    nh = cfg.b_heads
    n = cfg.dec_seq
    rows, kr, grp, union, starts = _na_geometry(cfg)
    n_groups = len(starts)
    base = 3 * cfg.a_width // hd
    past = ck.shape[1]
    table, plane = _na_bias_table(cfg, rpb)
    kern = functools.partial(_na_kernel, n_groups=n_groups, grp=grp, union=union, gw=cfg.grid_w,
                             scale=hd ** -0.5 * math.log2(math.e))
    cspec = pl.BlockSpec((1, past, hd), lambda h, b, *_: (b, 0, h))
    return pl.pallas_call(
        kern,
        grid_spec=pltpu.PrefetchScalarGridSpec(
            num_scalar_prefetch=2,
            grid=(nh, cfg.dec_batch),
            in_specs=[
                pl.BlockSpec((n, hd), lambda h, b, *_: (b, base + h)),
                pl.BlockSpec((n, hd), lambda h, b, *_: (b, base + nh + h)),
                pl.BlockSpec((n, hd), lambda h, b, *_: (b, base + 2 * nh + h)),
                cspec, cspec,
                pl.BlockSpec((1,) + table.shape[1:], lambda h, b, *_: (h, 0, 0, 0)),
            ],
            out_specs=pl.BlockSpec((n, hd), lambda h, b, *_: (b, h)),
            scratch_shapes=[pltpu.VMEM((n, past), F32), pltpu.VMEM((n, past), BF16), pltpu.VMEM((n, hd), F32)],
        ),
        out_shape=jax.ShapeDtypeStruct((cfg.n_s, cfg.b_width), BF16),
        compiler_params=_params(("arbitrary", "arbitrary")),
        name="neighbourhood_attention",
    )(jnp.asarray(np.asarray(starts, np.int32)), plane, proj, proj, proj, ck, cv, table)


def _linres_kernel(*refs, n_parts, n_pb, x_split):
    i = pl.program_id(0)
    is_p = i < n_pb
    pos = 0
    acc = None
    w_ref = refs[2 * n_parts]
    k0 = 0
    for p in range(n_parts):
        a_p, a_s = refs[2 * p], refs[2 * p + 1]
        a = jnp.where(is_p, a_p[...], a_s[...])
        kk = a.shape[1]
        part = jnp.dot(a, w_ref[k0:k0 + kk, :], preferred_element_type=F32)
        acc = part if acc is None else acc + part
        k0 += kk
    pos = 2 * n_parts + 1
    if x_split:
        x = jnp.where(is_p, refs[pos][...], refs[pos + 1][...])
        pos += 2
    else:
        x = refs[pos][...]
        pos += 1
    mod_ref, o_ref = refs[pos], refs[pos + 1]
    o_ref[...] = x + mod_ref[0, 2:3, :] * acc


def _linear_residual(cfg, parts, w, x, mod_l):
    d = cfg.d_model
    tm = min(256, cfg.tm)
    n_pb = cfg.n_p // tm
    n_sb = cfg.n_s // tm
    p_idx = lambda i: (jnp.minimum(i, n_pb - 1), 0)
    s_idx = lambda i: (jnp.maximum(i - n_pb, 0), 0)
    in_specs, args = [], []
    for a_p, a_s in parts:
        kk = a_p.shape[1]
        in_specs += [pl.BlockSpec((tm, kk), p_idx), pl.BlockSpec((tm, kk), s_idx)]
        args += [a_p, a_s]
    in_specs.append(pl.BlockSpec(w.shape, lambda i: (0, 0)))
    args.append(w)
    x_split = isinstance(x, tuple)
    if x_split:
        in_specs += [pl.BlockSpec((tm, d), p_idx), pl.BlockSpec((tm, d), s_idx)]
        args += list(x)
    else:
        in_specs.append(pl.BlockSpec((tm, d), lambda i: (i, 0)))
        args.append(x)
    mi = _merged_mod_index(cfg, tm)
    in_specs.append(pl.BlockSpec((1, 8, d), lambda i: (mi(i), 0, 0)))
    args.append(mod_l)
    kern = functools.partial(_linres_kernel, n_parts=len(parts), n_pb=n_pb, x_split=x_split)
    return pl.pallas_call(
        kern,
        grid=(n_pb + n_sb,),
        in_specs=in_specs,
        out_specs=pl.BlockSpec((tm, d), lambda i: (i, 0)),
        out_shape=jax.ShapeDtypeStruct((cfg.n_tok, d), F32),
        compiler_params=_params(("arbitrary",)),
        name="linear_gated_residual",
    )(*args)


def _dft_chan_kernel(x_ref, mod_ref, g_ref, cs_ref, y_ref, *, groups, eps):
    h = _norm_mod(x_ref[...], g_ref[...], mod_ref[0, 0:1, :], mod_ref[0, 1:2, :], eps).astype(BF16)
    gd = h.shape[1] // groups
    for g in range(groups):
        r = jnp.dot(h[:, g * gd:(g + 1) * gd], cs_ref[...], preferred_element_type=F32)
        y_ref[0, :, g * gd:(g + 1) * gd] = r[:, :gd].astype(BF16)
        y_ref[1, :, g * gd:(g + 1) * gd] = r[:, gd:].astype(BF16)


def _dft_mats(n):
    k = np.arange(n)
    ang = 2.0 * np.pi * ((k[:, None] * k[None, :]) % n) / n
    return np.cos(ang), np.sin(ang)


def _dft_channels(cfg, x, mod_l, gain):
    d = cfg.d_model
    gd = d // cfg.f_groups
    tm = cfg.tm
    c, s = _dft_mats(gd)
    cs = jnp.asarray(np.concatenate([c, s], axis=1), dtype=F32).astype(BF16)
    mi = _merged_mod_index(cfg, tm)
    return pl.pallas_call(
        functools.partial(_dft_chan_kernel, groups=cfg.f_groups, eps=cfg.norm_eps),
        grid=(cfg.n_tok // tm,),
        in_specs=[
            pl.BlockSpec((tm, d), lambda i: (i, 0)),
            pl.BlockSpec((1, 8, d), lambda i: (mi(i), 0, 0)),
            pl.BlockSpec((1, d), lambda i: (0, 0)),
            pl.BlockSpec((gd, 2 * gd), lambda i: (0, 0)),
        ],
        out_specs=pl.BlockSpec((2, tm, d), lambda i: (0, i, 0)),
        out_shape=jax.ShapeDtypeStruct((2, cfg.n_tok, d), BF16),
        compiler_params=_params(("arbitrary",)),
        name="dft_channels",
    )(x, mod_l, gain, cs)


def _dft_seq_kernel(w_ref, y_ref, o_ref, *, scale):
    acc = (jnp.dot(w_ref[0], y_ref[0], preferred_element_type=F32)
           + jnp.dot(w_ref[1], y_ref[1], preferred_element_type=F32))
    o_ref[...] = (acc * scale).astype(o_ref.dtype)


def _dft_sequence(cfg, y, n_batch, seq_len, first_block):
    d = cfg.d_model
    c, s = _dft_mats(seq_len)
    wm = jnp.asarray(np.stack([c, -s]), dtype=F32).astype(BF16)
    tml = min(1024, seq_len)
    tn = min(512, d)
    nm = seq_len // tml
    scale = 1.0 / math.sqrt(seq_len * (d // cfg.f_groups))
    return pl.pallas_call(
        functools.partial(_dft_seq_kernel, scale=scale),
        grid=(n_batch, nm, d // tn),
        in_specs=[
            pl.BlockSpec((2, tml, seq_len), lambda b, mi, j: (0, mi, 0)),
            pl.BlockSpec((2, seq_len, tn), lambda b, mi, j: (0, first_block + b, j)),
        ],
        out_specs=pl.BlockSpec((tml, tn), lambda b, mi, j: (b * nm + mi, j)),
        out_shape=jax.ShapeDtypeStruct((n_batch * seq_len, d), BF16),
        compiler_params=_params(("arbitrary", "arbitrary", "arbitrary")),
        name="dft_sequence_%d" % seq_len,
    )(wm, y)


def _route_kernel(x_ref, mod_ref, g_ref, wr_ref, rb_ref, hp_ref, idx_ref, wt_ref, *, eps, n_exp, per_grp):
    h = _norm_mod(x_ref[...], g_ref[...], mod_ref[0, 3:4, :], mod_ref[0, 4:5, :], eps)
    tm = h.shape[0]
    n_chunk = h.shape[1] // LANES
    for c in range(n_chunk):
        hp_ref[pl.ds(c, tm, stride=n_chunk), :] = h[:, c * LANES:(c + 1) * LANES]

    logits = lax.dot_general(wr_ref[...], h, (((1,), (1,)), ((), ())),
                             precision=lax.Precision.HIGHEST, preferred_element_type=F32)
    scores = jax.nn.sigmoid(logits)
    sel = scores + rb_ref[...]
    n_grp = n_exp // per_grp
    best = None
    gi = None
    for g in range(n_grp):
        v = [sel[g * per_grp + k:g * per_grp + k + 1, :] for k in range(per_grp)]
        gs = None
        for a in range(per_grp):
            for b in range(a + 1, per_grp):
                ps = v[a] + v[b]
                gs = ps if gs is None else jnp.maximum(gs, ps)
        if best is None:
            best, gi = gs, jnp.zeros(gs.shape, jnp.int32)
        else:
            better = gs > best
            gi = jnp.where(better, g, gi)
            best = jnp.where(better, gs, best)
    row = lax.broadcasted_iota(jnp.int32, sel.shape, 0)
    masked = jnp.where(row // per_grp == gi, sel, -jnp.inf)
    m1 = jnp.max(masked, axis=0, keepdims=True)
    i1 = jnp.min(jnp.where(masked == m1, row, n_exp), axis=0, keepdims=True)
    masked2 = jnp.where(row == i1, -jnp.inf, masked)
    m2 = jnp.max(masked2, axis=0, keepdims=True)
    i2 = jnp.min(jnp.where(masked2 == m2, row, n_exp), axis=0, keepdims=True)
    w1 = jnp.sum(jnp.where(row == i1, scores, 0.0), axis=0, keepdims=True)
    w2 = jnp.sum(jnp.where(row == i2, scores, 0.0), axis=0, keepdims=True)
    inv = 1.0 / (w1 + w2)
    idx_ref[...] = jnp.concatenate([i1, i2], axis=0)
    wt_ref[...] = jnp.concatenate([w1 * inv, w2 * inv], axis=0)


def _route(cfg, x, mod_l, gain, w_router_t, router_bias):
    d = cfg.d_model
    tm = cfg.tm
    t = cfg.n_tok
    mi = _merged_mod_index(cfg, tm)
    kern = functools.partial(_route_kernel, eps=cfg.norm_eps, n_exp=cfg.n_experts,
                             per_grp=cfg.n_experts // cfg.n_groups)
    return pl.pallas_call(
        kern,
        grid=(t // tm,),
        in_specs=[
            pl.BlockSpec((tm, d), lambda i: (i, 0)),
            pl.BlockSpec((1, 8, d), lambda i: (mi(i), 0, 0)),
            pl.BlockSpec((1, d), lambda i: (0, 0)),
            pl.BlockSpec((cfg.n_experts, d), lambda i: (0, 0)),
            pl.BlockSpec((cfg.n_experts, 1), lambda i: (0, 0)),
        ],
        out_specs=[
            pl.BlockSpec((tm * (d // LANES), LANES), lambda i: (i, 0)),
            pl.BlockSpec((2, tm), lambda i: (0, i)),
            pl.BlockSpec((2, tm), lambda i: (0, i)),
        ],
        out_shape=[
            jax.ShapeDtypeStruct((t * (d // LANES), LANES), F32),
            jax.ShapeDtypeStruct((2, t), jnp.int32),
            jax.ShapeDtypeStruct((2, t), F32),
        ],
        compiler_params=_params(("arbitrary",)),
        name="moe_route",
    )(x, mod_l, gain, w_router_t, router_bias.reshape(cfg.n_experts, 1))


def _plan(cfg, idx):
    t = cfg.n_tok
    eb = cfg.e_block
    ne = cfg.n_experts
    n_blocks = -(-(2 * t + ne * (eb - 1)) // eb)
    e_flat = idx.reshape(-1)
    onehot = (e_flat[:, None] == jnp.arange(ne, dtype=jnp.int32)[None, :]).astype(jnp.int32)
    csum = jnp.cumsum(onehot, axis=0)
    rank = jnp.sum(onehot * (csum - 1), axis=1)
    counts = csum[-1]
    padded = (counts + eb - 1) // eb * eb
    pad_end = jnp.cumsum(padded)
    pad_start = pad_end - padded
    pos = (pad_start[e_flat] + rank).astype(jnp.int32).reshape(2, t)
    n_used = (pad_end[-1] // eb).astype(jnp.int32)
    blk = jnp.arange(n_blocks, dtype=jnp.int32)
    blk = jnp.minimum(blk, n_used - 1)
    first_e = jnp.sum((pad_end[None, :] <= (blk * eb)[:, None]).astype(jnp.int32), axis=1)
    block_e = jnp.minimum(first_e, ne - 1).astype(jnp.int32)
    return pos, block_e, n_used.reshape(1), pad_end.astype(jnp.int32), n_blocks


def _weight_schedule(cfg, block_e, n_used, n_blocks):
    n_chunks = _expert_chunks(cfg)[2]
    big = n_blocks + 1
    idx = jnp.arange(n_blocks, dtype=jnp.int32)
    valid = idx < n_used[0]
    prev_e = jnp.concatenate([jnp.full((1,), -1, jnp.int32), block_e[:-1]])
    change = (block_e != prev_e) & valid
    earlier = (idx[None, :] <= idx[:, None]) & change[None, :]
    later = (idx[None, :] > idx[:, None]) & change[None, :]
    run_start = jnp.max(jnp.where(earlier, idx[None, :], 0), axis=1)
    next_start = jnp.minimum(jnp.min(jnp.where(later, idx[None, :], big), axis=1), n_used[0])
    slot = (jnp.sum(earlier.astype(jnp.int32), axis=1) - 1) % 2
    has_next = valid & (next_start < n_used[0])
    next_e = jnp.where(has_next, block_e[jnp.minimum(next_start, n_blocks - 1)], 0)
    run_len = jnp.maximum(next_start - run_start, 1)
    per_block = (n_chunks + run_len - 1) // run_len
    i = idx - run_start
    c0 = jnp.where(has_next, jnp.minimum(i * per_block, n_chunks), 0)
    c1 = jnp.where(has_next, jnp.minimum((i + 1) * per_block, n_chunks), 0)
    return tuple(a.astype(jnp.int32) for a in (slot, next_e, c0, c1))


def _dispatch_kernel(pend_ref, pos_ref, hp_ref, xr_ref, zbuf, sem, zsem, *, n_exp, eb, spt):
    i = pl.program_id(0)

    @pl.when(i == 0)
    def _():
        zbuf[...] = jnp.zeros(zbuf.shape, zbuf.dtype)
        for e in range(n_exp):
            start = pl.multiple_of(jnp.maximum(pend_ref[e] - eb, 0) * spt, eb * spt)
            cp = pltpu.make_async_copy(zbuf, xr_ref.at[pl.ds(start, eb * spt)], zsem)
            cp.start()
            cp.wait()
        n_used = pend_ref[n_exp - 1] // eb
        n_blocks = xr_ref.shape[0] // (eb * spt)
        for e in range(n_exp):
            @pl.when(n_used + e < n_blocks)
            def _():
                start = pl.multiple_of((n_used + e) * (eb * spt), eb * spt)
                cp = pltpu.make_async_copy(zbuf, xr_ref.at[pl.ds(start, eb * spt)], zsem)
                cp.start()
                cp.wait()

    rows = pos_ref.shape[2]

    def start(r, c):
        src = hp_ref.at[pl.ds(pl.multiple_of(r * spt, spt), spt)]
        for k in range(2):
            dst = xr_ref.at[pl.ds(pl.multiple_of(pos_ref[0, k, r] * spt, spt), spt)]
            pltpu.make_async_copy(src, dst, sem.at[k]).start()
        return c

    lax.fori_loop(0, rows, start, 0, unroll=8)
    for k in range(2):
        pltpu.make_async_copy(hp_ref, xr_ref.at[pl.ds(0, rows * spt)], sem.at[k]).wait()


def _dispatch(cfg, hp, pos_blocks, pad_end, n_rows):
    tb = pos_blocks.shape[2]
    spt = cfg.d_model // LANES
    kern = functools.partial(_dispatch_kernel, n_exp=cfg.n_experts, eb=cfg.e_block, spt=spt)
    return pl.pallas_call(
        kern,
        grid_spec=pltpu.PrefetchScalarGridSpec(
            num_scalar_prefetch=1,
            grid=(cfg.n_tok // tb,),
            in_specs=[
                pl.BlockSpec((1, 2, tb), lambda i, pe: (i, 0, 0), memory_space=pltpu.SMEM),
                pl.BlockSpec((tb * spt, LANES), lambda i, pe: (i, 0)),
            ],
            out_specs=pl.BlockSpec(memory_space=pl.ANY),
            scratch_shapes=[
                pltpu.VMEM((cfg.e_block * spt, LANES), F32),
                pltpu.SemaphoreType.DMA((2,)),
                pltpu.SemaphoreType.DMA(()),
            ],
        ),
        out_shape=jax.ShapeDtypeStruct((n_rows * spt, LANES), F32),
        compiler_params=_params(("arbitrary",)),
        name="moe_dispatch",
    )(pad_end, pos_blocks, hp)


EXPERT_CHUNK_ROWS = 128
EXPERT_RING = 6
EXPERT_VMEM_LIMIT = 60 * 1024 * 1024


def _expert_chunks(cfg):
    n_in = cfg.d_model // EXPERT_CHUNK_ROWS
    n_out = cfg.d_expert // EXPERT_CHUNK_ROWS
    return n_in, n_out, 2 * n_in + n_out


def _expert_kernel(nu_ref, e0_ref, slot_ref, ne_ref, c0_ref, c1_ref, xp_ref, wg_ref, wu_ref, wd_ref, y_ref,
                   wg_b, wu_b, wd_b, stg, sem, *, n_in, n_out, layer):
    b = pl.program_id(0)
    n_chunks = 2 * n_in + n_out
    ring = stg.shape[0]
    rows = EXPERT_CHUNK_ROWS

    parts = [(0, n_in, wg_ref, wg_b), (n_in, n_in, wu_ref, wu_b), (2 * n_in, n_out, wd_ref, wd_b)]

    def for_chunk(c, e, fn):
        s = c % ring
        for first, count, src, dst in parts:
            @pl.when((c >= first) & (c < first + count))
            def _():
                row = pl.multiple_of((c - first) * rows, rows)
                window = stg.at[s, :, pl.ds(0, dst.shape[2])]
                cp = pltpu.make_async_copy(src.at[layer, e, pl.ds(row, rows)], window, sem.at[s])
                fn(cp, dst, row, window)

    def start_chunk(c, e):
        for_chunk(c, e, lambda cp, dst, row, window: cp.start())

    def finish_chunk(c, e, dst_slot):
        def cast(cp, dst, row, window):
            cp.wait()
            dst[dst_slot, pl.ds(row, rows), :] = window[...].astype(BF16)
        for_chunk(c, e, cast)

    def load_chunks(e, dst_slot, lo, hi):
        @pl.when((lo == 0) & (hi > 0))
        def _():
            for k in range(min(ring, n_chunks)):
                start_chunk(k, e)

        def body(c, carry):
            finish_chunk(c, e, dst_slot)

            @pl.when(c + ring < n_chunks)
            def _():
                start_chunk(c + ring, e)
            return carry

        lax.fori_loop(lo, hi, body, 0)

    @pl.when(b == 0)
    def _():
        load_chunks(e0_ref[0], 0, 0, n_chunks)

    @pl.when(b < nu_ref[0])
    def _():
        slot = slot_ref[b]
        load_chunks(ne_ref[b], 1 - slot, c0_ref[b], c1_ref[b])
        d = wg_b.shape[1]
        n_chunk = d // LANES
        eb = xp_ref.shape[0] // n_chunk
        x = jnp.concatenate([xp_ref[pl.ds(c, eb, stride=n_chunk), :].astype(BF16) for c in range(n_chunk)],
                            axis=1)
        g = jnp.dot(x, wg_b[slot], preferred_element_type=F32)
        u = jnp.dot(x, wu_b[slot], preferred_element_type=F32)
        a = (_silu(g) * u).astype(BF16)
        y = jnp.dot(a, wd_b[slot], preferred_element_type=F32)
        for c in range(n_chunk):
            y_ref[pl.ds(c, eb, stride=n_chunk), :] = y[:, c * LANES:(c + 1) * LANES]

    @pl.when(b >= nu_ref[0])
    def _():
        y_ref[...] = jnp.zeros(y_ref.shape, y_ref.dtype)


def _experts(cfg, x_rows, block_e, n_used, wg, wu, wd, layer, n_blocks):
    d = cfg.d_model
    f = cfg.d_expert
    eb = cfg.e_block
    n_chunk = d // LANES
    n_in, n_out, _ = _expert_chunks(cfg)
    slot, next_e, c0, c1 = _weight_schedule(cfg, block_e, n_used, n_blocks)
    any_spec = pl.BlockSpec(memory_space=pl.ANY)
    x_idx = lambda b, nu, *_: (jnp.minimum(b, nu[0] - 1), 0)
    return pl.pallas_call(
        functools.partial(_expert_kernel, n_in=n_in, n_out=n_out, layer=layer),
        grid_spec=pltpu.PrefetchScalarGridSpec(
            num_scalar_prefetch=6,
            grid=(n_blocks,),
            in_specs=[pl.BlockSpec((eb * n_chunk, LANES), x_idx), any_spec, any_spec, any_spec],
            out_specs=pl.BlockSpec((eb * n_chunk, LANES), lambda b, *_: (b, 0)),
            scratch_shapes=[
                pltpu.VMEM((2, d, f), BF16), pltpu.VMEM((2, d, f), BF16), pltpu.VMEM((2, f, d), BF16),
                pltpu.VMEM((EXPERT_RING, EXPERT_CHUNK_ROWS, max(d, f)), F32),
                pltpu.SemaphoreType.DMA((EXPERT_RING,)),
            ],
        ),
        out_shape=jax.ShapeDtypeStruct((n_blocks * eb * n_chunk, LANES), F32),
        compiler_params=pltpu.CompilerParams(dimension_semantics=("arbitrary",),
                                             vmem_limit_bytes=EXPERT_VMEM_LIMIT),
        name="moe_experts",
    )(n_used, block_e[:1], slot, next_e, c0, c1, x_rows, wg, wu, wd)


def _combine_kernel(pos_ref, nxt_ref, y_ref, x_ref, mod_ref, wt_ref, fg_ref, o_ref, ybuf, sem, *, final, eps):
    i = pl.program_id(0)
    n_steps = pl.num_programs(0)
    rows, d = x_ref.shape
    n_chunk = d // LANES
    slot = i % 2
    per_choice = rows * n_chunk

    def gather(p_ref, s):
        def start(r, c):
            for k in range(2):
                src = y_ref.at[pl.ds(pl.multiple_of(p_ref[0, k, r] * n_chunk, n_chunk), n_chunk)]
                dst = ybuf.at[s, pl.ds(pl.multiple_of(k * per_choice + r * n_chunk, n_chunk), n_chunk)]
                pltpu.make_async_copy(src, dst, sem.at[s]).start()
            return c
        lax.fori_loop(0, rows, start, 0, unroll=8)

    @pl.when(i == 0)
    def _():
        gather(pos_ref, 0)

    @pl.when(i + 1 < n_steps)
    def _():
        gather(nxt_ref, 1 - slot)

    pltpu.make_async_copy(y_ref.at[pl.ds(0, 2 * per_choice)], ybuf.at[slot], sem.at[slot]).wait()

    w = wt_ref[...]
    w0, w1 = w[:, 0:1], w[:, 1:2]
    sumsq = jnp.zeros((rows, 1), F32)
    for c in range(n_chunk):
        cols = slice(c * LANES, (c + 1) * LANES)
        moe = (w0 * ybuf[slot, pl.ds(c, rows, stride=n_chunk), :]
               + w1 * ybuf[slot, pl.ds(per_choice + c, rows, stride=n_chunk), :])
        xc = x_ref[:, cols] + mod_ref[0, 5:6, cols] * moe
        o_ref[:, cols] = xc
        sumsq = sumsq + jnp.sum(xc * xc, axis=-1, keepdims=True)
    if final:
        o_ref[...] = (o_ref[...] * lax.rsqrt(sumsq * (1.0 / d) + eps)) * fg_ref[...]


def _combine(cfg, y, pos_blocks, wts_t, x, mod_l, final_gain, *, row0, n_rows, final):
    d = cfg.d_model
    tb = pos_blocks.shape[2]
    b0 = row0 // tb
    mi = _merged_mod_index(cfg, tb)
    kern = functools.partial(_combine_kernel, final=final, eps=cfg.norm_eps)
    n_steps = n_rows // tb
    return pl.pallas_call(
        kern,
        grid=(n_steps,),
        in_specs=[
            pl.BlockSpec((1, 2, tb), lambda i: (b0 + i, 0, 0), memory_space=pltpu.SMEM),
            pl.BlockSpec((1, 2, tb), lambda i: (b0 + jnp.minimum(i + 1, n_steps - 1), 0, 0),
                         memory_space=pltpu.SMEM),
            pl.BlockSpec(memory_space=pl.ANY),
            pl.BlockSpec((tb, d), lambda i: (b0 + i, 0)),
            pl.BlockSpec((1, 8, d), lambda i: (mi(b0 + i), 0, 0)),
            pl.BlockSpec((tb, 2), lambda i: (b0 + i, 0)),
            pl.BlockSpec((1, d), lambda i: (0, 0)),
        ],
        out_specs=pl.BlockSpec((tb, d), lambda i: (i, 0)),
        out_shape=jax.ShapeDtypeStruct((n_rows, d), F32),
        scratch_shapes=[pltpu.VMEM((2, 2 * tb * (d // LANES), LANES), F32), pltpu.SemaphoreType.DMA((2,))],
        compiler_params=_params(("arbitrary",)),
        name="moe_combine_final" if final else "moe_combine",
    )(pos_blocks, pos_blocks, y, x, mod_l, wts_t, final_gain)


def _moe(cfg, x, mod_l, gain, w_router_t, router_bias, wg, wu, wd, layer, final_gain, final):
    tb = min(256, cfg.tm)
    hp, idx, wts = _route(cfg, x, mod_l, gain, w_router_t, router_bias)
    pos, block_e, n_used, pad_end, n_blocks = _plan(cfg, idx)
    pos_blocks = pos.reshape(2, cfg.n_tok // tb, tb).transpose(1, 0, 2)
    td = cfg.tm
    x_rows = _dispatch(cfg, hp, pos.reshape(2, cfg.n_tok // td, td).transpose(1, 0, 2), pad_end,
                       n_blocks * cfg.e_block)
    y = _experts(cfg, x_rows, block_e, n_used, wg, wu, wd, layer, n_blocks)
    wts_t = wts.T
    comb = functools.partial(_combine, cfg, y, pos_blocks, wts_t, x, mod_l, final_gain, final=final)
    if final:
        return comb(row0=0, n_rows=cfg.n_p), comb(row0=cfg.n_p, n_rows=cfg.n_s)
    return comb(row0=0, n_rows=cfg.n_tok)


def _forward(cfg, x_prompt, x_sample, cache_a_k, cache_a_v, cache_b_k, cache_b_v, c, c_ctx, w_ada, b_ada,
             norm1, norm2, final_norm, w_in, w_mix_out, lambda_q1, lambda_k1, lambda_q2, lambda_k2,
             subln_gain, na_rel_bias, w_fourier_out, w_router, router_bias, w_exp_gate, w_exp_up,
             w_exp_down):
    d = cfg.d_model
    xp = x_prompt.reshape(cfg.n_p, d)
    xs = x_sample.reshape(cfg.n_s, d)

    n_cond = -(-cfg.n_mod // 8) * 8
    cond = jnp.concatenate([c_ctx[None, :], c, jnp.zeros((n_cond - cfg.n_mod, d), F32)], axis=0)
    mod = _modulation(cfg, cond, w_ada, b_ada)
    mod = mod.reshape(cfg.depth, n_cond, 6, d)[:, :cfg.n_mod]
    mod = jnp.pad(mod, ((0, 0), (0, 0), (0, 2), (0, 0)))

    w_router_t = w_router.T
    fgain = final_norm.reshape(1, d)
    x = (xp, xs)
    kv_cache = None
    for l in range(cfg.depth):
        j = l // 2
        g1 = norm1[l].reshape(1, d)
        if l % 2 == 0:
            lam_init = 0.8 - 0.6 * math.exp(-0.3 * l)
            lam_pack = jnp.zeros((8, LANES), F32).at[:4, :cfg.a_qk].set(
                jnp.stack([lambda_q1[j], lambda_k1[j], lambda_q2[j], lambda_k2[j]]))
            sgain = subln_gain[j].reshape(1, cfg.a_dim)
            w_in_b = w_in[j].astype(BF16)
            if isinstance(x, tuple):
                x_p, x_s = x
            else:
                x_p, x_s = x[:cfg.n_p], x[cfg.n_p:]
            proj_p = _projection(cfg, x_p, mod[l], g1, w_in_b, latent=False, out_dtype=F32)
            proj_s = _projection(cfg, x_s, mod[l], g1, w_in_b, latent=True, out_dtype=BF16)
            if kv_cache is None:
                kv_cache = []
            aw, bw = cfg.a_width, cfg.b_width
            kv_cache.append((proj_p[:, aw:2 * aw], proj_p[:, 2 * aw:3 * aw],
                             proj_p[:, 3 * aw + bw:3 * aw + 2 * bw], proj_p[:, 3 * aw + 2 * bw:]))
            cak = cache_a_k[:, j].reshape(cfg.dec_batch, cfg.past_len, aw)
            cav = cache_a_v[:, j].reshape(cfg.dec_batch, cfg.past_len, aw)
            cbk = cache_b_k[:, j].reshape(cfg.dec_batch, cfg.past_len, bw)
            cbv = cache_b_v[:, j].reshape(cfg.dec_batch, cfg.past_len, bw)
            a_p = _diff_attention(cfg, proj_p, lam_pack, sgain, cfg.batch, cfg.seq, lam_init)
            b_p = _soft_attention(cfg, proj_p, cfg.batch, cfg.seq)
            a_s = _diff_attention(cfg, proj_s, lam_pack, sgain, cfg.dec_batch, cfg.dec_seq, lam_init,
                                  ctx=(cak, cav))
            b_s = _na_attention(cfg, proj_s, cbk, cbv, na_rel_bias[j])
            x = _linear_residual(cfg, [(a_p, a_s), (b_p, b_s)], w_mix_out[j].astype(BF16), x, mod[l])
        else:
            if isinstance(x, tuple):
                x = jnp.concatenate(x, axis=0)
            y = _dft_channels(cfg, x, mod[l], g1)
            f_p = _dft_sequence(cfg, y, cfg.batch, cfg.seq, 0)
            f_s = _dft_sequence(cfg, y, cfg.dec_batch, cfg.dec_seq, cfg.n_p // cfg.dec_seq)
            x = _linear_residual(cfg, [(f_p, f_s)], w_fourier_out[j].astype(BF16), x, mod[l])
        x = _moe(cfg, x, mod[l], norm2[l].reshape(1, d), w_router_t, router_bias,
                 w_exp_gate, w_exp_up, w_exp_down, l, fgain, final=(l == cfg.depth - 1))
    y_p, y_s = x
    n_even = (cfg.depth + 1) // 2
    outs = [y_p.reshape(cfg.batch, cfg.seq, d), y_s.reshape(cfg.dec_batch, cfg.dec_seq, d)]
    for t in range(4):
        heads, hd = (cfg.a_heads, cfg.a_dim) if t < 2 else (cfg.b_heads, cfg.b_dim)
        stacked = jnp.stack([kv_cache[jj][t].reshape(cfg.batch, cfg.seq, heads, hd)
                             for jj in range(n_even)], axis=1)
        outs.append(stacked)
    return tuple(outs)


def kernel(x_prompt, x_sample, cache_a_k, cache_a_v, cache_b_k, cache_b_v, c, c_ctx, w_ada, b_ada, norm1, norm2, final_norm, w_in, w_mix_out, lambda_q1, lambda_k1, lambda_q2, lambda_k2, subln_gain, na_rel_bias, w_fourier_out, w_router, router_bias, w_exp_gate, w_exp_up, w_exp_down):
    return _forward(Cfg(), x_prompt, x_sample, cache_a_k, cache_a_v, cache_b_k, cache_b_v, c, c_ctx, w_ada,
                    b_ada, norm1, norm2, final_norm, w_in, w_mix_out, lambda_q1, lambda_k1, lambda_q2,
                    lambda_k2, subln_gain, na_rel_bias, w_fourier_out, w_router, router_bias, w_exp_gate,
                    w_exp_up, w_exp_down)
```

```python
import functools
import math
from typing import NamedTuple

import numpy as np
import jax
import jax.numpy as jnp
from jax import lax
from jax.experimental import pallas as pl
from jax.experimental.pallas import tpu as pltpu

F32 = jnp.float32
BF16 = jnp.bfloat16

LANES = 128
NEG_BIG = -1e30
VMEM_LIMIT = 56 * 1024 * 1024


class Cfg(NamedTuple):
    d_model: int = 2048
    batch: int = 16
    seq: int = 256
    depth: int = 2
    dec_batch: int = 8
    dec_seq: int = 2048
    past_len: int = 256
    grid_w: int = 64
    a_heads: int = 8
    a_qk: int = 64
    b_heads: int = 8
    b_dim: int = 128
    na_win_r: int = 8
    na_win_c: int = 16
    f_groups: int = 4
    n_experts: int = 16
    n_groups: int = 4
    d_expert: int = 1408
    rope_theta: float = 10000.0
    norm_eps: float = 1e-6
    subln_eps: float = 1e-5
    row_tile: int = 512
    q_tile: int = 512
    e_block: int = 256

    @property
    def a_dim(self):
        return 2 * self.a_qk

    @property
    def a_width(self):
        return self.a_heads * self.a_dim

    @property
    def b_width(self):
        return self.b_heads * self.b_dim

    @property
    def n_p(self):
        return self.batch * self.seq

    @property
    def n_s(self):
        return self.dec_batch * self.dec_seq

    @property
    def n_tok(self):
        return self.n_p + self.n_s

    @property
    def tm(self):
        return min(self.row_tile, self.n_p, self.dec_seq)

    @property
    def n_mod(self):
        return 1 + self.dec_batch


def _params(sem):
    return pltpu.CompilerParams(dimension_semantics=sem, vmem_limit_bytes=VMEM_LIMIT)


def _norm_mod(x, g, sh, sc, eps):
    ms = jnp.mean(x * x, axis=-1, keepdims=True)
    y = x * lax.rsqrt(ms + eps)
    return (y * g) * (1.0 + sc) + sh


def _silu(x):
    return x * jax.nn.sigmoid(x)


def _merged_mod_index(cfg, tm):
    def f(i):
        r = i * tm
        return jnp.where(r < cfg.n_p, 0, 1 + (r - cfg.n_p) // cfg.dec_seq)
    return f


def _ada_kernel(cond_ref, w_ref, b_ref, o_ref):
    s = _silu(cond_ref[...])
    s_hi = s.astype(BF16)
    s_lo = (s - s_hi.astype(F32)).astype(BF16)
    lhs = jnp.concatenate([s_hi, s_lo], axis=0)
    r = jnp.dot(lhs, w_ref[0].astype(BF16), preferred_element_type=F32)
    n = s.shape[0]
    o_ref[0] = r[:n] + r[n:] + b_ref[0]


def _modulation(cfg, cond, w_ada, b_ada):
    d = cfg.d_model
    r = cond.shape[0]
    tn = math.gcd(1024, 6 * d)
    return pl.pallas_call(
        _ada_kernel,
        grid=(cfg.depth, 6 * d // tn),
        in_specs=[
            pl.BlockSpec((r, d), lambda l, j: (0, 0)),
            pl.BlockSpec((1, d, tn), lambda l, j: (l, 0, j)),
            pl.BlockSpec((1, 1, tn), lambda l, j: (l, 0, j)),
        ],
        out_specs=pl.BlockSpec((1, r, tn), lambda l, j: (l, 0, j)),
        out_shape=jax.ShapeDtypeStruct((cfg.depth, r, 6 * d), F32),
        compiler_params=_params(("arbitrary", "arbitrary")),
        name="ada_modulation",
    )(cond, w_ada, b_ada.reshape(cfg.depth, 1, 6 * d))


PROJ_CHUNK = 256


def _proj_kernel(*refs, rope, n_rope_blocks, eps):
    if rope:
        x_ref, mod_ref, g_ref, w_ref, cos_ref, sa_ref, sb_ref, o_ref, h_ref = refs
    else:
        x_ref, mod_ref, g_ref, w_ref, o_ref, h_ref = refs
    j = pl.program_id(1)

    @pl.when(j == 0)
    def _():
        h = _norm_mod(x_ref[...], g_ref[...], mod_ref[0, 0:1, :], mod_ref[0, 1:2, :], eps)
        h_ref[...] = h.astype(BF16)

    if not rope:
        o_ref[...] = jnp.dot(h_ref[...], w_ref[...], preferred_element_type=F32).astype(o_ref.dtype)
        return

    @pl.when(j < n_rope_blocks)
    def _():
        cos, sa, sb = cos_ref[...], sa_ref[...], sb_ref[...]
        h = h_ref[...]
        tn = w_ref.shape[1]
        cw = min(PROJ_CHUNK, tn)
        for c0 in range(0, tn, cw):
            acc = jnp.dot(h, w_ref[:, c0:c0 + cw], preferred_element_type=F32)
            for c in range(cw // LANES):
                xa = acc[:, c * LANES:(c + 1) * LANES]
                up = pltpu.roll(xa, LANES - 16, 1)
                dn = pltpu.roll(xa, 16, 1)
                lo = c0 + c * LANES
                o_ref[:, lo:lo + LANES] = (xa * cos + up * sa + dn * sb).astype(o_ref.dtype)

    @pl.when(j >= n_rope_blocks)
    def _():
        o_ref[...] = jnp.dot(h_ref[...], w_ref[...], preferred_element_type=F32).astype(o_ref.dtype)


def _rope_tables(cfg):
    t = np.arange(cfg.dec_seq)
    row = (t // cfg.grid_w).astype(np.float64)
    col = (t % cfg.grid_w).astype(np.float64)
    lane = np.arange(LANES)
    l64 = lane % cfg.a_qk
    half = cfg.a_qk // 4
    freq = cfg.rope_theta ** (-(lane % half).astype(np.float64) / half)
    pos = np.where((l64 < cfg.a_qk // 2)[None, :], row[:, None], col[:, None])
    ang = pos * freq[None, :]
    first = (lane % (2 * half)) < half
    cos = np.cos(ang)
    sin = np.sin(ang)
    sa = np.where(first[None, :], -sin, 0.0)
    sb = np.where(first[None, :], 0.0, sin)
    return tuple(jnp.asarray(a, dtype=F32) for a in (cos, sa, sb))


def _projection(cfg, x, mod_l, gain, w, *, latent, out_dtype):
    m, d = x.shape
    n = w.shape[1]
    tm = min(cfg.row_tile, m)
    tn = min(1024, cfg.a_width)
    per_seq = cfg.dec_seq // tm if latent else 1
    mod_idx = (lambda i, j: (1 + i // per_seq, 0, 0)) if latent else (lambda i, j: (0, 0, 0))
    in_specs = [
        pl.BlockSpec((tm, d), lambda i, j: (i, 0)),
        pl.BlockSpec((1, 8, d), mod_idx),
        pl.BlockSpec((1, d), lambda i, j: (0, 0)),
        pl.BlockSpec((d, tn), lambda i, j: (0, j)),
    ]
    args = [x, mod_l, gain, w]
    if latent:
        tab_spec = pl.BlockSpec((tm, LANES), lambda i, j: (i % per_seq, 0))
        in_specs += [tab_spec, tab_spec, tab_spec]
        args += list(_rope_tables(cfg))
    kern = functools.partial(_proj_kernel, rope=latent, n_rope_blocks=2 * cfg.a_width // tn,
                             eps=cfg.norm_eps)
    return pl.pallas_call(
        kern,
        grid=(m // tm, n // tn),
        in_specs=in_specs,
        out_specs=pl.BlockSpec((tm, tn), lambda i, j: (i, j)),
        out_shape=jax.ShapeDtypeStruct((m, n), out_dtype),
        scratch_shapes=[pltpu.VMEM((tm, d), BF16)],
        compiler_params=_params(("arbitrary", "arbitrary")),
        name="qkv_projection_latent" if latent else "qkv_projection_context",
    )(*args)


def _nt_dot(a, b):
    return lax.dot_general(a, b, (((1,), (1,)), ((), ())), preferred_element_type=F32)


DIFF_SUB_TILE = 128


def _diff_attn_kernel(*refs, has_ctx, lam_init, eps, qk):
    if has_ctx:
        lam_ref, gain_ref, q_ref, k_ref, v_ref, ck_ref, cv_ref, o_ref = refs
    else:
        lam_ref, gain_ref, q_ref, k_ref, v_ref, o_ref = refs
    lv = lam_ref[...]
    s1 = jnp.sum(lv[0:1] * lv[1:2], axis=-1, keepdims=True)
    s2 = jnp.sum(lv[2:3] * lv[3:4], axis=-1, keepdims=True)
    lam = jnp.exp(s1) - jnp.exp(s2) + lam_init

    k = k_ref[...].astype(BF16)
    v = v_ref[...].astype(BF16)
    if has_ctx:
        ck = ck_ref[0].astype(BF16)
        cv = cv_ref[0].astype(BF16)
    tq = q_ref.shape[0]
    ts = min(tq, DIFF_SUB_TILE)
    for t in range(tq // ts):
        q = q_ref[t * ts:(t + 1) * ts, :].astype(F32) * (qk ** -0.5 * math.log2(math.e))
        lane = lax.broadcasted_iota(jnp.int32, q.shape, 1)
        qs = jnp.concatenate([jnp.where(lane < qk, q, 0.0), jnp.where(lane >= qk, q, 0.0)],
                             axis=0).astype(BF16)
        s_new = _nt_dot(qs, k)
        m = jnp.max(s_new, axis=-1, keepdims=True)
        if has_ctx:
            s_ctx = _nt_dot(qs, ck)
            m = jnp.maximum(m, jnp.max(s_ctx, axis=-1, keepdims=True))
        p_new = jnp.exp2(s_new - m)
        den = jnp.sum(p_new, axis=-1, keepdims=True)
        o2 = jnp.dot(p_new.astype(BF16), v, preferred_element_type=F32)
        if has_ctx:
            p_ctx = jnp.exp2(s_ctx - m)
            den = den + jnp.sum(p_ctx, axis=-1, keepdims=True)
            o2 = o2 + jnp.dot(p_ctx.astype(BF16), cv, preferred_element_type=F32)
        inv = 1.0 / den
        o = o2[:ts] * inv[:ts] - o2[ts:] * (lam * inv[ts:])
        ms = jnp.mean(o * o, axis=-1, keepdims=True)
        o = (o * lax.rsqrt(ms + eps)) * gain_ref[...]
        o_ref[t * ts:(t + 1) * ts, :] = (o * (1.0 - lam_init)).astype(o_ref.dtype)


def _diff_attention(cfg, proj, lam_pack, gain, n_batch, seq_len, lam_init, ctx=None):
    hd = cfg.a_dim
    nh = cfg.a_heads
    tq = min(cfg.q_tile, seq_len)
    nq = seq_len // tq
    in_specs = [
        pl.BlockSpec((8, LANES), lambda b, h, qi: (0, 0)),
        pl.BlockSpec((1, hd), lambda b, h, qi: (0, 0)),
        pl.BlockSpec((tq, hd), lambda b, h, qi: (b * nq + qi, h)),
        pl.BlockSpec((seq_len, hd), lambda b, h, qi: (b, nh + h)),
        pl.BlockSpec((seq_len, hd), lambda b, h, qi: (b, 2 * nh + h)),
    ]
    args = [lam_pack, gain, proj, proj, proj]
    if ctx is not None:
        ck, cv = ctx
        past = ck.shape[1]
        cspec = pl.BlockSpec((1, past, hd), lambda b, h, qi: (b, 0, h))
        in_specs += [cspec, cspec]
        args += [ck, cv]
    kern = functools.partial(_diff_attn_kernel, has_ctx=ctx is not None, lam_init=lam_init,
                             eps=cfg.subln_eps, qk=cfg.a_qk)
    return pl.pallas_call(
        kern,
        grid=(n_batch, nh, nq),
        in_specs=in_specs,
        out_specs=pl.BlockSpec((tq, hd), lambda b, h, qi: (b * nq + qi, h)),
        out_shape=jax.ShapeDtypeStruct((n_batch * seq_len, cfg.a_width), BF16),
        compiler_params=_params(("arbitrary", "arbitrary", "arbitrary")),
        name="diff_attention_latent" if ctx is not None else "diff_attention_context",
    )(*args)


def _soft_attn_kernel(q_ref, k_ref, v_ref, o_ref, *, scale):
    s = _nt_dot(q_ref[...].astype(BF16), k_ref[...].astype(BF16)) * (scale * math.log2(math.e))
    m = jnp.max(s, axis=-1, keepdims=True)
    p = jnp.exp2(s - m)
    inv = 1.0 / jnp.sum(p, axis=-1, keepdims=True)
    o = jnp.dot(p.astype(BF16), v_ref[...].astype(BF16), preferred_element_type=F32) * inv
    o_ref[...] = o.astype(o_ref.dtype)


def _soft_attention(cfg, proj, n_batch, seq_len):
    hd = cfg.b_dim
    nh = cfg.b_heads
    base = 3 * cfg.a_width // hd
    return pl.pallas_call(
        functools.partial(_soft_attn_kernel, scale=hd ** -0.5),
        grid=(n_batch, nh),
        in_specs=[
            pl.BlockSpec((seq_len, hd), lambda b, h: (b, base + h)),
            pl.BlockSpec((seq_len, hd), lambda b, h: (b, base + nh + h)),
            pl.BlockSpec((seq_len, hd), lambda b, h: (b, base + 2 * nh + h)),
        ],
        out_specs=pl.BlockSpec((seq_len, hd), lambda b, h: (b, h)),
        out_shape=jax.ShapeDtypeStruct((n_batch * seq_len, cfg.b_width), BF16),
        compiler_params=_params(("arbitrary", "arbitrary")),
        name="softmax_attention_context",
    )(proj, proj, proj)


NA_GROUP_ROWS = 4


def _na_geometry(cfg):
    rows = cfg.dec_seq // cfg.grid_w
    kr = min(cfg.na_win_r, rows)
    grp = min(NA_GROUP_ROWS, rows)
    union = min(rows, kr + grp - 1 + (kr + grp - 1) % 2)
    starts = []
    for g in range(rows // grp):
        rs0 = min(max(g * grp - kr // 2, 0), rows - kr)
        starts.append(min(rs0, rows - union))
    return rows, kr, grp, union, starts


def _na_kernel(ws_ref, plane_ref, q_ref, k_ref, v_ref, ck_ref, cv_ref, tab_ref, o_ref, sctx, pctx, oacc, *,
               n_groups, grp, union, gw, scale):
    gq, uk = grp * gw, union * gw
    ck = ck_ref[0].astype(BF16)
    cv = cv_ref[0].astype(BF16)
    sctx[...] = _nt_dot(q_ref[...], ck) * scale
    first_half = lax.broadcasted_iota(jnp.int32, (gw, LANES), 1) < gw

    def body(g, carry):
        q0 = pl.multiple_of(g * gq, gq)
        k0 = pl.multiple_of(ws_ref[g] * gw, gw)
        q = q_ref[pl.ds(q0, gq), :]
        bias_rows = []
        for a in range(grp):
            base = (g * grp + a) * union
            tiles = [jnp.where(first_half, tab_ref[0, plane_ref[base + 2 * jj]],
                               tab_ref[0, plane_ref[base + 2 * jj + 1]]) for jj in range(union // 2)]
            bias_rows.append(jnp.concatenate(tiles, axis=1))
        bias = jnp.concatenate(bias_rows, axis=0)
        s = _nt_dot(q, k_ref[pl.ds(k0, uk), :]) * scale + bias
        sc = sctx[pl.ds(q0, gq), :]
        m = jnp.maximum(jnp.max(s, axis=-1, keepdims=True), jnp.max(sc, axis=-1, keepdims=True))
        p = jnp.exp2(s - m)
        pc = jnp.exp2(sc - m)
        inv = 1.0 / (jnp.sum(p, axis=-1, keepdims=True) + jnp.sum(pc, axis=-1, keepdims=True))
        oacc[pl.ds(q0, gq), :] = jnp.dot(p.astype(BF16), v_ref[pl.ds(k0, uk), :],
                                         preferred_element_type=F32) * inv
        pctx[pl.ds(q0, gq), :] = (pc * inv).astype(BF16)
        return carry

    lax.fori_loop(0, n_groups, body, 0, unroll=2)
    o_ref[...] = (oacc[...] + jnp.dot(pctx[...], cv, preferred_element_type=F32)).astype(o_ref.dtype)


def _na_bias_table(cfg, rpb):
    w = cfg.grid_w
    assert LANES == 2 * w
    rows, kr, grp, union, starts = _na_geometry(cfg)
    qc = np.arange(w)
    kc = np.arange(w)
    cs = np.clip(qc - cfg.na_win_c // 2, 0, w - cfg.na_win_c)
    col_mask = (kc[None, :] >= cs[:, None]) & (kc[None, :] < cs[:, None] + cfg.na_win_c)
    col_idx = np.clip(kc[None, :] - qc[:, None] + cfg.na_win_c - 1, 0, 2 * cfg.na_win_c - 2)
    n_c = 2 * cfg.na_win_c - 1
    onehot = (col_idx[None] == np.arange(n_c)[:, None, None]) & col_mask[None]
    t = jnp.einsum('hrc,cqk->hrqk', rpb.astype(F32), jnp.asarray(onehot, dtype=F32),
                   precision=lax.Precision.HIGHEST)
    t = jnp.where(jnp.asarray(col_mask)[None, None], t * math.log2(math.e), NEG_BIG)
    n_r = 2 * cfg.na_win_r - 1
    t = jnp.concatenate([t, jnp.full((cfg.b_heads, 1, w, w), NEG_BIG, F32)], axis=1)
    plane = np.full((len(starts), grp, union), n_r, np.int32)
    for g, ws in enumerate(starts):
        for a in range(grp):
            r = g * grp + a
            rs = min(max(r - kr // 2, 0), rows - kr)
            for j in range(union):
                if rs <= ws + j < rs + kr:
                    plane[g, a, j] = ws + j - r + cfg.na_win_r - 1
    return jnp.concatenate([t, t], axis=-1), jnp.asarray(plane.reshape(-1))


def _na_attention(cfg, proj, ck, cv, rpb):
    hd = cfg.b_dim
    nh = cfg.b_heads
    n = cfg.dec_seq
    rows, kr, grp, union, starts = _na_geometry(cfg)
    n_groups = len(starts)
    base = 3 * cfg.a_width // hd
    past = ck.shape[1]
    table, plane = _na_bias_table(cfg, rpb)
    kern = functools.partial(_na_kernel, n_groups=n_groups, grp=grp, union=union, gw=cfg.grid_w,
                             scale=hd ** -0.5 * math.log2(math.e))
    cspec = pl.BlockSpec((1, past, hd), lambda h, b, *_: (b, 0, h))
    return pl.pallas_call(
        kern,
        grid_spec=pltpu.PrefetchScalarGridSpec(
            num_scalar_prefetch=2,
            grid=(nh, cfg.dec_batch),
            in_specs=[
                pl.BlockSpec((n, hd), lambda h, b, *_: (b, base + h)),
                pl.BlockSpec((n, hd), lambda h, b, *_: (b, base + nh + h)),
                pl.BlockSpec((n, hd), lambda h, b, *_: (b, base + 2 * nh + h)),
                cspec, cspec,
                pl.BlockSpec((1,) + table.shape[1:], lambda h, b, *_: (h, 0, 0, 0)),
            ],
            out_specs=pl.BlockSpec((n, hd), lambda h, b, *_: (b, h)),
            scratch_shapes=[pltpu.VMEM((n, past), F32), pltpu.VMEM((n, past), BF16), pltpu.VMEM((n, hd), F32)],
        ),
        out_shape=jax.ShapeDtypeStruct((cfg.n_s, cfg.b_width), BF16),
        compiler_params=_params(("arbitrary", "arbitrary")),
        name="neighbourhood_attention",
    )(jnp.asarray(np.asarray(starts, np.int32)), plane, proj, proj, proj, ck, cv, table)


def _linres_kernel(*refs, n_parts, n_pb, x_split):
    i = pl.program_id(0)
    is_p = i < n_pb
    pos = 0
    acc = None
    w_ref = refs[2 * n_parts]
    k0 = 0
    for p in range(n_parts):
        a_p, a_s = refs[2 * p], refs[2 * p + 1]
        a = jnp.where(is_p, a_p[...], a_s[...])
        kk = a.shape[1]
        part = jnp.dot(a, w_ref[k0:k0 + kk, :], preferred_element_type=F32)
        acc = part if acc is None else acc + part
        k0 += kk
    pos = 2 * n_parts + 1
    if x_split:
        x = jnp.where(is_p, refs[pos][...], refs[pos + 1][...])
        pos += 2
    else:
        x = refs[pos][...]
        pos += 1
    mod_ref, o_ref = refs[pos], refs[pos + 1]
    o_ref[...] = x + mod_ref[0, 2:3, :] * acc


def _linear_residual(cfg, parts, w, x, mod_l):
    d = cfg.d_model
    tm = min(256, cfg.tm)
    n_pb = cfg.n_p // tm
    n_sb = cfg.n_s // tm
    p_idx = lambda i: (jnp.minimum(i, n_pb - 1), 0)
    s_idx = lambda i: (jnp.maximum(i - n_pb, 0), 0)
    in_specs, args = [], []
    for a_p, a_s in parts:
        kk = a_p.shape[1]
        in_specs += [pl.BlockSpec((tm, kk), p_idx), pl.BlockSpec((tm, kk), s_idx)]
        args += [a_p, a_s]
    in_specs.append(pl.BlockSpec(w.shape, lambda i: (0, 0)))
    args.append(w)
    x_split = isinstance(x, tuple)
    if x_split:
        in_specs += [pl.BlockSpec((tm, d), p_idx), pl.BlockSpec((tm, d), s_idx)]
        args += list(x)
    else:
        in_specs.append(pl.BlockSpec((tm, d), lambda i: (i, 0)))
        args.append(x)
    mi = _merged_mod_index(cfg, tm)
    in_specs.append(pl.BlockSpec((1, 8, d), lambda i: (mi(i), 0, 0)))
    args.append(mod_l)
    kern = functools.partial(_linres_kernel, n_parts=len(parts), n_pb=n_pb, x_split=x_split)
    return pl.pallas_call(
        kern,
        grid=(n_pb + n_sb,),
        in_specs=in_specs,
        out_specs=pl.BlockSpec((tm, d), lambda i: (i, 0)),
        out_shape=jax.ShapeDtypeStruct((cfg.n_tok, d), F32),
        compiler_params=_params(("arbitrary",)),
        name="linear_gated_residual",
    )(*args)


def _dft_chan_kernel(x_ref, mod_ref, g_ref, cs_ref, y_ref, *, groups, eps):
    h = _norm_mod(x_ref[...], g_ref[...], mod_ref[0, 0:1, :], mod_ref[0, 1:2, :], eps).astype(BF16)
    gd = h.shape[1] // groups
    for g in range(groups):
        r = jnp.dot(h[:, g * gd:(g + 1) * gd], cs_ref[...], preferred_element_type=F32)
        y_ref[0, :, g * gd:(g + 1) * gd] = r[:, :gd].astype(BF16)
        y_ref[1, :, g * gd:(g + 1) * gd] = r[:, gd:].astype(BF16)


def _dft_mats(n):
    k = np.arange(n)
    ang = 2.0 * np.pi * ((k[:, None] * k[None, :]) % n) / n
    return np.cos(ang), np.sin(ang)


def _dft_channels(cfg, x, mod_l, gain):
    d = cfg.d_model
    gd = d // cfg.f_groups
    tm = cfg.tm
    c, s = _dft_mats(gd)
    cs = jnp.asarray(np.concatenate([c, s], axis=1), dtype=F32).astype(BF16)
    mi = _merged_mod_index(cfg, tm)
    return pl.pallas_call(
        functools.partial(_dft_chan_kernel, groups=cfg.f_groups, eps=cfg.norm_eps),
        grid=(cfg.n_tok // tm,),
        in_specs=[
            pl.BlockSpec((tm, d), lambda i: (i, 0)),
            pl.BlockSpec((1, 8, d), lambda i: (mi(i), 0, 0)),
            pl.BlockSpec((1, d), lambda i: (0, 0)),
            pl.BlockSpec((gd, 2 * gd), lambda i: (0, 0)),
        ],
        out_specs=pl.BlockSpec((2, tm, d), lambda i: (0, i, 0)),
        out_shape=jax.ShapeDtypeStruct((2, cfg.n_tok, d), BF16),
        compiler_params=_params(("arbitrary",)),
        name="dft_channels",
    )(x, mod_l, gain, cs)


def _dft_seq_kernel(w_ref, y_ref, o_ref, *, scale):
    acc = (jnp.dot(w_ref[0], y_ref[0], preferred_element_type=F32)
           + jnp.dot(w_ref[1], y_ref[1], preferred_element_type=F32))
    o_ref[...] = (acc * scale).astype(o_ref.dtype)


def _dft_sequence(cfg, y, n_batch, seq_len, first_block):
    d = cfg.d_model
    c, s = _dft_mats(seq_len)
    wm = jnp.asarray(np.stack([c, -s]), dtype=F32).astype(BF16)
    tml = min(1024, seq_len)
    tn = min(512, d)
    nm = seq_len // tml
    scale = 1.0 / math.sqrt(seq_len * (d // cfg.f_groups))
    return pl.pallas_call(
        functools.partial(_dft_seq_kernel, scale=scale),
        grid=(n_batch, nm, d // tn),
        in_specs=[
            pl.BlockSpec((2, tml, seq_len), lambda b, mi, j: (0, mi, 0)),
            pl.BlockSpec((2, seq_len, tn), lambda b, mi, j: (0, first_block + b, j)),
        ],
        out_specs=pl.BlockSpec((tml, tn), lambda b, mi, j: (b * nm + mi, j)),
        out_shape=jax.ShapeDtypeStruct((n_batch * seq_len, d), BF16),
        compiler_params=_params(("arbitrary", "arbitrary", "arbitrary")),
        name="dft_sequence_%d" % seq_len,
    )(wm, y)


TOKEN_TILE_ROWS = 8


def _token_tile(d):
    rows = min(TOKEN_TILE_ROWS, d // LANES)
    return rows, d // (rows * LANES)


def _tile_chunk(c, n_tok, rows, first=0):
    return c // rows, pl.ds(first + c % rows, n_tok, stride=rows), slice(None)


def _tile_tokens(ref, first_row, n_rows):
    return ref.at[:, pl.ds(first_row, n_rows), :]


def _route_kernel(x_ref, mod_ref, g_ref, wr_ref, rb_ref, hp_ref, idx_ref, wt_ref, *, eps, n_exp, per_grp):
    h = _norm_mod(x_ref[...], g_ref[...], mod_ref[0, 3:4, :], mod_ref[0, 4:5, :], eps)
    tm, d = h.shape
    rt, _ = _token_tile(d)
    for c in range(d // LANES):
        hp_ref[_tile_chunk(c, tm, rt)] = h[:, c * LANES:(c + 1) * LANES]

    h_hi = h.astype(BF16)
    h_lo = (h - h_hi.astype(F32)).astype(BF16)
    w = wr_ref[...]
    w_hi = w.astype(BF16)
    w_lo = (w - w_hi.astype(F32)).astype(BF16)
    logits = _nt_dot(w_hi, h_hi) + (_nt_dot(w_hi, h_lo) + _nt_dot(w_lo, h_hi))
    scores = jax.nn.sigmoid(logits)
    sel = scores + rb_ref[...]
    n_grp = n_exp // per_grp
    best = None
    gi = None
    for g in range(n_grp):
        v = [sel[g * per_grp + k:g * per_grp + k + 1, :] for k in range(per_grp)]
        gs = None
        for a in range(per_grp):
            for b in range(a + 1, per_grp):
                ps = v[a] + v[b]
                gs = ps if gs is None else jnp.maximum(gs, ps)
        if best is None:
            best, gi = gs, jnp.zeros(gs.shape, jnp.int32)
        else:
            better = gs > best
            gi = jnp.where(better, g, gi)
            best = jnp.where(better, gs, best)
    row = lax.broadcasted_iota(jnp.int32, sel.shape, 0)
    masked = jnp.where(row // per_grp == gi, sel, -jnp.inf)
    m1 = jnp.max(masked, axis=0, keepdims=True)
    i1 = jnp.min(jnp.where(masked == m1, row, n_exp), axis=0, keepdims=True)
    masked2 = jnp.where(row == i1, -jnp.inf, masked)
    m2 = jnp.max(masked2, axis=0, keepdims=True)
    i2 = jnp.min(jnp.where(masked2 == m2, row, n_exp), axis=0, keepdims=True)
    w1 = jnp.sum(jnp.where(row == i1, scores, 0.0), axis=0, keepdims=True)
    w2 = jnp.sum(jnp.where(row == i2, scores, 0.0), axis=0, keepdims=True)
    inv = 1.0 / (w1 + w2)
    idx_ref[...] = jnp.concatenate([i1, i2], axis=0)
    wt_ref[...] = jnp.concatenate([w1 * inv, w2 * inv], axis=0)


def _route(cfg, x, mod_l, gain, w_router_t, router_bias):
    d = cfg.d_model
    tm = cfg.tm
    t = cfg.n_tok
    rt, npl = _token_tile(d)
    mi = _merged_mod_index(cfg, tm)
    kern = functools.partial(_route_kernel, eps=cfg.norm_eps, n_exp=cfg.n_experts,
                             per_grp=cfg.n_experts // cfg.n_groups)
    return pl.pallas_call(
        kern,
        grid=(t // tm,),
        in_specs=[
            pl.BlockSpec((tm, d), lambda i: (i, 0)),
            pl.BlockSpec((1, 8, d), lambda i: (mi(i), 0, 0)),
            pl.BlockSpec((1, d), lambda i: (0, 0)),
            pl.BlockSpec((cfg.n_experts, d), lambda i: (0, 0)),
            pl.BlockSpec((cfg.n_experts, 1), lambda i: (0, 0)),
        ],
        out_specs=[
            pl.BlockSpec((npl, tm * rt, LANES), lambda i: (0, i, 0)),
            pl.BlockSpec((2, tm), lambda i: (0, i)),
            pl.BlockSpec((2, tm), lambda i: (0, i)),
        ],
        out_shape=[
            jax.ShapeDtypeStruct((npl, t * rt, LANES), F32),
            jax.ShapeDtypeStruct((2, t), jnp.int32),
            jax.ShapeDtypeStruct((2, t), F32),
        ],
        compiler_params=_params(("arbitrary",)),
        name="moe_route",
    )(x, mod_l, gain, w_router_t, router_bias.reshape(cfg.n_experts, 1))


def _plan(cfg, idx):
    t = cfg.n_tok
    eb = cfg.e_block
    ne = cfg.n_experts
    n_blocks = -(-(2 * t + ne * (eb - 1)) // eb)
    e_flat = idx.reshape(-1)
    onehot = (e_flat[:, None] == jnp.arange(ne, dtype=jnp.int32)[None, :]).astype(jnp.int32)
    csum = jnp.cumsum(onehot, axis=0)
    rank = jnp.sum(onehot * (csum - 1), axis=1)
    counts = csum[-1]
    padded = (counts + eb - 1) // eb * eb
    pad_end = jnp.cumsum(padded)
    pad_start = pad_end - padded
    pos = (pad_start[e_flat] + rank).astype(jnp.int32).reshape(2, t)
    n_used = (pad_end[-1] // eb).astype(jnp.int32)
    blk = jnp.arange(n_blocks, dtype=jnp.int32)
    blk = jnp.minimum(blk, n_used - 1)
    first_e = jnp.sum((pad_end[None, :] <= (blk * eb)[:, None]).astype(jnp.int32), axis=1)
    block_e = jnp.minimum(first_e, ne - 1).astype(jnp.int32)
    return pos, block_e, n_used.reshape(1), pad_end.astype(jnp.int32), n_blocks


def _weight_schedule(cfg, block_e, n_used, n_blocks):
    n_chunks = _expert_chunks(cfg)[2]
    big = n_blocks + 1
    idx = jnp.arange(n_blocks, dtype=jnp.int32)
    valid = idx < n_used[0]
    prev_e = jnp.concatenate([jnp.full((1,), -1, jnp.int32), block_e[:-1]])
    change = (block_e != prev_e) & valid
    earlier = (idx[None, :] <= idx[:, None]) & change[None, :]
    later = (idx[None, :] > idx[:, None]) & change[None, :]
    run_start = jnp.max(jnp.where(earlier, idx[None, :], 0), axis=1)
    next_start = jnp.minimum(jnp.min(jnp.where(later, idx[None, :], big), axis=1), n_used[0])
    slot = (jnp.sum(earlier.astype(jnp.int32), axis=1) - 1) % 2
    has_next = valid & (next_start < n_used[0])
    next_e = jnp.where(has_next, block_e[jnp.minimum(next_start, n_blocks - 1)], 0)
    run_len = jnp.maximum(next_start - run_start, 1)
    per_block = (n_chunks + run_len - 1) // run_len
    i = idx - run_start
    c0 = jnp.where(has_next, jnp.minimum(i * per_block, n_chunks), 0)
    c1 = jnp.where(has_next, jnp.minimum((i + 1) * per_block, n_chunks), 0)
    return tuple(a.astype(jnp.int32) for a in (slot, next_e, c0, c1))


def _dispatch_kernel(pend_ref, pos_ref, hp_ref, xr_ref, zbuf, sem, zsem, *, n_exp, eb, spt):
    i = pl.program_id(0)

    @pl.when(i == 0)
    def _():
        zbuf[...] = jnp.zeros(zbuf.shape, zbuf.dtype)
        for e in range(n_exp):
            start = pl.multiple_of(jnp.maximum(pend_ref[e] - eb, 0) * spt, eb * spt)
            cp = pltpu.make_async_copy(zbuf, _tile_tokens(xr_ref, start, eb * spt), zsem)
            cp.start()
            cp.wait()
        n_used = pend_ref[n_exp - 1] // eb
        n_blocks = xr_ref.shape[1] // (eb * spt)
        for e in range(n_exp):
            @pl.when(n_used + e < n_blocks)
            def _():
                start = pl.multiple_of((n_used + e) * (eb * spt), eb * spt)
                cp = pltpu.make_async_copy(zbuf, _tile_tokens(xr_ref, start, eb * spt), zsem)
                cp.start()
                cp.wait()

    rows = pos_ref.shape[2]

    def start(r, c):
        src = _tile_tokens(hp_ref, pl.multiple_of(r * spt, spt), spt)
        for k in range(2):
            dst = _tile_tokens(xr_ref, pl.multiple_of(pos_ref[0, k, r] * spt, spt), spt)
            pltpu.make_async_copy(src, dst, sem.at[k]).start()
        return c

    lax.fori_loop(0, rows, start, 0, unroll=8)
    for k in range(2):
        pltpu.make_async_copy(hp_ref, _tile_tokens(xr_ref, 0, rows * spt), sem.at[k]).wait()


def _dispatch(cfg, hp, pos_blocks, pad_end, n_rows):
    tb = pos_blocks.shape[2]
    spt, npl = _token_tile(cfg.d_model)
    kern = functools.partial(_dispatch_kernel, n_exp=cfg.n_experts, eb=cfg.e_block, spt=spt)
    return pl.pallas_call(
        kern,
        grid_spec=pltpu.PrefetchScalarGridSpec(
            num_scalar_prefetch=1,
            grid=(cfg.n_tok // tb,),
            in_specs=[
                pl.BlockSpec((1, 2, tb), lambda i, pe: (i, 0, 0), memory_space=pltpu.SMEM),
                pl.BlockSpec((npl, tb * spt, LANES), lambda i, pe: (0, i, 0)),
            ],
            out_specs=pl.BlockSpec(memory_space=pl.ANY),
            scratch_shapes=[
                pltpu.VMEM((npl, cfg.e_block * spt, LANES), F32),
                pltpu.SemaphoreType.DMA((2,)),
                pltpu.SemaphoreType.DMA(()),
            ],
        ),
        out_shape=jax.ShapeDtypeStruct((npl, n_rows * spt, LANES), F32),
        compiler_params=_params(("arbitrary",)),
        name="moe_dispatch",
    )(pad_end, pos_blocks, hp)


EXPERT_CHUNK_ROWS = 128
EXPERT_RING = 6
EXPERT_VMEM_LIMIT = 60 * 1024 * 1024


def _expert_chunks(cfg):
    n_in = cfg.d_model // EXPERT_CHUNK_ROWS
    n_out = cfg.d_expert // EXPERT_CHUNK_ROWS
    return n_in, n_out, 2 * n_in + n_out


def _expert_kernel(nu_ref, e0_ref, slot_ref, ne_ref, c0_ref, c1_ref, xp_ref, wg_ref, wu_ref, wd_ref, y_ref,
                   wg_b, wu_b, wd_b, stg, sem, *, n_in, n_out, layer):
    b = pl.program_id(0)
    n_chunks = 2 * n_in + n_out
    ring = stg.shape[0]
    rows = EXPERT_CHUNK_ROWS

    parts = [(0, n_in, wg_ref, wg_b), (n_in, n_in, wu_ref, wu_b), (2 * n_in, n_out, wd_ref, wd_b)]

    def for_chunk(c, e, fn):
        s = c % ring
        for first, count, src, dst in parts:
            @pl.when((c >= first) & (c < first + count))
            def _():
                row = pl.multiple_of((c - first) * rows, rows)
                window = stg.at[s, :, pl.ds(0, dst.shape[2])]
                cp = pltpu.make_async_copy(src.at[layer, e, pl.ds(row, rows)], window, sem.at[s])
                fn(cp, dst, row, window)

    def start_chunk(c, e):
        for_chunk(c, e, lambda cp, dst, row, window: cp.start())

    def finish_chunk(c, e, dst_slot):
        def cast(cp, dst, row, window):
            cp.wait()
            dst[dst_slot, pl.ds(row, rows), :] = window[...].astype(BF16)
        for_chunk(c, e, cast)

    def load_chunks(e, dst_slot, lo, hi):
        @pl.when((lo == 0) & (hi > 0))
        def _():
            for k in range(min(ring, n_chunks)):
                start_chunk(k, e)

        def body(c, carry):
            finish_chunk(c, e, dst_slot)

            @pl.when(c + ring < n_chunks)
            def _():
                start_chunk(c + ring, e)
            return carry

        lax.fori_loop(lo, hi, body, 0)

    @pl.when(b == 0)
    def _():
        load_chunks(e0_ref[0], 0, 0, n_chunks)

    @pl.when(b < nu_ref[0])
    def _():
        slot = slot_ref[b]
        load_chunks(ne_ref[b], 1 - slot, c0_ref[b], c1_ref[b])
        d = wg_b.shape[1]
        n_chunk = d // LANES
        rt, _ = _token_tile(d)
        eb = xp_ref.shape[1] // rt
        x = jnp.concatenate([xp_ref[_tile_chunk(c, eb, rt)].astype(BF16) for c in range(n_chunk)], axis=1)
        g = jnp.dot(x, wg_b[slot], preferred_element_type=F32)
        u = jnp.dot(x, wu_b[slot], preferred_element_type=F32)
        a = (_silu(g) * u).astype(BF16)
        y = jnp.dot(a, wd_b[slot], preferred_element_type=F32)
        for c in range(n_chunk):
            y_ref[_tile_chunk(c, eb, rt)] = y[:, c * LANES:(c + 1) * LANES]

    @pl.when(b >= nu_ref[0])
    def _():
        y_ref[...] = jnp.zeros(y_ref.shape, y_ref.dtype)


def _experts(cfg, x_rows, block_e, n_used, wg, wu, wd, layer, n_blocks):
    d = cfg.d_model
    f = cfg.d_expert
    eb = cfg.e_block
    rt, npl = _token_tile(d)
    n_in, n_out, _ = _expert_chunks(cfg)
    slot, next_e, c0, c1 = _weight_schedule(cfg, block_e, n_used, n_blocks)
    any_spec = pl.BlockSpec(memory_space=pl.ANY)
    x_idx = lambda b, nu, *_: (0, jnp.minimum(b, nu[0] - 1), 0)
    return pl.pallas_call(
        functools.partial(_expert_kernel, n_in=n_in, n_out=n_out, layer=layer),
        grid_spec=pltpu.PrefetchScalarGridSpec(
            num_scalar_prefetch=6,
            grid=(n_blocks,),
            in_specs=[pl.BlockSpec((npl, eb * rt, LANES), x_idx), any_spec, any_spec, any_spec],
            out_specs=pl.BlockSpec((npl, eb * rt, LANES), lambda b, *_: (0, b, 0)),
            scratch_shapes=[
                pltpu.VMEM((2, d, f), BF16), pltpu.VMEM((2, d, f), BF16), pltpu.VMEM((2, f, d), BF16),
                pltpu.VMEM((EXPERT_RING, EXPERT_CHUNK_ROWS, max(d, f)), F32),
                pltpu.SemaphoreType.DMA((EXPERT_RING,)),
            ],
        ),
        out_shape=jax.ShapeDtypeStruct((npl, n_blocks * eb * rt, LANES), F32),
        compiler_params=pltpu.CompilerParams(dimension_semantics=("arbitrary",),
                                             vmem_limit_bytes=EXPERT_VMEM_LIMIT),
        name="moe_experts",
    )(n_used, block_e[:1], slot, next_e, c0, c1, x_rows, wg, wu, wd)


def _combine_kernel(pos_ref, nxt_ref, y_ref, x_ref, mod_ref, wt_ref, fg_ref, o_ref, ybuf, sem, *, final, eps):
    i = pl.program_id(0)
    n_steps = pl.num_programs(0)
    rows, d = x_ref.shape
    n_chunk = d // LANES
    rt, _ = _token_tile(d)
    slot = i % 2
    per_choice = rows * rt

    def gather(p_ref, s):
        def start(r, c):
            for k in range(2):
                src = _tile_tokens(y_ref, pl.multiple_of(p_ref[0, k, r] * rt, rt), rt)
                dst = _tile_tokens(ybuf.at[s], pl.multiple_of(k * per_choice + r * rt, rt), rt)
                pltpu.make_async_copy(src, dst, sem.at[s]).start()
            return c
        lax.fori_loop(0, rows, start, 0, unroll=8)

    @pl.when(i == 0)
    def _():
        gather(pos_ref, 0)

    @pl.when(i + 1 < n_steps)
    def _():
        gather(nxt_ref, 1 - slot)

    pltpu.make_async_copy(_tile_tokens(y_ref, 0, 2 * per_choice), ybuf.at[slot], sem.at[slot]).wait()

    w = wt_ref[...]
    w0, w1 = w[:, 0:1], w[:, 1:2]
    sumsq = jnp.zeros((rows, 1), F32)
    for c in range(n_chunk):
        cols = slice(c * LANES, (c + 1) * LANES)
        moe = (w0 * ybuf[(slot,) + _tile_chunk(c, rows, rt)]
               + w1 * ybuf[(slot,) + _tile_chunk(c, rows, rt, first=per_choice)])
        xc = x_ref[:, cols] + mod_ref[0, 5:6, cols] * moe
        o_ref[:, cols] = xc
        sumsq = sumsq + jnp.sum(xc * xc, axis=-1, keepdims=True)
    if final:
        o_ref[...] = (o_ref[...] * lax.rsqrt(sumsq * (1.0 / d) + eps)) * fg_ref[...]


def _combine(cfg, y, pos_blocks, wts_t, x, mod_l, final_gain, *, row0, n_rows, final):
    d = cfg.d_model
    tb = pos_blocks.shape[2]
    rt, npl = _token_tile(d)
    b0 = row0 // tb
    mi = _merged_mod_index(cfg, tb)
    kern = functools.partial(_combine_kernel, final=final, eps=cfg.norm_eps)
    n_steps = n_rows // tb
    return pl.pallas_call(
        kern,
        grid=(n_steps,),
        in_specs=[
            pl.BlockSpec((1, 2, tb), lambda i: (b0 + i, 0, 0), memory_space=pltpu.SMEM),
            pl.BlockSpec((1, 2, tb), lambda i: (b0 + jnp.minimum(i + 1, n_steps - 1), 0, 0),
                         memory_space=pltpu.SMEM),
            pl.BlockSpec(memory_space=pl.ANY),
            pl.BlockSpec((tb, d), lambda i: (b0 + i, 0)),
            pl.BlockSpec((1, 8, d), lambda i: (mi(b0 + i), 0, 0)),
            pl.BlockSpec((tb, 2), lambda i: (b0 + i, 0)),
            pl.BlockSpec((1, d), lambda i: (0, 0)),
        ],
        out_specs=pl.BlockSpec((tb, d), lambda i: (i, 0)),
        out_shape=jax.ShapeDtypeStruct((n_rows, d), F32),
        scratch_shapes=[pltpu.VMEM((2, npl, 2 * tb * rt, LANES), F32), pltpu.SemaphoreType.DMA((2,))],
        compiler_params=_params(("arbitrary",)),
        name="moe_combine_final" if final else "moe_combine",
    )(pos_blocks, pos_blocks, y, x, mod_l, wts_t, final_gain)


def _moe(cfg, x, mod_l, gain, w_router_t, router_bias, wg, wu, wd, layer, final_gain, final):
    tb = min(256, cfg.tm)
    hp, idx, wts = _route(cfg, x, mod_l, gain, w_router_t, router_bias)
    pos, block_e, n_used, pad_end, n_blocks = _plan(cfg, idx)
    pos_blocks = pos.reshape(2, cfg.n_tok // tb, tb).transpose(1, 0, 2)
    td = cfg.tm
    x_rows = _dispatch(cfg, hp, pos.reshape(2, cfg.n_tok // td, td).transpose(1, 0, 2), pad_end,
                       n_blocks * cfg.e_block)
    y = _experts(cfg, x_rows, block_e, n_used, wg, wu, wd, layer, n_blocks)
    wts_t = wts.T
    comb = functools.partial(_combine, cfg, y, pos_blocks, wts_t, x, mod_l, final_gain, final=final)
    if final:
        return comb(row0=0, n_rows=cfg.n_p), comb(row0=cfg.n_p, n_rows=cfg.n_s)
    return comb(row0=0, n_rows=cfg.n_tok)


def _forward(cfg, x_prompt, x_sample, cache_a_k, cache_a_v, cache_b_k, cache_b_v, c, c_ctx, w_ada, b_ada,
             norm1, norm2, final_norm, w_in, w_mix_out, lambda_q1, lambda_k1, lambda_q2, lambda_k2,
             subln_gain, na_rel_bias, w_fourier_out, w_router, router_bias, w_exp_gate, w_exp_up,
             w_exp_down):
    d = cfg.d_model
    xp = x_prompt.reshape(cfg.n_p, d)
    xs = x_sample.reshape(cfg.n_s, d)

    n_cond = -(-cfg.n_mod // 8) * 8
    cond = jnp.concatenate([c_ctx[None, :], c, jnp.zeros((n_cond - cfg.n_mod, d), F32)], axis=0)
    mod = _modulation(cfg, cond, w_ada, b_ada)
    mod = mod.reshape(cfg.depth, n_cond, 6, d)[:, :cfg.n_mod]
    mod = jnp.pad(mod, ((0, 0), (0, 0), (0, 2), (0, 0)))

    w_router_t = w_router.T
    fgain = final_norm.reshape(1, d)
    x = (xp, xs)
    kv_cache = None
    for l in range(cfg.depth):
        j = l // 2
        g1 = norm1[l].reshape(1, d)
        if l % 2 == 0:
            lam_init = 0.8 - 0.6 * math.exp(-0.3 * l)
            lam_pack = jnp.zeros((8, LANES), F32).at[:4, :cfg.a_qk].set(
                jnp.stack([lambda_q1[j], lambda_k1[j], lambda_q2[j], lambda_k2[j]]))
            sgain = subln_gain[j].reshape(1, cfg.a_dim)
            w_in_b = w_in[j].astype(BF16)
            if isinstance(x, tuple):
                x_p, x_s = x
            else:
                x_p, x_s = x[:cfg.n_p], x[cfg.n_p:]
            proj_p = _projection(cfg, x_p, mod[l], g1, w_in_b, latent=False, out_dtype=F32)
            proj_s = _projection(cfg, x_s, mod[l], g1, w_in_b, latent=True, out_dtype=BF16)
            if kv_cache is None:
                kv_cache = []
            aw, bw = cfg.a_width, cfg.b_width
            kv_cache.append((proj_p[:, aw:2 * aw], proj_p[:, 2 * aw:3 * aw],
                             proj_p[:, 3 * aw + bw:3 * aw + 2 * bw], proj_p[:, 3 * aw + 2 * bw:]))
            cak = cache_a_k[:, j].reshape(cfg.dec_batch, cfg.past_len, aw)
            cav = cache_a_v[:, j].reshape(cfg.dec_batch, cfg.past_len, aw)
            cbk = cache_b_k[:, j].reshape(cfg.dec_batch, cfg.past_len, bw)
            cbv = cache_b_v[:, j].reshape(cfg.dec_batch, cfg.past_len, bw)
            a_p = _diff_attention(cfg, proj_p, lam_pack, sgain, cfg.batch, cfg.seq, lam_init)
            b_p = _soft_attention(cfg, proj_p, cfg.batch, cfg.seq)
            a_s = _diff_attention(cfg, proj_s, lam_pack, sgain, cfg.dec_batch, cfg.dec_seq, lam_init,
                                  ctx=(cak, cav))
            b_s = _na_attention(cfg, proj_s, cbk, cbv, na_rel_bias[j])
            x = _linear_residual(cfg, [(a_p, a_s), (b_p, b_s)], w_mix_out[j].astype(BF16), x, mod[l])
        else:
            if isinstance(x, tuple):
                x = jnp.concatenate(x, axis=0)
            y = _dft_channels(cfg, x, mod[l], g1)
            f_p = _dft_sequence(cfg, y, cfg.batch, cfg.seq, 0)
            f_s = _dft_sequence(cfg, y, cfg.dec_batch, cfg.dec_seq, cfg.n_p // cfg.dec_seq)
            x = _linear_residual(cfg, [(f_p, f_s)], w_fourier_out[j].astype(BF16), x, mod[l])
        x = _moe(cfg, x, mod[l], norm2[l].reshape(1, d), w_router_t, router_bias,
                 w_exp_gate, w_exp_up, w_exp_down, l, fgain, final=(l == cfg.depth - 1))
    y_p, y_s = x
    n_even = (cfg.depth + 1) // 2
    outs = [y_p.reshape(cfg.batch, cfg.seq, d), y_s.reshape(cfg.dec_batch, cfg.dec_seq, d)]
    for t in range(4):
        heads, hd = (cfg.a_heads, cfg.a_dim) if t < 2 else (cfg.b_heads, cfg.b_dim)
        stacked = jnp.stack([kv_cache[jj][t].reshape(cfg.batch, cfg.seq, heads, hd)
                             for jj in range(n_even)], axis=1)
        outs.append(stacked)
    return tuple(outs)


def kernel(x_prompt, x_sample, cache_a_k, cache_a_v, cache_b_k, cache_b_v, c, c_ctx, w_ada, b_ada, norm1, norm2, final_norm, w_in, w_mix_out, lambda_q1, lambda_k1, lambda_q2, lambda_k2, subln_gain, na_rel_bias, w_fourier_out, w_router, router_bias, w_exp_gate, w_exp_up, w_exp_down):
    return _forward(Cfg(), x_prompt, x_sample, cache_a_k, cache_a_v, cache_b_k, cache_b_v, c, c_ctx, w_ada,
                    b_ada, norm1, norm2, final_norm, w_in, w_mix_out, lambda_q1, lambda_k1, lambda_q2,
                    lambda_k2, subln_gain, na_rel_bias, w_fourier_out, w_router, router_bias, w_exp_gate,
                    w_exp_up, w_exp_down)
```

```python
import functools
import math
from typing import NamedTuple

import numpy as np
import jax
import jax.numpy as jnp
from jax import lax
from jax.experimental import pallas as pl
from jax.experimental.pallas import tpu as pltpu

F32 = jnp.float32
BF16 = jnp.bfloat16

LANES = 128
NEG_BIG = -1e30
VMEM_LIMIT = 56 * 1024 * 1024


class Cfg(NamedTuple):
    d_model: int = 2048
    batch: int = 16
    seq: int = 256
    depth: int = 2
    dec_batch: int = 8
    dec_seq: int = 2048
    past_len: int = 256
    grid_w: int = 64
    a_heads: int = 8
    a_qk: int = 64
    b_heads: int = 8
    b_dim: int = 128
    na_win_r: int = 8
    na_win_c: int = 16
    f_groups: int = 4
    n_experts: int = 16
    n_groups: int = 4
    d_expert: int = 1408
    rope_theta: float = 10000.0
    norm_eps: float = 1e-6
    subln_eps: float = 1e-5
    row_tile: int = 512
    proj_tile: int = 1024
    q_tile: int = 512
    e_block: int = 256

    @property
    def a_dim(self):
        return 2 * self.a_qk

    @property
    def a_width(self):
        return self.a_heads * self.a_dim

    @property
    def b_width(self):
        return self.b_heads * self.b_dim

    @property
    def n_p(self):
        return self.batch * self.seq

    @property
    def n_s(self):
        return self.dec_batch * self.dec_seq

    @property
    def n_tok(self):
        return self.n_p + self.n_s

    @property
    def tm(self):
        return min(self.row_tile, self.n_p, self.dec_seq)

    @property
    def n_mod(self):
        return 1 + self.dec_batch


def _params(sem):
    return pltpu.CompilerParams(dimension_semantics=sem, vmem_limit_bytes=VMEM_LIMIT)


def _norm_mod(x, g, sh, sc, eps):
    ms = jnp.mean(x * x, axis=-1, keepdims=True)
    y = x * lax.rsqrt(ms + eps)
    return (y * g) * (1.0 + sc) + sh


def _silu(x):
    return x * jax.nn.sigmoid(x)


def _merged_mod_index(cfg, tm):
    def f(i):
        r = i * tm
        return jnp.where(r < cfg.n_p, 0, 1 + (r - cfg.n_p) // cfg.dec_seq)
    return f


def _ada_kernel(cond_ref, w_ref, b_ref, o_ref):
    s = _silu(cond_ref[...])
    s_hi = s.astype(BF16)
    s_lo = (s - s_hi.astype(F32)).astype(BF16)
    lhs = jnp.concatenate([s_hi, s_lo], axis=0)
    r = jnp.dot(lhs, w_ref[0].astype(BF16), preferred_element_type=F32)
    n = s.shape[0]
    o_ref[0] = r[:n] + r[n:] + b_ref[0]


def _modulation(cfg, cond, w_ada, b_ada):
    d = cfg.d_model
    r = cond.shape[0]
    tn = math.gcd(1024, 6 * d)
    return pl.pallas_call(
        _ada_kernel,
        grid=(cfg.depth, 6 * d // tn),
        in_specs=[
            pl.BlockSpec((r, d), lambda l, j: (0, 0)),
            pl.BlockSpec((1, d, tn), lambda l, j: (l, 0, j)),
            pl.BlockSpec((1, 1, tn), lambda l, j: (l, 0, j)),
        ],
        out_specs=pl.BlockSpec((1, r, tn), lambda l, j: (l, 0, j)),
        out_shape=jax.ShapeDtypeStruct((cfg.depth, r, 6 * d), F32),
        compiler_params=_params(("arbitrary", "arbitrary")),
        name="ada_modulation",
    )(cond, w_ada, b_ada.reshape(cfg.depth, 1, 6 * d))


PROJ_CHUNK = 256


def _proj_kernel(*refs, rope, n_rope_blocks, eps):
    if rope:
        x_ref, mod_ref, g_ref, w_ref, cos_ref, sa_ref, sb_ref, o_ref, h_ref = refs
    else:
        x_ref, mod_ref, g_ref, w_ref, o_ref, h_ref = refs
    j = pl.program_id(1)

    @pl.when(j == 0)
    def _():
        h = _norm_mod(x_ref[...], g_ref[...], mod_ref[0, 0:1, :], mod_ref[0, 1:2, :], eps)
        h_ref[...] = h.astype(BF16)

    if not rope:
        o_ref[...] = jnp.dot(h_ref[...], w_ref[...], preferred_element_type=F32).astype(o_ref.dtype)
        return

    @pl.when(j < n_rope_blocks)
    def _():
        cos, sa, sb = cos_ref[...], sa_ref[...], sb_ref[...]
        h = h_ref[...]
        tn = w_ref.shape[1]
        cw = min(PROJ_CHUNK, tn)
        for c0 in range(0, tn, cw):
            acc = jnp.dot(h, w_ref[:, c0:c0 + cw], preferred_element_type=F32)
            for c in range(cw // LANES):
                xa = acc[:, c * LANES:(c + 1) * LANES]
                up = pltpu.roll(xa, LANES - 16, 1)
                dn = pltpu.roll(xa, 16, 1)
                lo = c0 + c * LANES
                o_ref[:, lo:lo + LANES] = (xa * cos + up * sa + dn * sb).astype(o_ref.dtype)

    @pl.when(j >= n_rope_blocks)
    def _():
        o_ref[...] = jnp.dot(h_ref[...], w_ref[...], preferred_element_type=F32).astype(o_ref.dtype)


def _rope_tables(cfg):
    t = np.arange(cfg.dec_seq)
    row = (t // cfg.grid_w).astype(np.float64)
    col = (t % cfg.grid_w).astype(np.float64)
    lane = np.arange(LANES)
    l64 = lane % cfg.a_qk
    half = cfg.a_qk // 4
    freq = cfg.rope_theta ** (-(lane % half).astype(np.float64) / half)
    pos = np.where((l64 < cfg.a_qk // 2)[None, :], row[:, None], col[:, None])
    ang = pos * freq[None, :]
    first = (lane % (2 * half)) < half
    cos = np.cos(ang)
    sin = np.sin(ang)
    sa = np.where(first[None, :], -sin, 0.0)
    sb = np.where(first[None, :], 0.0, sin)
    return tuple(jnp.asarray(a, dtype=F32) for a in (cos, sa, sb))


def _projection(cfg, x, mod_l, gain, w, *, latent, out_dtype):
    m, d = x.shape
    n = w.shape[1]
    tm = min(cfg.proj_tile, m, cfg.dec_seq)
    tn = min(1024, cfg.a_width)
    per_seq = cfg.dec_seq // tm if latent else 1
    mod_idx = (lambda i, j: (1 + i // per_seq, 0, 0)) if latent else (lambda i, j: (0, 0, 0))
    in_specs = [
        pl.BlockSpec((tm, d), lambda i, j: (i, 0)),
        pl.BlockSpec((1, 8, d), mod_idx),
        pl.BlockSpec((1, d), lambda i, j: (0, 0)),
        pl.BlockSpec((d, tn), lambda i, j: (0, j)),
    ]
    args = [x, mod_l, gain, w]
    if latent:
        tab_spec = pl.BlockSpec((tm, LANES), lambda i, j: (i % per_seq, 0))
        in_specs += [tab_spec, tab_spec, tab_spec]
        args += list(_rope_tables(cfg))
    kern = functools.partial(_proj_kernel, rope=latent, n_rope_blocks=2 * cfg.a_width // tn,
                             eps=cfg.norm_eps)
    return pl.pallas_call(
        kern,
        grid=(m // tm, n // tn),
        in_specs=in_specs,
        out_specs=pl.BlockSpec((tm, tn), lambda i, j: (i, j)),
        out_shape=jax.ShapeDtypeStruct((m, n), out_dtype),
        scratch_shapes=[pltpu.VMEM((tm, d), BF16)],
        compiler_params=_params(("arbitrary", "arbitrary")),
        name="qkv_projection_latent" if latent else "qkv_projection_context",
    )(*args)


def _nt_dot(a, b):
    return lax.dot_general(a, b, (((1,), (1,)), ((), ())), preferred_element_type=F32)


DIFF_SUB_TILE = 128


def _diff_attn_kernel(*refs, has_ctx, lam_init, eps, qk):
    if has_ctx:
        lam_ref, gain_ref, q_ref, k_ref, v_ref, ck_ref, cv_ref, o_ref = refs
    else:
        lam_ref, gain_ref, q_ref, k_ref, v_ref, o_ref = refs
    lv = lam_ref[...]
    s1 = jnp.sum(lv[0:1] * lv[1:2], axis=-1, keepdims=True)
    s2 = jnp.sum(lv[2:3] * lv[3:4], axis=-1, keepdims=True)
    lam = jnp.exp(s1) - jnp.exp(s2) + lam_init

    hd = 2 * qk
    tq = q_ref.shape[0]
    ts = min(tq, DIFF_SUB_TILE)
    for hh, t in [(hh, t) for hh in range(q_ref.shape[1] // hd) for t in range(tq // ts)]:
        cols = slice(hh * hd, (hh + 1) * hd)
        k = k_ref[:, cols].astype(BF16)
        v = v_ref[:, cols].astype(BF16)
        if has_ctx:
            ck = ck_ref[0, :, cols].astype(BF16)
            cv = cv_ref[0, :, cols].astype(BF16)
        q = q_ref[t * ts:(t + 1) * ts, cols].astype(F32) * (qk ** -0.5 * math.log2(math.e))
        lane = lax.broadcasted_iota(jnp.int32, q.shape, 1)
        qs = jnp.concatenate([jnp.where(lane < qk, q, 0.0), jnp.where(lane >= qk, q, 0.0)],
                             axis=0).astype(BF16)
        s_new = _nt_dot(qs, k)
        m = jnp.max(s_new, axis=-1, keepdims=True)
        if has_ctx:
            s_ctx = _nt_dot(qs, ck)
            m = jnp.maximum(m, jnp.max(s_ctx, axis=-1, keepdims=True))
        p_new = jnp.exp2(s_new - m)
        den = jnp.sum(p_new, axis=-1, keepdims=True)
        o2 = jnp.dot(p_new.astype(BF16), v, preferred_element_type=F32)
        if has_ctx:
            p_ctx = jnp.exp2(s_ctx - m)
            den = den + jnp.sum(p_ctx, axis=-1, keepdims=True)
            o2 = o2 + jnp.dot(p_ctx.astype(BF16), cv, preferred_element_type=F32)
        inv = 1.0 / den
        o = o2[:ts] * inv[:ts] - o2[ts:] * (lam * inv[ts:])
        ms = jnp.mean(o * o, axis=-1, keepdims=True)
        o = (o * lax.rsqrt(ms + eps)) * gain_ref[...]
        o_ref[t * ts:(t + 1) * ts, cols] = (o * (1.0 - lam_init)).astype(o_ref.dtype)


def _diff_attention(cfg, proj, lam_pack, gain, n_batch, seq_len, lam_init, ctx=None):
    hd = cfg.a_dim
    hps = cfg.a_heads if ctx is None else 1
    nh = cfg.a_heads // hps
    bw = hps * hd
    tq = min(cfg.q_tile, seq_len)
    nq = seq_len // tq
    in_specs = [
        pl.BlockSpec((8, LANES), lambda b, h, qi: (0, 0)),
        pl.BlockSpec((1, hd), lambda b, h, qi: (0, 0)),
        pl.BlockSpec((tq, bw), lambda b, h, qi: (b * nq + qi, h)),
        pl.BlockSpec((seq_len, bw), lambda b, h, qi: (b, nh + h)),
        pl.BlockSpec((seq_len, bw), lambda b, h, qi: (b, 2 * nh + h)),
    ]
    args = [lam_pack, gain, proj, proj, proj]
    if ctx is not None:
        ck, cv = ctx
        past = ck.shape[1]
        cspec = pl.BlockSpec((1, past, bw), lambda b, h, qi: (b, 0, h))
        in_specs += [cspec, cspec]
        args += [ck, cv]
    kern = functools.partial(_diff_attn_kernel, has_ctx=ctx is not None, lam_init=lam_init,
                             eps=cfg.subln_eps, qk=cfg.a_qk)
    return pl.pallas_call(
        kern,
        grid=(n_batch, nh, nq),
        in_specs=in_specs,
        out_specs=pl.BlockSpec((tq, bw), lambda b, h, qi: (b * nq + qi, h)),
        out_shape=jax.ShapeDtypeStruct((n_batch * seq_len, cfg.a_width), BF16),
        compiler_params=_params(("arbitrary", "arbitrary", "arbitrary")),
        name="diff_attention_latent" if ctx is not None else "diff_attention_context",
    )(*args)


def _soft_attn_kernel(q_ref, k_ref, v_ref, o_ref, *, hd):
    for hh in range(q_ref.shape[1] // hd):
        cols = slice(hh * hd, (hh + 1) * hd)
        s = _nt_dot(q_ref[:, cols].astype(BF16), k_ref[:, cols].astype(BF16)) * (hd ** -0.5 * math.log2(math.e))
        m = jnp.max(s, axis=-1, keepdims=True)
        p = jnp.exp2(s - m)
        inv = 1.0 / jnp.sum(p, axis=-1, keepdims=True)
        o = jnp.dot(p.astype(BF16), v_ref[:, cols].astype(BF16), preferred_element_type=F32) * inv
        o_ref[:, cols] = o.astype(o_ref.dtype)


def _soft_attention(cfg, proj, n_batch, seq_len):
    bw = cfg.b_width
    assert (3 * cfg.a_width) % bw == 0
    base = 3 * cfg.a_width // bw
    return pl.pallas_call(
        functools.partial(_soft_attn_kernel, hd=cfg.b_dim),
        grid=(n_batch,),
        in_specs=[
            pl.BlockSpec((seq_len, bw), lambda b: (b, base)),
            pl.BlockSpec((seq_len, bw), lambda b: (b, base + 1)),
            pl.BlockSpec((seq_len, bw), lambda b: (b, base + 2)),
        ],
        out_specs=pl.BlockSpec((seq_len, bw), lambda b: (b, 0)),
        out_shape=jax.ShapeDtypeStruct((n_batch * seq_len, bw), BF16),
        compiler_params=_params(("arbitrary",)),
        name="softmax_attention_context",
    )(proj, proj, proj)


NA_GROUP_ROWS = 4


def _na_geometry(cfg):
    rows = cfg.dec_seq // cfg.grid_w
    kr = min(cfg.na_win_r, rows)
    grp = min(NA_GROUP_ROWS, rows)
    union = min(rows, kr + grp - 1 + (kr + grp - 1) % 2)
    starts = []
    for g in range(rows // grp):
        rs0 = min(max(g * grp - kr // 2, 0), rows - kr)
        starts.append(min(rs0, rows - union))
    return rows, kr, grp, union, starts


def _na_kernel(ws_ref, plane_ref, q_ref, k_ref, v_ref, ck_ref, cv_ref, tab_ref, o_ref, sctx, pctx, oacc, *,
               n_groups, grp, union, gw, scale):
    gq, uk = grp * gw, union * gw
    ck = ck_ref[0].astype(BF16)
    cv = cv_ref[0].astype(BF16)
    sctx[...] = _nt_dot(q_ref[...], ck) * scale
    first_half = lax.broadcasted_iota(jnp.int32, (gw, LANES), 1) < gw

    def body(g, carry):
        q0 = pl.multiple_of(g * gq, gq)
        k0 = pl.multiple_of(ws_ref[g] * gw, gw)
        q = q_ref[pl.ds(q0, gq), :]
        bias_rows = []
        for a in range(grp):
            base = (g * grp + a) * union
            tiles = [jnp.where(first_half, tab_ref[0, plane_ref[base + 2 * jj]],
                               tab_ref[0, plane_ref[base + 2 * jj + 1]]) for jj in range(union // 2)]
            bias_rows.append(jnp.concatenate(tiles, axis=1))
        bias = jnp.concatenate(bias_rows, axis=0)
        s = _nt_dot(q, k_ref[pl.ds(k0, uk), :]) * scale + bias
        sc = sctx[pl.ds(q0, gq), :]
        m = jnp.maximum(jnp.max(s, axis=-1, keepdims=True), jnp.max(sc, axis=-1, keepdims=True))
        p = jnp.exp2(s - m)
        pc = jnp.exp2(sc - m)
        inv = 1.0 / (jnp.sum(p, axis=-1, keepdims=True) + jnp.sum(pc, axis=-1, keepdims=True))
        oacc[pl.ds(q0, gq), :] = jnp.dot(p.astype(BF16), v_ref[pl.ds(k0, uk), :],
                                         preferred_element_type=F32) * inv
        pctx[pl.ds(q0, gq), :] = (pc * inv).astype(BF16)
        return carry

    lax.fori_loop(0, n_groups, body, 0, unroll=2)
    o_ref[...] = (oacc[...] + jnp.dot(pctx[...], cv, preferred_element_type=F32)).astype(o_ref.dtype)


def _na_bias_table(cfg, rpb):
    w = cfg.grid_w
    assert LANES == 2 * w
    rows, kr, grp, union, starts = _na_geometry(cfg)
    qc = np.arange(w)
    kc = np.arange(w)
    cs = np.clip(qc - cfg.na_win_c // 2, 0, w - cfg.na_win_c)
    col_mask = (kc[None, :] >= cs[:, None]) & (kc[None, :] < cs[:, None] + cfg.na_win_c)
    col_idx = np.clip(kc[None, :] - qc[:, None] + cfg.na_win_c - 1, 0, 2 * cfg.na_win_c - 2)
    n_c = 2 * cfg.na_win_c - 1
    onehot = (col_idx[None] == np.arange(n_c)[:, None, None]) & col_mask[None]
    t = jnp.einsum('hrc,cqk->hrqk', rpb.astype(F32), jnp.asarray(onehot, dtype=F32),
                   precision=lax.Precision.HIGHEST)
    t = jnp.where(jnp.asarray(col_mask)[None, None], t * math.log2(math.e), NEG_BIG)
    n_r = 2 * cfg.na_win_r - 1
    t = jnp.concatenate([t, jnp.full((cfg.b_heads, 1, w, w), NEG_BIG, F32)], axis=1)
    plane = np.full((len(starts), grp, union), n_r, np.int32)
    for g, ws in enumerate(starts):
        for a in range(grp):
            r = g * grp + a
            rs = min(max(r - kr // 2, 0), rows - kr)
            for j in range(union):
                if rs <= ws + j < rs + kr:
                    plane[g, a, j] = ws + j - r + cfg.na_win_r - 1
    return jnp.concatenate([t, t], axis=-1), jnp.asarray(plane.reshape(-1))


def _na_attention(cfg, proj, ck, cv, rpb):
    hd = cfg.b_dim
    nh = cfg.b_heads
    n = cfg.dec_seq
    rows, kr, grp, union, starts = _na_geometry(cfg)
    n_groups = len(starts)
    base = 3 * cfg.a_width // hd
    past = ck.shape[1]
    table, plane = _na_bias_table(cfg, rpb)
    kern = functools.partial(_na_kernel, n_groups=n_groups, grp=grp, union=union, gw=cfg.grid_w,
                             scale=hd ** -0.5 * math.log2(math.e))
    cspec = pl.BlockSpec((1, past, hd), lambda h, b, *_: (b, 0, h))
    return pl.pallas_call(
        kern,
        grid_spec=pltpu.PrefetchScalarGridSpec(
            num_scalar_prefetch=2,
            grid=(nh, cfg.dec_batch),
            in_specs=[
                pl.BlockSpec((n, hd), lambda h, b, *_: (b, base + h)),
                pl.BlockSpec((n, hd), lambda h, b, *_: (b, base + nh + h)),
                pl.BlockSpec((n, hd), lambda h, b, *_: (b, base + 2 * nh + h)),
                cspec, cspec,
                pl.BlockSpec((1,) + table.shape[1:], lambda h, b, *_: (h, 0, 0, 0)),
            ],
            out_specs=pl.BlockSpec((n, hd), lambda h, b, *_: (b, h)),
            scratch_shapes=[pltpu.VMEM((n, past), F32), pltpu.VMEM((n, past), BF16), pltpu.VMEM((n, hd), F32)],
        ),
        out_shape=jax.ShapeDtypeStruct((cfg.n_s, cfg.b_width), BF16),
        compiler_params=_params(("arbitrary", "arbitrary")),
        name="neighbourhood_attention",
    )(jnp.asarray(np.asarray(starts, np.int32)), plane, proj, proj, proj, ck, cv, table)


def _linres_kernel(*refs, n_parts, n_pb, x_split):
    i = pl.program_id(0)
    is_p = i < n_pb
    pos = 0
    acc = None
    w_ref = refs[2 * n_parts]
    k0 = 0
    for p in range(n_parts):
        a_p, a_s = refs[2 * p], refs[2 * p + 1]
        a = jnp.where(is_p, a_p[...], a_s[...])
        kk = a.shape[1]
        part = jnp.dot(a, w_ref[k0:k0 + kk, :], preferred_element_type=F32)
        acc = part if acc is None else acc + part
        k0 += kk
    pos = 2 * n_parts + 1
    if x_split:
        x = jnp.where(is_p, refs[pos][...], refs[pos + 1][...])
        pos += 2
    else:
        x = refs[pos][...]
        pos += 1
    mod_ref, o_ref = refs[pos], refs[pos + 1]
    o_ref[...] = x + mod_ref[0, 2:3, :] * acc


def _linear_residual(cfg, parts, w, x, mod_l):
    d = cfg.d_model
    tm = min(256, cfg.tm)
    n_pb = cfg.n_p // tm
    n_sb = cfg.n_s // tm
    p_idx = lambda i: (jnp.minimum(i, n_pb - 1), 0)
    s_idx = lambda i: (jnp.maximum(i - n_pb, 0), 0)
    in_specs, args = [], []
    for a_p, a_s in parts:
        kk = a_p.shape[1]
        in_specs += [pl.BlockSpec((tm, kk), p_idx), pl.BlockSpec((tm, kk), s_idx)]
        args += [a_p, a_s]
    in_specs.append(pl.BlockSpec(w.shape, lambda i: (0, 0)))
    args.append(w)
    x_split = isinstance(x, tuple)
    if x_split:
        in_specs += [pl.BlockSpec((tm, d), p_idx), pl.BlockSpec((tm, d), s_idx)]
        args += list(x)
    else:
        in_specs.append(pl.BlockSpec((tm, d), lambda i: (i, 0)))
        args.append(x)
    mi = _merged_mod_index(cfg, tm)
    in_specs.append(pl.BlockSpec((1, 8, d), lambda i: (mi(i), 0, 0)))
    args.append(mod_l)
    kern = functools.partial(_linres_kernel, n_parts=len(parts), n_pb=n_pb, x_split=x_split)
    return pl.pallas_call(
        kern,
        grid=(n_pb + n_sb,),
        in_specs=in_specs,
        out_specs=pl.BlockSpec((tm, d), lambda i: (i, 0)),
        out_shape=jax.ShapeDtypeStruct((cfg.n_tok, d), F32),
        compiler_params=_params(("arbitrary",)),
        name="linear_gated_residual",
    )(*args)


def _dft_chan_kernel(x_ref, mod_ref, g_ref, cs_ref, y_ref, *, groups, eps):
    h = _norm_mod(x_ref[...], g_ref[...], mod_ref[0, 0:1, :], mod_ref[0, 1:2, :], eps).astype(BF16)
    gd = h.shape[1] // groups
    for g in range(groups):
        r = jnp.dot(h[:, g * gd:(g + 1) * gd], cs_ref[...], preferred_element_type=F32)
        y_ref[0, :, g * gd:(g + 1) * gd] = r[:, :gd].astype(BF16)
        y_ref[1, :, g * gd:(g + 1) * gd] = r[:, gd:].astype(BF16)


def _dft_mats(n):
    k = np.arange(n)
    ang = 2.0 * np.pi * ((k[:, None] * k[None, :]) % n) / n
    return np.cos(ang), np.sin(ang)


def _dft_channels(cfg, x, mod_l, gain):
    d = cfg.d_model
    gd = d // cfg.f_groups
    tm = cfg.tm
    c, s = _dft_mats(gd)
    cs = jnp.asarray(np.concatenate([c, s], axis=1), dtype=F32).astype(BF16)
    mi = _merged_mod_index(cfg, tm)
    return pl.pallas_call(
        functools.partial(_dft_chan_kernel, groups=cfg.f_groups, eps=cfg.norm_eps),
        grid=(cfg.n_tok // tm,),
        in_specs=[
            pl.BlockSpec((tm, d), lambda i: (i, 0)),
            pl.BlockSpec((1, 8, d), lambda i: (mi(i), 0, 0)),
            pl.BlockSpec((1, d), lambda i: (0, 0)),
            pl.BlockSpec((gd, 2 * gd), lambda i: (0, 0)),
        ],
        out_specs=pl.BlockSpec((2, tm, d), lambda i: (0, i, 0)),
        out_shape=jax.ShapeDtypeStruct((2, cfg.n_tok, d), BF16),
        compiler_params=_params(("arbitrary",)),
        name="dft_channels",
    )(x, mod_l, gain, cs)


def _dft_seq_kernel(w_ref, y_ref, o_ref, *, scale):
    acc = (jnp.dot(w_ref[0], y_ref[0], preferred_element_type=F32)
           + jnp.dot(w_ref[1], y_ref[1], preferred_element_type=F32))
    o_ref[...] = (acc * scale).astype(o_ref.dtype)


def _dft_sequence(cfg, y, n_batch, seq_len, first_block):
    d = cfg.d_model
    c, s = _dft_mats(seq_len)
    wm = jnp.asarray(np.stack([c, -s]), dtype=F32).astype(BF16)
    tml = min(1024, seq_len)
    tn = min(512, d)
    nm = seq_len // tml
    scale = 1.0 / math.sqrt(seq_len * (d // cfg.f_groups))
    return pl.pallas_call(
        functools.partial(_dft_seq_kernel, scale=scale),
        grid=(n_batch, nm, d // tn),
        in_specs=[
            pl.BlockSpec((2, tml, seq_len), lambda b, mi, j: (0, mi, 0)),
            pl.BlockSpec((2, seq_len, tn), lambda b, mi, j: (0, first_block + b, j)),
        ],
        out_specs=pl.BlockSpec((tml, tn), lambda b, mi, j: (b * nm + mi, j)),
        out_shape=jax.ShapeDtypeStruct((n_batch * seq_len, d), BF16),
        compiler_params=_params(("arbitrary", "arbitrary", "arbitrary")),
        name="dft_sequence_%d" % seq_len,
    )(wm, y)


TOKEN_TILE_ROWS = 8


def _token_tile(d):
    rows = min(TOKEN_TILE_ROWS, d // LANES)
    return rows, d // (rows * LANES)


def _tile_chunk(c, n_tok, rows, first=0):
    return c // rows, pl.ds(first + c % rows, n_tok, stride=rows), slice(None)


def _tile_tokens(ref, first_row, n_rows):
    return ref.at[:, pl.ds(first_row, n_rows), :]


def _route_kernel(x_ref, mod_ref, g_ref, wr_ref, rb_ref, hp_ref, idx_ref, wt_ref, *, eps, n_exp, per_grp):
    h = _norm_mod(x_ref[...], g_ref[...], mod_ref[0, 3:4, :], mod_ref[0, 4:5, :], eps)
    tm, d = h.shape
    rt, _ = _token_tile(d)
    for c in range(d // LANES):
        hp_ref[_tile_chunk(c, tm, rt)] = h[:, c * LANES:(c + 1) * LANES]

    h_hi = h.astype(BF16)
    h_lo = (h - h_hi.astype(F32)).astype(BF16)
    w = wr_ref[...]
    w_hi = w.astype(BF16)
    w_lo = (w - w_hi.astype(F32)).astype(BF16)
    logits = _nt_dot(w_hi, h_hi) + (_nt_dot(w_hi, h_lo) + _nt_dot(w_lo, h_hi))
    scores = jax.nn.sigmoid(logits)
    sel = scores + rb_ref[...]
    n_grp = n_exp // per_grp
    best = None
    gi = None
    for g in range(n_grp):
        v = [sel[g * per_grp + k:g * per_grp + k + 1, :] for k in range(per_grp)]
        gs = None
        for a in range(per_grp):
            for b in range(a + 1, per_grp):
                ps = v[a] + v[b]
                gs = ps if gs is None else jnp.maximum(gs, ps)
        if best is None:
            best, gi = gs, jnp.zeros(gs.shape, jnp.int32)
        else:
            better = gs > best
            gi = jnp.where(better, g, gi)
            best = jnp.where(better, gs, best)
    row = lax.broadcasted_iota(jnp.int32, sel.shape, 0)
    masked = jnp.where(row // per_grp == gi, sel, -jnp.inf)
    m1 = jnp.max(masked, axis=0, keepdims=True)
    i1 = jnp.min(jnp.where(masked == m1, row, n_exp), axis=0, keepdims=True)
    masked2 = jnp.where(row == i1, -jnp.inf, masked)
    m2 = jnp.max(masked2, axis=0, keepdims=True)
    i2 = jnp.min(jnp.where(masked2 == m2, row, n_exp), axis=0, keepdims=True)
    w1 = jnp.sum(jnp.where(row == i1, scores, 0.0), axis=0, keepdims=True)
    w2 = jnp.sum(jnp.where(row == i2, scores, 0.0), axis=0, keepdims=True)
    inv = 1.0 / (w1 + w2)
    idx_ref[...] = jnp.concatenate([i1, i2], axis=0)
    wt_ref[...] = jnp.concatenate([w1 * inv, w2 * inv], axis=0)


def _route(cfg, x, mod_l, gain, w_router_t, router_bias):
    d = cfg.d_model
    tm = cfg.tm
    t = cfg.n_tok
    rt, npl = _token_tile(d)
    mi = _merged_mod_index(cfg, tm)
    kern = functools.partial(_route_kernel, eps=cfg.norm_eps, n_exp=cfg.n_experts,
                             per_grp=cfg.n_experts // cfg.n_groups)
    return pl.pallas_call(
        kern,
        grid=(t // tm,),
        in_specs=[
            pl.BlockSpec((tm, d), lambda i: (i, 0)),
            pl.BlockSpec((1, 8, d), lambda i: (mi(i), 0, 0)),
            pl.BlockSpec((1, d), lambda i: (0, 0)),
            pl.BlockSpec((cfg.n_experts, d), lambda i: (0, 0)),
            pl.BlockSpec((cfg.n_experts, 1), lambda i: (0, 0)),
        ],
        out_specs=[
            pl.BlockSpec((npl, tm * rt, LANES), lambda i: (0, i, 0)),
            pl.BlockSpec((2, tm), lambda i: (0, i)),
            pl.BlockSpec((2, tm), lambda i: (0, i)),
        ],
        out_shape=[
            jax.ShapeDtypeStruct((npl, t * rt, LANES), F32),
            jax.ShapeDtypeStruct((2, t), jnp.int32),
            jax.ShapeDtypeStruct((2, t), F32),
        ],
        compiler_params=_params(("arbitrary",)),
        name="moe_route",
    )(x, mod_l, gain, w_router_t, router_bias.reshape(cfg.n_experts, 1))


def _plan(cfg, idx):
    t = cfg.n_tok
    eb = cfg.e_block
    ne = cfg.n_experts
    n_blocks = -(-(2 * t + ne * (eb - 1)) // eb)
    e_flat = idx.reshape(-1)
    onehot = (e_flat[:, None] == jnp.arange(ne, dtype=jnp.int32)[None, :]).astype(jnp.int32)
    csum = jnp.cumsum(onehot, axis=0)
    rank = jnp.sum(onehot * (csum - 1), axis=1)
    counts = csum[-1]
    padded = (counts + eb - 1) // eb * eb
    pad_end = jnp.cumsum(padded)
    pad_start = pad_end - padded
    pos = (pad_start[e_flat] + rank).astype(jnp.int32).reshape(2, t)
    n_used = (pad_end[-1] // eb).astype(jnp.int32)
    blk = jnp.arange(n_blocks, dtype=jnp.int32)
    blk = jnp.minimum(blk, n_used - 1)
    first_e = jnp.sum((pad_end[None, :] <= (blk * eb)[:, None]).astype(jnp.int32), axis=1)
    block_e = jnp.minimum(first_e, ne - 1).astype(jnp.int32)
    return pos, block_e, n_used.reshape(1), pad_end.astype(jnp.int32), n_blocks


def _weight_schedule(cfg, block_e, n_used, n_blocks):
    n_chunks = _expert_chunks(cfg)[2]
    big = n_blocks + 1
    idx = jnp.arange(n_blocks, dtype=jnp.int32)
    valid = idx < n_used[0]
    prev_e = jnp.concatenate([jnp.full((1,), -1, jnp.int32), block_e[:-1]])
    change = (block_e != prev_e) & valid
    earlier = (idx[None, :] <= idx[:, None]) & change[None, :]
    later = (idx[None, :] > idx[:, None]) & change[None, :]
    run_start = jnp.max(jnp.where(earlier, idx[None, :], 0), axis=1)
    next_start = jnp.minimum(jnp.min(jnp.where(later, idx[None, :], big), axis=1), n_used[0])
    slot = (jnp.sum(earlier.astype(jnp.int32), axis=1) - 1) % 2
    has_next = valid & (next_start < n_used[0])
    next_e = jnp.where(has_next, block_e[jnp.minimum(next_start, n_blocks - 1)], 0)
    run_len = jnp.maximum(next_start - run_start, 1)
    per_block = (n_chunks + run_len - 1) // run_len
    i = idx - run_start
    c0 = jnp.where(has_next, jnp.minimum(i * per_block, n_chunks), 0)
    c1 = jnp.where(has_next, jnp.minimum((i + 1) * per_block, n_chunks), 0)
    return tuple(a.astype(jnp.int32) for a in (slot, next_e, c0, c1))


def _dispatch_kernel(pend_ref, pos_ref, hp_ref, xr_ref, zbuf, sem, zsem, *, n_exp, eb, spt):
    i = pl.program_id(0)

    @pl.when(i == 0)
    def _():
        zbuf[...] = jnp.zeros(zbuf.shape, zbuf.dtype)
        for e in range(n_exp):
            start = pl.multiple_of(jnp.maximum(pend_ref[e] - eb, 0) * spt, eb * spt)
            cp = pltpu.make_async_copy(zbuf, _tile_tokens(xr_ref, start, eb * spt), zsem)
            cp.start()
            cp.wait()
        n_used = pend_ref[n_exp - 1] // eb
        n_blocks = xr_ref.shape[1] // (eb * spt)
        for e in range(n_exp):
            @pl.when(n_used + e < n_blocks)
            def _():
                start = pl.multiple_of((n_used + e) * (eb * spt), eb * spt)
                cp = pltpu.make_async_copy(zbuf, _tile_tokens(xr_ref, start, eb * spt), zsem)
                cp.start()
                cp.wait()

    rows = pos_ref.shape[2]

    def start(r, c):
        src = _tile_tokens(hp_ref, pl.multiple_of(r * spt, spt), spt)
        for k in range(2):
            dst = _tile_tokens(xr_ref, pl.multiple_of(pos_ref[0, k, r] * spt, spt), spt)
            pltpu.make_async_copy(src, dst, sem.at[k]).start(priority=k)
        return c

    lax.fori_loop(0, rows, start, 0, unroll=8)
    for k in range(2):
        pltpu.make_async_copy(hp_ref, _tile_tokens(xr_ref, 0, rows * spt), sem.at[k]).wait()


def _dispatch(cfg, hp, pos_blocks, pad_end, n_rows):
    tb = pos_blocks.shape[2]
    spt, npl = _token_tile(cfg.d_model)
    kern = functools.partial(_dispatch_kernel, n_exp=cfg.n_experts, eb=cfg.e_block, spt=spt)
    return pl.pallas_call(
        kern,
        grid_spec=pltpu.PrefetchScalarGridSpec(
            num_scalar_prefetch=1,
            grid=(cfg.n_tok // tb,),
            in_specs=[
                pl.BlockSpec((1, 2, tb), lambda i, pe: (i, 0, 0), memory_space=pltpu.SMEM),
                pl.BlockSpec((npl, tb * spt, LANES), lambda i, pe: (0, i, 0)),
            ],
            out_specs=pl.BlockSpec(memory_space=pl.ANY),
            scratch_shapes=[
                pltpu.VMEM((npl, cfg.e_block * spt, LANES), F32),
                pltpu.SemaphoreType.DMA((2,)),
                pltpu.SemaphoreType.DMA(()),
            ],
        ),
        out_shape=jax.ShapeDtypeStruct((npl, n_rows * spt, LANES), F32),
        compiler_params=_params(("arbitrary",)),
        name="moe_dispatch",
    )(pad_end, pos_blocks, hp)


EXPERT_CHUNK_ROWS = 128
EXPERT_RING = 6
EXPERT_VMEM_LIMIT = 60 * 1024 * 1024


def _expert_chunks(cfg):
    n_in = cfg.d_model // EXPERT_CHUNK_ROWS
    n_out = cfg.d_expert // EXPERT_CHUNK_ROWS
    return n_in, n_out, 2 * n_in + n_out


def _expert_kernel(nu_ref, e0_ref, slot_ref, ne_ref, c0_ref, c1_ref, xp_ref, wg_ref, wu_ref, wd_ref, y_ref,
                   wg_b, wu_b, wd_b, stg, sem, *, n_in, n_out, layer):
    b = pl.program_id(0)
    n_chunks = 2 * n_in + n_out
    ring = stg.shape[0]
    rows = EXPERT_CHUNK_ROWS

    parts = [(0, n_in, wg_ref, wg_b), (n_in, n_in, wu_ref, wu_b), (2 * n_in, n_out, wd_ref, wd_b)]

    def for_chunk(c, e, fn):
        s = c % ring
        for first, count, src, dst in parts:
            @pl.when((c >= first) & (c < first + count))
            def _():
                row = pl.multiple_of((c - first) * rows, rows)
                window = stg.at[s, :, pl.ds(0, dst.shape[2])]
                cp = pltpu.make_async_copy(src.at[layer, e, pl.ds(row, rows)], window, sem.at[s])
                fn(cp, dst, row, window)

    def start_chunk(c, e):
        for_chunk(c, e, lambda cp, dst, row, window: cp.start())

    def finish_chunk(c, e, dst_slot):
        def cast(cp, dst, row, window):
            cp.wait()
            dst[dst_slot, pl.ds(row, rows), :] = window[...].astype(BF16)
        for_chunk(c, e, cast)

    def load_chunks(e, dst_slot, lo, hi):
        @pl.when((lo == 0) & (hi > 0))
        def _():
            for k in range(min(ring, n_chunks)):
                start_chunk(k, e)

        def body(c, carry):
            finish_chunk(c, e, dst_slot)

            @pl.when(c + ring < n_chunks)
            def _():
                start_chunk(c + ring, e)
            return carry

        lax.fori_loop(lo, hi, body, 0)

    @pl.when(b == 0)
    def _():
        load_chunks(e0_ref[0], 0, 0, n_chunks)

    @pl.when(b < nu_ref[0])
    def _():
        slot = slot_ref[b]
        load_chunks(ne_ref[b], 1 - slot, c0_ref[b], c1_ref[b])
        d = wg_b.shape[1]
        n_chunk = d // LANES
        rt, _ = _token_tile(d)
        eb = xp_ref.shape[1] // rt
        x = jnp.concatenate([xp_ref[_tile_chunk(c, eb, rt)].astype(BF16) for c in range(n_chunk)], axis=1)
        g = jnp.dot(x, wg_b[slot], preferred_element_type=F32)
        u = jnp.dot(x, wu_b[slot], preferred_element_type=F32)
        a = (_silu(g) * u).astype(BF16)
        y = jnp.dot(a, wd_b[slot], preferred_element_type=F32)
        for c in range(n_chunk):
            y_ref[_tile_chunk(c, eb, rt)] = y[:, c * LANES:(c + 1) * LANES]

    @pl.when(b >= nu_ref[0])
    def _():
        y_ref[...] = jnp.zeros(y_ref.shape, y_ref.dtype)


def _experts(cfg, x_rows, block_e, n_used, wg, wu, wd, layer, n_blocks):
    d = cfg.d_model
    f = cfg.d_expert
    eb = cfg.e_block
    rt, npl = _token_tile(d)
    n_in, n_out, _ = _expert_chunks(cfg)
    slot, next_e, c0, c1 = _weight_schedule(cfg, block_e, n_used, n_blocks)
    any_spec = pl.BlockSpec(memory_space=pl.ANY)
    x_idx = lambda b, nu, *_: (0, jnp.minimum(b, nu[0] - 1), 0)
    return pl.pallas_call(
        functools.partial(_expert_kernel, n_in=n_in, n_out=n_out, layer=layer),
        grid_spec=pltpu.PrefetchScalarGridSpec(
            num_scalar_prefetch=6,
            grid=(n_blocks,),
            in_specs=[pl.BlockSpec((npl, eb * rt, LANES), x_idx), any_spec, any_spec, any_spec],
            out_specs=pl.BlockSpec((npl, eb * rt, LANES), lambda b, *_: (0, b, 0)),
            scratch_shapes=[
                pltpu.VMEM((2, d, f), BF16), pltpu.VMEM((2, d, f), BF16), pltpu.VMEM((2, f, d), BF16),
                pltpu.VMEM((EXPERT_RING, EXPERT_CHUNK_ROWS, max(d, f)), F32),
                pltpu.SemaphoreType.DMA((EXPERT_RING,)),
            ],
        ),
        out_shape=jax.ShapeDtypeStruct((npl, n_blocks * eb * rt, LANES), F32),
        compiler_params=pltpu.CompilerParams(dimension_semantics=("arbitrary",),
                                             vmem_limit_bytes=EXPERT_VMEM_LIMIT),
        name="moe_experts",
    )(n_used, block_e[:1], slot, next_e, c0, c1, x_rows, wg, wu, wd)


def _combine_kernel(pos_ref, nxt_ref, y_ref, x_ref, mod_ref, wt_ref, fg_ref, o_ref, ybuf, sem, *, final, eps):
    i = pl.program_id(0)
    n_steps = pl.num_programs(0)
    rows, d = x_ref.shape
    n_chunk = d // LANES
    rt, _ = _token_tile(d)
    slot = i % 2
    per_choice = rows * rt

    def gather(p_ref, s):
        def start(r, c):
            for k in range(2):
                src = _tile_tokens(y_ref, pl.multiple_of(p_ref[0, k, r] * rt, rt), rt)
                dst = _tile_tokens(ybuf.at[s], pl.multiple_of(k * per_choice + r * rt, rt), rt)
                pltpu.make_async_copy(src, dst, sem.at[s]).start(priority=k)
            return c
        lax.fori_loop(0, rows, start, 0, unroll=8)

    @pl.when(i == 0)
    def _():
        gather(pos_ref, 0)

    @pl.when(i + 1 < n_steps)
    def _():
        gather(nxt_ref, 1 - slot)

    pltpu.make_async_copy(_tile_tokens(y_ref, 0, 2 * per_choice), ybuf.at[slot], sem.at[slot]).wait()

    w = wt_ref[...]
    w0, w1 = w[:, 0:1], w[:, 1:2]
    sumsq = jnp.zeros((rows, 1), F32)
    for c in range(n_chunk):
        cols = slice(c * LANES, (c + 1) * LANES)
        moe = (w0 * ybuf[(slot,) + _tile_chunk(c, rows, rt)]
               + w1 * ybuf[(slot,) + _tile_chunk(c, rows, rt, first=per_choice)])
        xc = x_ref[:, cols] + mod_ref[0, 5:6, cols] * moe
        o_ref[:, cols] = xc
        sumsq = sumsq + jnp.sum(xc * xc, axis=-1, keepdims=True)
    if final:
        o_ref[...] = (o_ref[...] * lax.rsqrt(sumsq * (1.0 / d) + eps)) * fg_ref[...]


def _combine(cfg, y, pos_blocks, wts_t, x, mod_l, final_gain, *, row0, n_rows, final):
    d = cfg.d_model
    tb = pos_blocks.shape[2]
    rt, npl = _token_tile(d)
    b0 = row0 // tb
    mi = _merged_mod_index(cfg, tb)
    kern = functools.partial(_combine_kernel, final=final, eps=cfg.norm_eps)
    n_steps = n_rows // tb
    return pl.pallas_call(
        kern,
        grid=(n_steps,),
        in_specs=[
            pl.BlockSpec((1, 2, tb), lambda i: (b0 + i, 0, 0), memory_space=pltpu.SMEM),
            pl.BlockSpec((1, 2, tb), lambda i: (b0 + jnp.minimum(i + 1, n_steps - 1), 0, 0),
                         memory_space=pltpu.SMEM),
            pl.BlockSpec(memory_space=pl.ANY),
            pl.BlockSpec((tb, d), lambda i: (b0 + i, 0)),
            pl.BlockSpec((1, 8, d), lambda i: (mi(b0 + i), 0, 0)),
            pl.BlockSpec((tb, 2), lambda i: (b0 + i, 0)),
            pl.BlockSpec((1, d), lambda i: (0, 0)),
        ],
        out_specs=pl.BlockSpec((tb, d), lambda i: (i, 0)),
        out_shape=jax.ShapeDtypeStruct((n_rows, d), F32),
        scratch_shapes=[pltpu.VMEM((2, npl, 2 * tb * rt, LANES), F32), pltpu.SemaphoreType.DMA((2,))],
        compiler_params=_params(("arbitrary",)),
        name="moe_combine_final" if final else "moe_combine",
    )(pos_blocks, pos_blocks, y, x, mod_l, wts_t, final_gain)


def _moe(cfg, x, mod_l, gain, w_router_t, router_bias, wg, wu, wd, layer, final_gain, final):
    tb = min(256, cfg.tm)
    hp, idx, wts = _route(cfg, x, mod_l, gain, w_router_t, router_bias)
    pos, block_e, n_used, pad_end, n_blocks = _plan(cfg, idx)
    pos_blocks = pos.reshape(2, cfg.n_tok // tb, tb).transpose(1, 0, 2)
    td = cfg.tm
    x_rows = _dispatch(cfg, hp, pos.reshape(2, cfg.n_tok // td, td).transpose(1, 0, 2), pad_end,
                       n_blocks * cfg.e_block)
    y = _experts(cfg, x_rows, block_e, n_used, wg, wu, wd, layer, n_blocks)
    wts_t = wts.T
    comb = functools.partial(_combine, cfg, y, pos_blocks, wts_t, x, mod_l, final_gain, final=final)
    if final:
        return comb(row0=0, n_rows=cfg.n_p), comb(row0=cfg.n_p, n_rows=cfg.n_s)
    return comb(row0=0, n_rows=cfg.n_tok)


def _forward(cfg, x_prompt, x_sample, cache_a_k, cache_a_v, cache_b_k, cache_b_v, c, c_ctx, w_ada, b_ada,
             norm1, norm2, final_norm, w_in, w_mix_out, lambda_q1, lambda_k1, lambda_q2, lambda_k2,
             subln_gain, na_rel_bias, w_fourier_out, w_router, router_bias, w_exp_gate, w_exp_up,
             w_exp_down):
    d = cfg.d_model
    xp = x_prompt.reshape(cfg.n_p, d)
    xs = x_sample.reshape(cfg.n_s, d)

    n_cond = -(-cfg.n_mod // 8) * 8
    cond = jnp.concatenate([c_ctx[None, :], c, jnp.zeros((n_cond - cfg.n_mod, d), F32)], axis=0)
    mod = _modulation(cfg, cond, w_ada, b_ada)
    mod = mod.reshape(cfg.depth, n_cond, 6, d)[:, :cfg.n_mod]
    mod = jnp.pad(mod, ((0, 0), (0, 0), (0, 2), (0, 0)))

    w_router_t = w_router.T
    fgain = final_norm.reshape(1, d)
    x = (xp, xs)
    kv_cache = None
    for l in range(cfg.depth):
        j = l // 2
        g1 = norm1[l].reshape(1, d)
        if l % 2 == 0:
            lam_init = 0.8 - 0.6 * math.exp(-0.3 * l)
            lam_pack = jnp.zeros((8, LANES), F32).at[:4, :cfg.a_qk].set(
                jnp.stack([lambda_q1[j], lambda_k1[j], lambda_q2[j], lambda_k2[j]]))
            sgain = subln_gain[j].reshape(1, cfg.a_dim)
            w_in_b = w_in[j].astype(BF16)
            if isinstance(x, tuple):
                x_p, x_s = x
            else:
                x_p, x_s = x[:cfg.n_p], x[cfg.n_p:]
            proj_p = _projection(cfg, x_p, mod[l], g1, w_in_b, latent=False, out_dtype=F32)
            proj_s = _projection(cfg, x_s, mod[l], g1, w_in_b, latent=True, out_dtype=BF16)
            if kv_cache is None:
                kv_cache = []
            aw, bw = cfg.a_width, cfg.b_width
            kv_cache.append((proj_p[:, aw:2 * aw], proj_p[:, 2 * aw:3 * aw],
                             proj_p[:, 3 * aw + bw:3 * aw + 2 * bw], proj_p[:, 3 * aw + 2 * bw:]))
            cak = cache_a_k[:, j].reshape(cfg.dec_batch, cfg.past_len, aw)
            cav = cache_a_v[:, j].reshape(cfg.dec_batch, cfg.past_len, aw)
            cbk = cache_b_k[:, j].reshape(cfg.dec_batch, cfg.past_len, bw)
            cbv = cache_b_v[:, j].reshape(cfg.dec_batch, cfg.past_len, bw)
            a_p = _diff_attention(cfg, proj_p, lam_pack, sgain, cfg.batch, cfg.seq, lam_init)
            b_p = _soft_attention(cfg, proj_p, cfg.batch, cfg.seq)
            a_s = _diff_attention(cfg, proj_s, lam_pack, sgain, cfg.dec_batch, cfg.dec_seq, lam_init,
                                  ctx=(cak, cav))
            b_s = _na_attention(cfg, proj_s, cbk, cbv, na_rel_bias[j])
            x = _linear_residual(cfg, [(a_p, a_s), (b_p, b_s)], w_mix_out[j].astype(BF16), x, mod[l])
        else:
            if isinstance(x, tuple):
                x = jnp.concatenate(x, axis=0)
            y = _dft_channels(cfg, x, mod[l], g1)
            f_p = _dft_sequence(cfg, y, cfg.batch, cfg.seq, 0)
            f_s = _dft_sequence(cfg, y, cfg.dec_batch, cfg.dec_seq, cfg.n_p // cfg.dec_seq)
            x = _linear_residual(cfg, [(f_p, f_s)], w_fourier_out[j].astype(BF16), x, mod[l])
        x = _moe(cfg, x, mod[l], norm2[l].reshape(1, d), w_router_t, router_bias,
                 w_exp_gate, w_exp_up, w_exp_down, l, fgain, final=(l == cfg.depth - 1))
    y_p, y_s = x
    n_even = (cfg.depth + 1) // 2
    outs = [y_p.reshape(cfg.batch, cfg.seq, d), y_s.reshape(cfg.dec_batch, cfg.dec_seq, d)]
    for t in range(4):
        heads, hd = (cfg.a_heads, cfg.a_dim) if t < 2 else (cfg.b_heads, cfg.b_dim)
        stacked = jnp.stack([kv_cache[jj][t].reshape(cfg.batch, cfg.seq, heads, hd)
                             for jj in range(n_even)], axis=1)
        outs.append(stacked)
    return tuple(outs)


def kernel(x_prompt, x_sample, cache_a_k, cache_a_v, cache_b_k, cache_b_v, c, c_ctx, w_ada, b_ada, norm1, norm2, final_norm, w_in, w_mix_out, lambda_q1, lambda_k1, lambda_q2, lambda_k2, subln_gain, na_rel_bias, w_fourier_out, w_router, router_bias, w_exp_gate, w_exp_up, w_exp_down):
    return _forward(Cfg(), x_prompt, x_sample, cache_a_k, cache_a_v, cache_b_k, cache_b_v, c, c_ctx, w_ada,
                    b_ada, norm1, norm2, final_norm, w_in, w_mix_out, lambda_q1, lambda_k1, lambda_q2,
                    lambda_k2, subln_gain, na_rel_bias, w_fourier_out, w_router, router_bias, w_exp_gate,
                    w_exp_up, w_exp_down)
```

```python
import functools
import math
from typing import NamedTuple

import numpy as np
import jax
import jax.numpy as jnp
from jax import lax
from jax.experimental import pallas as pl
from jax.experimental.pallas import tpu as pltpu

F32 = jnp.float32
BF16 = jnp.bfloat16

LANES = 128
NEG_BIG = -1e30
VMEM_LIMIT = 56 * 1024 * 1024


class Cfg(NamedTuple):
    d_model: int = 2048
    batch: int = 16
    seq: int = 256
    depth: int = 2
    dec_batch: int = 8
    dec_seq: int = 2048
    past_len: int = 256
    grid_w: int = 64
    a_heads: int = 8
    a_qk: int = 64
    b_heads: int = 8
    b_dim: int = 128
    na_win_r: int = 8
    na_win_c: int = 16
    f_groups: int = 4
    n_experts: int = 16
    n_groups: int = 4
    d_expert: int = 1408
    rope_theta: float = 10000.0
    norm_eps: float = 1e-6
    subln_eps: float = 1e-5
    row_tile: int = 512
    proj_tile: int = 1024
    q_tile: int = 512
    e_block: int = 256

    @property
    def a_dim(self):
        return 2 * self.a_qk

    @property
    def a_width(self):
        return self.a_heads * self.a_dim

    @property
    def b_width(self):
        return self.b_heads * self.b_dim

    @property
    def n_p(self):
        return self.batch * self.seq

    @property
    def n_s(self):
        return self.dec_batch * self.dec_seq

    @property
    def n_tok(self):
        return self.n_p + self.n_s

    @property
    def tm(self):
        return min(self.row_tile, self.n_p, self.dec_seq)

    @property
    def n_mod(self):
        return 1 + self.dec_batch


def _params(sem):
    return pltpu.CompilerParams(dimension_semantics=sem, vmem_limit_bytes=VMEM_LIMIT)


def _norm_mod(x, g, sh, sc, eps):
    ms = jnp.mean(x * x, axis=-1, keepdims=True)
    y = x * lax.rsqrt(ms + eps)
    return (y * g) * (1.0 + sc) + sh


def _silu(x):
    return x * jax.nn.sigmoid(x)


def _merged_mod_index(cfg, tm):
    def f(i):
        r = i * tm
        return jnp.where(r < cfg.n_p, 0, 1 + (r - cfg.n_p) // cfg.dec_seq)
    return f


def _ada_kernel(cond_ref, w_ref, b_ref, o_ref):
    s = _silu(cond_ref[...])
    s_hi = s.astype(BF16)
    s_lo = (s - s_hi.astype(F32)).astype(BF16)
    lhs = jnp.concatenate([s_hi, s_lo], axis=0)
    r = jnp.dot(lhs, w_ref[0].astype(BF16), preferred_element_type=F32)
    n = s.shape[0]
    o_ref[0] = r[:n] + r[n:] + b_ref[0]


def _modulation(cfg, cond, w_ada, b_ada):
    d = cfg.d_model
    r = cond.shape[0]
    tn = math.gcd(1024, 6 * d)
    return pl.pallas_call(
        _ada_kernel,
        grid=(cfg.depth, 6 * d // tn),
        in_specs=[
            pl.BlockSpec((r, d), lambda l, j: (0, 0)),
            pl.BlockSpec((1, d, tn), lambda l, j: (l, 0, j)),
            pl.BlockSpec((1, 1, tn), lambda l, j: (l, 0, j)),
        ],
        out_specs=pl.BlockSpec((1, r, tn), lambda l, j: (l, 0, j)),
        out_shape=jax.ShapeDtypeStruct((cfg.depth, r, 6 * d), F32),
        compiler_params=_params(("arbitrary", "arbitrary")),
        name="ada_modulation",
    )(cond, w_ada, b_ada.reshape(cfg.depth, 1, 6 * d))


PROJ_CHUNK = 256


def _proj_kernel(*refs, rope, n_rope_blocks, eps):
    if rope:
        x_ref, mod_ref, g_ref, w_ref, cos_ref, sa_ref, sb_ref, o_ref, h_ref = refs
    else:
        x_ref, mod_ref, g_ref, w_ref, o_ref, h_ref = refs
    j = pl.program_id(1)

    @pl.when(j == 0)
    def _():
        h = _norm_mod(x_ref[...], g_ref[...], mod_ref[0, 0:1, :], mod_ref[0, 1:2, :], eps)
        h_ref[...] = h.astype(BF16)

    if not rope:
        o_ref[...] = jnp.dot(h_ref[...], w_ref[...], preferred_element_type=F32).astype(o_ref.dtype)
        return

    @pl.when(j < n_rope_blocks)
    def _():
        cos, sa, sb = cos_ref[...], sa_ref[...], sb_ref[...]
        h = h_ref[...]
        tn = w_ref.shape[1]
        cw = min(PROJ_CHUNK, tn)
        for c0 in range(0, tn, cw):
            acc = jnp.dot(h, w_ref[:, c0:c0 + cw], preferred_element_type=F32)
            for c in range(cw // LANES):
                xa = acc[:, c * LANES:(c + 1) * LANES]
                up = pltpu.roll(xa, LANES - 16, 1)
                dn = pltpu.roll(xa, 16, 1)
                lo = c0 + c * LANES
                o_ref[:, lo:lo + LANES] = (xa * cos + up * sa + dn * sb).astype(o_ref.dtype)

    @pl.when(j >= n_rope_blocks)
    def _():
        o_ref[...] = jnp.dot(h_ref[...], w_ref[...], preferred_element_type=F32).astype(o_ref.dtype)


def _rope_tables(cfg):
    t = np.arange(cfg.dec_seq)
    row = (t // cfg.grid_w).astype(np.float64)
    col = (t % cfg.grid_w).astype(np.float64)
    lane = np.arange(LANES)
    l64 = lane % cfg.a_qk
    half = cfg.a_qk // 4
    freq = cfg.rope_theta ** (-(lane % half).astype(np.float64) / half)
    pos = np.where((l64 < cfg.a_qk // 2)[None, :], row[:, None], col[:, None])
    ang = pos * freq[None, :]
    first = (lane % (2 * half)) < half
    cos = np.cos(ang)
    sin = np.sin(ang)
    sa = np.where(first[None, :], -sin, 0.0)
    sb = np.where(first[None, :], 0.0, sin)
    return tuple(jnp.asarray(a, dtype=F32) for a in (cos, sa, sb))


def _projection(cfg, x, mod_l, gain, w, *, latent, out_dtype):
    m, d = x.shape
    n = w.shape[1]
    tm = min(cfg.proj_tile, m, cfg.dec_seq)
    tn = min(1024, cfg.a_width)
    per_seq = cfg.dec_seq // tm if latent else 1
    mod_idx = (lambda i, j: (1 + i // per_seq, 0, 0)) if latent else (lambda i, j: (0, 0, 0))
    in_specs = [
        pl.BlockSpec((tm, d), lambda i, j: (i, 0)),
        pl.BlockSpec((1, 8, d), mod_idx),
        pl.BlockSpec((1, d), lambda i, j: (0, 0)),
        pl.BlockSpec((d, tn), lambda i, j: (0, j)),
    ]
    args = [x, mod_l, gain, w]
    if latent:
        tab_spec = pl.BlockSpec((tm, LANES), lambda i, j: (i % per_seq, 0))
        in_specs += [tab_spec, tab_spec, tab_spec]
        args += list(_rope_tables(cfg))
    kern = functools.partial(_proj_kernel, rope=latent, n_rope_blocks=2 * cfg.a_width // tn,
                             eps=cfg.norm_eps)
    return pl.pallas_call(
        kern,
        grid=(m // tm, n // tn),
        in_specs=in_specs,
        out_specs=pl.BlockSpec((tm, tn), lambda i, j: (i, j)),
        out_shape=jax.ShapeDtypeStruct((m, n), out_dtype),
        scratch_shapes=[pltpu.VMEM((tm, d), BF16)],
        compiler_params=_params(("arbitrary", "arbitrary")),
        name="qkv_projection_latent" if latent else "qkv_projection_context",
    )(*args)


def _nt_dot(a, b):
    return lax.dot_general(a, b, (((1,), (1,)), ((), ())), preferred_element_type=F32)


DIFF_SUB_TILE = 128


def _diff_attn_kernel(*refs, has_ctx, lam_init, eps, qk):
    if has_ctx:
        lam_ref, gain_ref, q_ref, k_ref, v_ref, ck_ref, cv_ref, o_ref = refs
    else:
        lam_ref, gain_ref, q_ref, k_ref, v_ref, o_ref, ko_ref, vo_ref = refs
        ko_ref[...] = k_ref[...]
        vo_ref[...] = v_ref[...]
    lv = lam_ref[...]
    s1 = jnp.sum(lv[0:1] * lv[1:2], axis=-1, keepdims=True)
    s2 = jnp.sum(lv[2:3] * lv[3:4], axis=-1, keepdims=True)
    lam = jnp.exp(s1) - jnp.exp(s2) + lam_init

    hd = 2 * qk
    tq = q_ref.shape[0]
    ts = min(tq, DIFF_SUB_TILE)
    for hh, t in [(hh, t) for hh in range(q_ref.shape[1] // hd) for t in range(tq // ts)]:
        cols = slice(hh * hd, (hh + 1) * hd)
        k = k_ref[:, cols].astype(BF16)
        v = v_ref[:, cols].astype(BF16)
        if has_ctx:
            ck = ck_ref[0, :, cols].astype(BF16)
            cv = cv_ref[0, :, cols].astype(BF16)
        q = q_ref[t * ts:(t + 1) * ts, cols].astype(F32) * (qk ** -0.5 * math.log2(math.e))
        lane = lax.broadcasted_iota(jnp.int32, q.shape, 1)
        qs = jnp.concatenate([jnp.where(lane < qk, q, 0.0), jnp.where(lane >= qk, q, 0.0)],
                             axis=0).astype(BF16)
        s_new = _nt_dot(qs, k)
        m = jnp.max(s_new, axis=-1, keepdims=True)
        if has_ctx:
            s_ctx = _nt_dot(qs, ck)
            m = jnp.maximum(m, jnp.max(s_ctx, axis=-1, keepdims=True))
        p_new = jnp.exp2(s_new - m)
        den = jnp.sum(p_new, axis=-1, keepdims=True)
        o2 = jnp.dot(p_new.astype(BF16), v, preferred_element_type=F32)
        if has_ctx:
            p_ctx = jnp.exp2(s_ctx - m)
            den = den + jnp.sum(p_ctx, axis=-1, keepdims=True)
            o2 = o2 + jnp.dot(p_ctx.astype(BF16), cv, preferred_element_type=F32)
        inv = 1.0 / den
        o = o2[:ts] * inv[:ts] - o2[ts:] * (lam * inv[ts:])
        ms = jnp.mean(o * o, axis=-1, keepdims=True)
        o = (o * lax.rsqrt(ms + eps)) * gain_ref[...]
        o_ref[t * ts:(t + 1) * ts, cols] = (o * (1.0 - lam_init)).astype(o_ref.dtype)


def _diff_attention(cfg, proj, lam_pack, gain, n_batch, seq_len, lam_init, ctx=None):
    hd = cfg.a_dim
    hps = cfg.a_heads if ctx is None else 1
    nh = cfg.a_heads // hps
    bw = hps * hd
    tq = min(cfg.q_tile, seq_len)
    nq = seq_len // tq
    in_specs = [
        pl.BlockSpec((8, LANES), lambda b, h, qi: (0, 0)),
        pl.BlockSpec((1, hd), lambda b, h, qi: (0, 0)),
        pl.BlockSpec((tq, bw), lambda b, h, qi: (b * nq + qi, h)),
        pl.BlockSpec((seq_len, bw), lambda b, h, qi: (b, nh + h)),
        pl.BlockSpec((seq_len, bw), lambda b, h, qi: (b, 2 * nh + h)),
    ]
    args = [lam_pack, gain, proj, proj, proj]
    if ctx is not None:
        ck, cv = ctx
        past = ck.shape[1]
        cspec = pl.BlockSpec((1, past, bw), lambda b, h, qi: (b, 0, h))
        in_specs += [cspec, cspec]
        args += [ck, cv]
    kern = functools.partial(_diff_attn_kernel, has_ctx=ctx is not None, lam_init=lam_init,
                             eps=cfg.subln_eps, qk=cfg.a_qk)
    out_specs = pl.BlockSpec((tq, bw), lambda b, h, qi: (b * nq + qi, h))
    out_shape = jax.ShapeDtypeStruct((n_batch * seq_len, cfg.a_width), BF16)
    if ctx is None:
        assert nq == 1 and proj.dtype == F32
        kv_spec = pl.BlockSpec((seq_len, bw), lambda b, h, qi: (b, h))
        kv_shape = jax.ShapeDtypeStruct((n_batch * seq_len, cfg.a_width), F32)
        out_specs, out_shape = [out_specs, kv_spec, kv_spec], [out_shape, kv_shape, kv_shape]
    return pl.pallas_call(
        kern,
        grid=(n_batch, nh, nq),
        in_specs=in_specs,
        out_specs=out_specs,
        out_shape=out_shape,
        compiler_params=_params(("arbitrary", "arbitrary", "arbitrary")),
        name="diff_attention_latent" if ctx is not None else "diff_attention_context",
    )(*args)


def _soft_attn_kernel(q_ref, k_ref, v_ref, o_ref, ko_ref, vo_ref, *, hd):
    ko_ref[...] = k_ref[...]
    vo_ref[...] = v_ref[...]
    for hh in range(q_ref.shape[1] // hd):
        cols = slice(hh * hd, (hh + 1) * hd)
        s = _nt_dot(q_ref[:, cols].astype(BF16), k_ref[:, cols].astype(BF16)) * (hd ** -0.5 * math.log2(math.e))
        m = jnp.max(s, axis=-1, keepdims=True)
        p = jnp.exp2(s - m)
        inv = 1.0 / jnp.sum(p, axis=-1, keepdims=True)
        o = jnp.dot(p.astype(BF16), v_ref[:, cols].astype(BF16), preferred_element_type=F32) * inv
        o_ref[:, cols] = o.astype(o_ref.dtype)


def _soft_attention(cfg, proj, n_batch, seq_len):
    bw = cfg.b_width
    assert (3 * cfg.a_width) % bw == 0
    base = 3 * cfg.a_width // bw
    return pl.pallas_call(
        functools.partial(_soft_attn_kernel, hd=cfg.b_dim),
        grid=(n_batch,),
        in_specs=[
            pl.BlockSpec((seq_len, bw), lambda b: (b, base)),
            pl.BlockSpec((seq_len, bw), lambda b: (b, base + 1)),
            pl.BlockSpec((seq_len, bw), lambda b: (b, base + 2)),
        ],
        out_specs=[pl.BlockSpec((seq_len, bw), lambda b: (b, 0))] * 3,
        out_shape=[jax.ShapeDtypeStruct((n_batch * seq_len, bw), BF16),
                   jax.ShapeDtypeStruct((n_batch * seq_len, bw), proj.dtype),
                   jax.ShapeDtypeStruct((n_batch * seq_len, bw), proj.dtype)],
        compiler_params=_params(("arbitrary",)),
        name="softmax_attention_context",
    )(proj, proj, proj)


NA_GROUP_ROWS = 4


def _na_geometry(cfg):
    rows = cfg.dec_seq // cfg.grid_w
    kr = min(cfg.na_win_r, rows)
    grp = min(NA_GROUP_ROWS, rows)
    union = min(rows, kr + grp - 1 + (kr + grp - 1) % 2)
    starts = []
    for g in range(rows // grp):
        rs0 = min(max(g * grp - kr // 2, 0), rows - kr)
        starts.append(min(rs0, rows - union))
    return rows, kr, grp, union, starts


def _na_kernel(ws_ref, plane_ref, q_ref, k_ref, v_ref, ck_ref, cv_ref, tab_ref, o_ref, sctx, pctx, oacc, *,
               n_groups, grp, union, gw, scale):
    gq, uk = grp * gw, union * gw
    ck = ck_ref[0].astype(BF16)
    cv = cv_ref[0].astype(BF16)
    sctx[...] = _nt_dot(q_ref[...], ck) * scale
    first_half = lax.broadcasted_iota(jnp.int32, (gw, LANES), 1) < gw

    def body(g, carry):
        q0 = pl.multiple_of(g * gq, gq)
        k0 = pl.multiple_of(ws_ref[g] * gw, gw)
        q = q_ref[pl.ds(q0, gq), :]
        bias_rows = []
        for a in range(grp):
            base = (g * grp + a) * union
            tiles = [jnp.where(first_half, tab_ref[0, plane_ref[base + 2 * jj]],
                               tab_ref[0, plane_ref[base + 2 * jj + 1]]) for jj in range(union // 2)]
            bias_rows.append(jnp.concatenate(tiles, axis=1))
        bias = jnp.concatenate(bias_rows, axis=0)
        s = _nt_dot(q, k_ref[pl.ds(k0, uk), :]) * scale + bias
        sc = sctx[pl.ds(q0, gq), :]
        m = jnp.maximum(jnp.max(s, axis=-1, keepdims=True), jnp.max(sc, axis=-1, keepdims=True))
        p = jnp.exp2(s - m)
        pc = jnp.exp2(sc - m)
        inv = 1.0 / (jnp.sum(p, axis=-1, keepdims=True) + jnp.sum(pc, axis=-1, keepdims=True))
        oacc[pl.ds(q0, gq), :] = jnp.dot(p.astype(BF16), v_ref[pl.ds(k0, uk), :],
                                         preferred_element_type=F32) * inv
        pctx[pl.ds(q0, gq), :] = (pc * inv).astype(BF16)
        return carry

    lax.fori_loop(0, n_groups, body, 0, unroll=2)
    o_ref[...] = (oacc[...] + jnp.dot(pctx[...], cv, preferred_element_type=F32)).astype(o_ref.dtype)


def _na_bias_table(cfg, rpb):
    w = cfg.grid_w
    assert LANES == 2 * w
    rows, kr, grp, union, starts = _na_geometry(cfg)
    qc = np.arange(w)
    kc = np.arange(w)
    cs = np.clip(qc - cfg.na_win_c // 2, 0, w - cfg.na_win_c)
    col_mask = (kc[None, :] >= cs[:, None]) & (kc[None, :] < cs[:, None] + cfg.na_win_c)
    col_idx = np.clip(kc[None, :] - qc[:, None] + cfg.na_win_c - 1, 0, 2 * cfg.na_win_c - 2)
    n_c = 2 * cfg.na_win_c - 1
    onehot = (col_idx[None] == np.arange(n_c)[:, None, None]) & col_mask[None]
    t = jnp.einsum('hrc,cqk->hrqk', rpb.astype(F32), jnp.asarray(onehot, dtype=F32),
                   precision=lax.Precision.HIGHEST)
    t = jnp.where(jnp.asarray(col_mask)[None, None], t * math.log2(math.e), NEG_BIG)
    n_r = 2 * cfg.na_win_r - 1
    t = jnp.concatenate([t, jnp.full((cfg.b_heads, 1, w, w), NEG_BIG, F32)], axis=1)
    plane = np.full((len(starts), grp, union), n_r, np.int32)
    for g, ws in enumerate(starts):
        for a in range(grp):
            r = g * grp + a
            rs = min(max(r - kr // 2, 0), rows - kr)
            for j in range(union):
                if rs <= ws + j < rs + kr:
                    plane[g, a, j] = ws + j - r + cfg.na_win_r - 1
    return jnp.concatenate([t, t], axis=-1), jnp.asarray(plane.reshape(-1))


def _na_attention(cfg, proj, ck, cv, rpb):
    hd = cfg.b_dim
    nh = cfg.b_heads
    n = cfg.dec_seq
    rows, kr, grp, union, starts = _na_geometry(cfg)
    n_groups = len(starts)
    base = 3 * cfg.a_width // hd
    past = ck.shape[1]
    table, plane = _na_bias_table(cfg, rpb)
    kern = functools.partial(_na_kernel, n_groups=n_groups, grp=grp, union=union, gw=cfg.grid_w,
                             scale=hd ** -0.5 * math.log2(math.e))
    cspec = pl.BlockSpec((1, past, hd), lambda h, b, *_: (b, 0, h))
    return pl.pallas_call(
        kern,
        grid_spec=pltpu.PrefetchScalarGridSpec(
            num_scalar_prefetch=2,
            grid=(nh, cfg.dec_batch),
            in_specs=[
                pl.BlockSpec((n, hd), lambda h, b, *_: (b, base + h)),
                pl.BlockSpec((n, hd), lambda h, b, *_: (b, base + nh + h)),
                pl.BlockSpec((n, hd), lambda h, b, *_: (b, base + 2 * nh + h)),
                cspec, cspec,
                pl.BlockSpec((1,) + table.shape[1:], lambda h, b, *_: (h, 0, 0, 0)),
            ],
            out_specs=pl.BlockSpec((n, hd), lambda h, b, *_: (b, h)),
            scratch_shapes=[pltpu.VMEM((n, past), F32), pltpu.VMEM((n, past), BF16), pltpu.VMEM((n, hd), F32)],
        ),
        out_shape=jax.ShapeDtypeStruct((cfg.n_s, cfg.b_width), BF16),
        compiler_params=_params(("arbitrary", "arbitrary")),
        name="neighbourhood_attention",
    )(jnp.asarray(np.asarray(starts, np.int32)), plane, proj, proj, proj, ck, cv, table)


def _linres_kernel(*refs, n_parts, n_pb, x_split):
    i = pl.program_id(0)
    is_p = i < n_pb
    pos = 0
    acc = None
    w_ref = refs[2 * n_parts]
    k0 = 0
    for p in range(n_parts):
        a_p, a_s = refs[2 * p], refs[2 * p + 1]
        a = jnp.where(is_p, a_p[...], a_s[...])
        kk = a.shape[1]
        part = jnp.dot(a, w_ref[k0:k0 + kk, :], preferred_element_type=F32)
        acc = part if acc is None else acc + part
        k0 += kk
    pos = 2 * n_parts + 1
    if x_split:
        x = jnp.where(is_p, refs[pos][...], refs[pos + 1][...])
        pos += 2
    else:
        x = refs[pos][...]
        pos += 1
    mod_ref, o_ref = refs[pos], refs[pos + 1]
    o_ref[...] = x + mod_ref[0, 2:3, :] * acc


def _linear_residual(cfg, parts, w, x, mod_l):
    d = cfg.d_model
    tm = min(256, cfg.tm)
    n_pb = cfg.n_p // tm
    n_sb = cfg.n_s // tm
    p_idx = lambda i: (jnp.minimum(i, n_pb - 1), 0)
    s_idx = lambda i: (jnp.maximum(i - n_pb, 0), 0)
    in_specs, args = [], []
    for a_p, a_s in parts:
        kk = a_p.shape[1]
        in_specs += [pl.BlockSpec((tm, kk), p_idx), pl.BlockSpec((tm, kk), s_idx)]
        args += [a_p, a_s]
    in_specs.append(pl.BlockSpec(w.shape, lambda i: (0, 0)))
    args.append(w)
    x_split = isinstance(x, tuple)
    if x_split:
        in_specs += [pl.BlockSpec((tm, d), p_idx), pl.BlockSpec((tm, d), s_idx)]
        args += list(x)
    else:
        in_specs.append(pl.BlockSpec((tm, d), lambda i: (i, 0)))
        args.append(x)
    mi = _merged_mod_index(cfg, tm)
    in_specs.append(pl.BlockSpec((1, 8, d), lambda i: (mi(i), 0, 0)))
    args.append(mod_l)
    kern = functools.partial(_linres_kernel, n_parts=len(parts), n_pb=n_pb, x_split=x_split)
    return pl.pallas_call(
        kern,
        grid=(n_pb + n_sb,),
        in_specs=in_specs,
        out_specs=pl.BlockSpec((tm, d), lambda i: (i, 0)),
        out_shape=jax.ShapeDtypeStruct((cfg.n_tok, d), F32),
        compiler_params=_params(("arbitrary",)),
        name="linear_gated_residual",
    )(*args)


def _dft_chan_kernel(x_ref, mod_ref, g_ref, cs_ref, y_ref, *, groups, eps):
    h = _norm_mod(x_ref[...], g_ref[...], mod_ref[0, 0:1, :], mod_ref[0, 1:2, :], eps).astype(BF16)
    gd = h.shape[1] // groups
    for g in range(groups):
        r = jnp.dot(h[:, g * gd:(g + 1) * gd], cs_ref[...], preferred_element_type=F32)
        y_ref[0, :, g * gd:(g + 1) * gd] = r[:, :gd].astype(BF16)
        y_ref[1, :, g * gd:(g + 1) * gd] = r[:, gd:].astype(BF16)


def _dft_mats(n):
    k = np.arange(n)
    ang = 2.0 * np.pi * ((k[:, None] * k[None, :]) % n) / n
    return np.cos(ang), np.sin(ang)


def _dft_channels(cfg, x, mod_l, gain):
    d = cfg.d_model
    gd = d // cfg.f_groups
    tm = cfg.tm
    c, s = _dft_mats(gd)
    cs = jnp.asarray(np.concatenate([c, s], axis=1), dtype=F32).astype(BF16)
    mi = _merged_mod_index(cfg, tm)
    return pl.pallas_call(
        functools.partial(_dft_chan_kernel, groups=cfg.f_groups, eps=cfg.norm_eps),
        grid=(cfg.n_tok // tm,),
        in_specs=[
            pl.BlockSpec((tm, d), lambda i: (i, 0)),
            pl.BlockSpec((1, 8, d), lambda i: (mi(i), 0, 0)),
            pl.BlockSpec((1, d), lambda i: (0, 0)),
            pl.BlockSpec((gd, 2 * gd), lambda i: (0, 0)),
        ],
        out_specs=pl.BlockSpec((2, tm, d), lambda i: (0, i, 0)),
        out_shape=jax.ShapeDtypeStruct((2, cfg.n_tok, d), BF16),
        compiler_params=_params(("arbitrary",)),
        name="dft_channels",
    )(x, mod_l, gain, cs)


def _dft_seq_kernel(w_ref, y_ref, o_ref, *, scale):
    acc = (jnp.dot(w_ref[0], y_ref[0], preferred_element_type=F32)
           + jnp.dot(w_ref[1], y_ref[1], preferred_element_type=F32))
    o_ref[...] = (acc * scale).astype(o_ref.dtype)


def _dft_sequence(cfg, y, n_batch, seq_len, first_block):
    d = cfg.d_model
    c, s = _dft_mats(seq_len)
    wm = jnp.asarray(np.stack([c, -s]), dtype=F32).astype(BF16)
    tml = min(1024, seq_len)
    tn = min(512, d)
    nm = seq_len // tml
    scale = 1.0 / math.sqrt(seq_len * (d // cfg.f_groups))
    return pl.pallas_call(
        functools.partial(_dft_seq_kernel, scale=scale),
        grid=(n_batch, nm, d // tn),
        in_specs=[
            pl.BlockSpec((2, tml, seq_len), lambda b, mi, j: (0, mi, 0)),
            pl.BlockSpec((2, seq_len, tn), lambda b, mi, j: (0, first_block + b, j)),
        ],
        out_specs=pl.BlockSpec((tml, tn), lambda b, mi, j: (b * nm + mi, j)),
        out_shape=jax.ShapeDtypeStruct((n_batch * seq_len, d), BF16),
        compiler_params=_params(("arbitrary", "arbitrary", "arbitrary")),
        name="dft_sequence_%d" % seq_len,
    )(wm, y)


TOKEN_TILE_ROWS = 8


def _token_tile(d):
    rows = min(TOKEN_TILE_ROWS, d // LANES)
    return rows, d // (rows * LANES)


def _tile_chunk(c, n_tok, rows, first=0):
    return c // rows, pl.ds(first + c % rows, n_tok, stride=rows), slice(None)


def _tile_tokens(ref, first_row, n_rows):
    return ref.at[:, pl.ds(first_row, n_rows), :]


def _route_kernel(x_ref, mod_ref, g_ref, wr_ref, rb_ref, hp_ref, idx_ref, wt_ref, *, eps, n_exp, per_grp):
    h = _norm_mod(x_ref[...], g_ref[...], mod_ref[0, 3:4, :], mod_ref[0, 4:5, :], eps)
    tm, d = h.shape
    rt, _ = _token_tile(d)
    for c in range(d // LANES):
        hp_ref[_tile_chunk(c, tm, rt)] = h[:, c * LANES:(c + 1) * LANES]

    h_hi = h.astype(BF16)
    h_lo = (h - h_hi.astype(F32)).astype(BF16)
    w = wr_ref[...]
    w_hi = w.astype(BF16)
    w_lo = (w - w_hi.astype(F32)).astype(BF16)
    logits = _nt_dot(w_hi, h_hi) + (_nt_dot(w_hi, h_lo) + _nt_dot(w_lo, h_hi))
    scores = jax.nn.sigmoid(logits)
    sel = scores + rb_ref[...]
    n_grp = n_exp // per_grp
    best = None
    gi = None
    for g in range(n_grp):
        v = [sel[g * per_grp + k:g * per_grp + k + 1, :] for k in range(per_grp)]
        gs = None
        for a in range(per_grp):
            for b in range(a + 1, per_grp):
                ps = v[a] + v[b]
                gs = ps if gs is None else jnp.maximum(gs, ps)
        if best is None:
            best, gi = gs, jnp.zeros(gs.shape, jnp.int32)
        else:
            better = gs > best
            gi = jnp.where(better, g, gi)
            best = jnp.where(better, gs, best)
    row = lax.broadcasted_iota(jnp.int32, sel.shape, 0)
    masked = jnp.where(row // per_grp == gi, sel, -jnp.inf)
    m1 = jnp.max(masked, axis=0, keepdims=True)
    i1 = jnp.min(jnp.where(masked == m1, row, n_exp), axis=0, keepdims=True)
    masked2 = jnp.where(row == i1, -jnp.inf, masked)
    m2 = jnp.max(masked2, axis=0, keepdims=True)
    i2 = jnp.min(jnp.where(masked2 == m2, row, n_exp), axis=0, keepdims=True)
    w1 = jnp.sum(jnp.where(row == i1, scores, 0.0), axis=0, keepdims=True)
    w2 = jnp.sum(jnp.where(row == i2, scores, 0.0), axis=0, keepdims=True)
    inv = 1.0 / (w1 + w2)
    idx_ref[...] = jnp.concatenate([i1, i2], axis=0)
    wt_ref[...] = jnp.concatenate([w1 * inv, w2 * inv], axis=0)


def _route(cfg, x, mod_l, gain, w_router_t, router_bias):
    d = cfg.d_model
    tm = cfg.tm
    t = cfg.n_tok
    rt, npl = _token_tile(d)
    mi = _merged_mod_index(cfg, tm)
    kern = functools.partial(_route_kernel, eps=cfg.norm_eps, n_exp=cfg.n_experts,
                             per_grp=cfg.n_experts // cfg.n_groups)
    return pl.pallas_call(
        kern,
        grid=(t // tm,),
        in_specs=[
            pl.BlockSpec((tm, d), lambda i: (i, 0)),
            pl.BlockSpec((1, 8, d), lambda i: (mi(i), 0, 0)),
            pl.BlockSpec((1, d), lambda i: (0, 0)),
            pl.BlockSpec((cfg.n_experts, d), lambda i: (0, 0)),
            pl.BlockSpec((cfg.n_experts, 1), lambda i: (0, 0)),
        ],
        out_specs=[
            pl.BlockSpec((npl, tm * rt, LANES), lambda i: (0, i, 0)),
            pl.BlockSpec((2, tm), lambda i: (0, i)),
            pl.BlockSpec((2, tm), lambda i: (0, i)),
        ],
        out_shape=[
            jax.ShapeDtypeStruct((npl, t * rt, LANES), F32),
            jax.ShapeDtypeStruct((2, t), jnp.int32),
            jax.ShapeDtypeStruct((2, t), F32),
        ],
        compiler_params=_params(("arbitrary",)),
        name="moe_route",
    )(x, mod_l, gain, w_router_t, router_bias.reshape(cfg.n_experts, 1))


def _plan(cfg, idx):
    t = cfg.n_tok
    eb = cfg.e_block
    ne = cfg.n_experts
    n_blocks = -(-(2 * t + ne * (eb - 1)) // eb)
    e_flat = idx.reshape(-1)
    onehot = (e_flat[:, None] == jnp.arange(ne, dtype=jnp.int32)[None, :]).astype(jnp.int32)
    csum = jnp.cumsum(onehot, axis=0)
    rank = jnp.sum(onehot * (csum - 1), axis=1)
    counts = csum[-1]
    padded = (counts + eb - 1) // eb * eb
    pad_end = jnp.cumsum(padded)
    pad_start = pad_end - padded
    pos = (pad_start[e_flat] + rank).astype(jnp.int32).reshape(2, t)
    n_used = (pad_end[-1] // eb).astype(jnp.int32)
    blk = jnp.arange(n_blocks, dtype=jnp.int32)
    blk = jnp.minimum(blk, n_used - 1)
    first_e = jnp.sum((pad_end[None, :] <= (blk * eb)[:, None]).astype(jnp.int32), axis=1)
    block_e = jnp.minimum(first_e, ne - 1).astype(jnp.int32)
    return pos, block_e, n_used.reshape(1), pad_end.astype(jnp.int32), n_blocks


def _weight_schedule(cfg, block_e, n_used, n_blocks):
    n_chunks = _expert_chunks(cfg)[2]
    big = n_blocks + 1
    idx = jnp.arange(n_blocks, dtype=jnp.int32)
    valid = idx < n_used[0]
    prev_e = jnp.concatenate([jnp.full((1,), -1, jnp.int32), block_e[:-1]])
    change = (block_e != prev_e) & valid
    earlier = (idx[None, :] <= idx[:, None]) & change[None, :]
    later = (idx[None, :] > idx[:, None]) & change[None, :]
    run_start = jnp.max(jnp.where(earlier, idx[None, :], 0), axis=1)
    next_start = jnp.minimum(jnp.min(jnp.where(later, idx[None, :], big), axis=1), n_used[0])
    slot = (jnp.sum(earlier.astype(jnp.int32), axis=1) - 1) % 2
    has_next = valid & (next_start < n_used[0])
    next_e = jnp.where(has_next, block_e[jnp.minimum(next_start, n_blocks - 1)], 0)
    run_len = jnp.maximum(next_start - run_start, 1)
    per_block = (n_chunks + run_len - 1) // run_len
    i = idx - run_start
    c0 = jnp.where(has_next, jnp.minimum(i * per_block, n_chunks), 0)
    c1 = jnp.where(has_next, jnp.minimum((i + 1) * per_block, n_chunks), 0)
    return tuple(a.astype(jnp.int32) for a in (slot, next_e, c0, c1))


def _dispatch_kernel(pend_ref, pos_ref, hp_ref, xr_ref, zbuf, sem, zsem, *, n_exp, eb, spt):
    i = pl.program_id(0)

    @pl.when(i == 0)
    def _():
        zbuf[...] = jnp.zeros(zbuf.shape, zbuf.dtype)
        for e in range(n_exp):
            start = pl.multiple_of(jnp.maximum(pend_ref[e] - eb, 0) * spt, eb * spt)
            cp = pltpu.make_async_copy(zbuf, _tile_tokens(xr_ref, start, eb * spt), zsem)
            cp.start()
            cp.wait()
        n_used = pend_ref[n_exp - 1] // eb
        n_blocks = xr_ref.shape[1] // (eb * spt)
        for e in range(n_exp):
            @pl.when(n_used + e < n_blocks)
            def _():
                start = pl.multiple_of((n_used + e) * (eb * spt), eb * spt)
                cp = pltpu.make_async_copy(zbuf, _tile_tokens(xr_ref, start, eb * spt), zsem)
                cp.start()
                cp.wait()

    rows = pos_ref.shape[2]

    def start(r, c):
        src = _tile_tokens(hp_ref, pl.multiple_of(r * spt, spt), spt)
        for k in range(2):
            dst = _tile_tokens(xr_ref, pl.multiple_of(pos_ref[0, k, r] * spt, spt), spt)
            pltpu.make_async_copy(src, dst, sem.at[k]).start(priority=k)
        return c

    lax.fori_loop(0, rows, start, 0, unroll=8)
    for k in range(2):
        pltpu.make_async_copy(hp_ref, _tile_tokens(xr_ref, 0, rows * spt), sem.at[k]).wait()


def _dispatch(cfg, hp, pos_blocks, pad_end, n_rows):
    tb = pos_blocks.shape[2]
    spt, npl = _token_tile(cfg.d_model)
    kern = functools.partial(_dispatch_kernel, n_exp=cfg.n_experts, eb=cfg.e_block, spt=spt)
    return pl.pallas_call(
        kern,
        grid_spec=pltpu.PrefetchScalarGridSpec(
            num_scalar_prefetch=1,
            grid=(cfg.n_tok // tb,),
            in_specs=[
                pl.BlockSpec((1, 2, tb), lambda i, pe: (i, 0, 0), memory_space=pltpu.SMEM),
                pl.BlockSpec((npl, tb * spt, LANES), lambda i, pe: (0, i, 0)),
            ],
            out_specs=pl.BlockSpec(memory_space=pl.ANY),
            scratch_shapes=[
                pltpu.VMEM((npl, cfg.e_block * spt, LANES), F32),
                pltpu.SemaphoreType.DMA((2,)),
                pltpu.SemaphoreType.DMA(()),
            ],
        ),
        out_shape=jax.ShapeDtypeStruct((npl, n_rows * spt, LANES), F32),
        compiler_params=_params(("arbitrary",)),
        name="moe_dispatch",
    )(pad_end, pos_blocks, hp)


EXPERT_CHUNK_ROWS = 128
EXPERT_RING = 6
EXPERT_VMEM_LIMIT = 60 * 1024 * 1024


def _expert_chunks(cfg):
    n_in = cfg.d_model // EXPERT_CHUNK_ROWS
    n_out = cfg.d_expert // EXPERT_CHUNK_ROWS
    return n_in, n_out, 2 * n_in + n_out


def _expert_kernel(nu_ref, e0_ref, slot_ref, ne_ref, c0_ref, c1_ref, xp_ref, wg_ref, wu_ref, wd_ref, y_ref,
                   wg_b, wu_b, wd_b, stg, sem, *, n_in, n_out, layer):
    b = pl.program_id(0)
    n_chunks = 2 * n_in + n_out
    ring = stg.shape[0]
    rows = EXPERT_CHUNK_ROWS

    parts = [(0, n_in, wg_ref, wg_b), (n_in, n_in, wu_ref, wu_b), (2 * n_in, n_out, wd_ref, wd_b)]

    def for_chunk(c, e, fn):
        s = c % ring
        for first, count, src, dst in parts:
            @pl.when((c >= first) & (c < first + count))
            def _():
                row = pl.multiple_of((c - first) * rows, rows)
                window = stg.at[s, :, pl.ds(0, dst.shape[2])]
                cp = pltpu.make_async_copy(src.at[layer, e, pl.ds(row, rows)], window, sem.at[s])
                fn(cp, dst, row, window)

    def start_chunk(c, e):
        for_chunk(c, e, lambda cp, dst, row, window: cp.start())

    def finish_chunk(c, e, dst_slot):
        def cast(cp, dst, row, window):
            cp.wait()
            dst[dst_slot, pl.ds(row, rows), :] = window[...].astype(BF16)
        for_chunk(c, e, cast)

    def load_chunks(e, dst_slot, lo, hi):
        @pl.when((lo == 0) & (hi > 0))
        def _():
            for k in range(min(ring, n_chunks)):
                start_chunk(k, e)

        def body(c, carry):
            finish_chunk(c, e, dst_slot)

            @pl.when(c + ring < n_chunks)
            def _():
                start_chunk(c + ring, e)
            return carry

        lax.fori_loop(lo, hi, body, 0)

    @pl.when(b == 0)
    def _():
        load_chunks(e0_ref[0], 0, 0, n_chunks)

    @pl.when(b < nu_ref[0])
    def _():
        slot = slot_ref[b]
        load_chunks(ne_ref[b], 1 - slot, c0_ref[b], c1_ref[b])
        d = wg_b.shape[1]
        n_chunk = d // LANES
        rt, _ = _token_tile(d)
        eb = xp_ref.shape[1] // rt
        x = jnp.concatenate([xp_ref[_tile_chunk(c, eb, rt)].astype(BF16) for c in range(n_chunk)], axis=1)
        g = jnp.dot(x, wg_b[slot], preferred_element_type=F32)
        u = jnp.dot(x, wu_b[slot], preferred_element_type=F32)
        a = (_silu(g) * u).astype(BF16)
        y = jnp.dot(a, wd_b[slot], preferred_element_type=F32)
        for c in range(n_chunk):
            y_ref[_tile_chunk(c, eb, rt)] = y[:, c * LANES:(c + 1) * LANES]

    @pl.when(b >= nu_ref[0])
    def _():
        y_ref[...] = jnp.zeros(y_ref.shape, y_ref.dtype)


def _experts(cfg, x_rows, block_e, n_used, wg, wu, wd, layer, n_blocks):
    d = cfg.d_model
    f = cfg.d_expert
    eb = cfg.e_block
    rt, npl = _token_tile(d)
    n_in, n_out, _ = _expert_chunks(cfg)
    slot, next_e, c0, c1 = _weight_schedule(cfg, block_e, n_used, n_blocks)
    any_spec = pl.BlockSpec(memory_space=pl.ANY)
    x_idx = lambda b, nu, *_: (0, jnp.minimum(b, nu[0] - 1), 0)
    return pl.pallas_call(
        functools.partial(_expert_kernel, n_in=n_in, n_out=n_out, layer=layer),
        grid_spec=pltpu.PrefetchScalarGridSpec(
            num_scalar_prefetch=6,
            grid=(n_blocks,),
            in_specs=[pl.BlockSpec((npl, eb * rt, LANES), x_idx), any_spec, any_spec, any_spec],
            out_specs=pl.BlockSpec((npl, eb * rt, LANES), lambda b, *_: (0, b, 0)),
            scratch_shapes=[
                pltpu.VMEM((2, d, f), BF16), pltpu.VMEM((2, d, f), BF16), pltpu.VMEM((2, f, d), BF16),
                pltpu.VMEM((EXPERT_RING, EXPERT_CHUNK_ROWS, max(d, f)), F32),
                pltpu.SemaphoreType.DMA((EXPERT_RING,)),
            ],
        ),
        out_shape=jax.ShapeDtypeStruct((npl, n_blocks * eb * rt, LANES), F32),
        compiler_params=pltpu.CompilerParams(dimension_semantics=("arbitrary",),
                                             vmem_limit_bytes=EXPERT_VMEM_LIMIT),
        name="moe_experts",
    )(n_used, block_e[:1], slot, next_e, c0, c1, x_rows, wg, wu, wd)


def _combine_kernel(pos_ref, nxt_ref, y_ref, x_ref, mod_ref, wt_ref, fg_ref, o_ref, ybuf, sem, *, final, eps):
    i = pl.program_id(0)
    n_steps = pl.num_programs(0)
    rows, d = x_ref.shape
    n_chunk = d // LANES
    rt, _ = _token_tile(d)
    slot = i % 2
    per_choice = rows * rt

    def gather(p_ref, s):
        def start(r, c):
            for k in range(2):
                src = _tile_tokens(y_ref, pl.multiple_of(p_ref[0, k, r] * rt, rt), rt)
                dst = _tile_tokens(ybuf.at[s], pl.multiple_of(k * per_choice + r * rt, rt), rt)
                pltpu.make_async_copy(src, dst, sem.at[s]).start(priority=k)
            return c
        lax.fori_loop(0, rows, start, 0, unroll=8)

    @pl.when(i == 0)
    def _():
        gather(pos_ref, 0)

    @pl.when(i + 1 < n_steps)
    def _():
        gather(nxt_ref, 1 - slot)

    pltpu.make_async_copy(_tile_tokens(y_ref, 0, 2 * per_choice), ybuf.at[slot], sem.at[slot]).wait()

    w = wt_ref[...]
    w0, w1 = w[:, 0:1], w[:, 1:2]
    sumsq = jnp.zeros((rows, 1), F32)
    for c in range(n_chunk):
        cols = slice(c * LANES, (c + 1) * LANES)
        moe = (w0 * ybuf[(slot,) + _tile_chunk(c, rows, rt)]
               + w1 * ybuf[(slot,) + _tile_chunk(c, rows, rt, first=per_choice)])
        xc = x_ref[:, cols] + mod_ref[0, 5:6, cols] * moe
        o_ref[:, cols] = xc
        sumsq = sumsq + jnp.sum(xc * xc, axis=-1, keepdims=True)
    if final:
        o_ref[...] = (o_ref[...] * lax.rsqrt(sumsq * (1.0 / d) + eps)) * fg_ref[...]


def _combine(cfg, y, pos_blocks, wts_t, x, mod_l, final_gain, *, row0, n_rows, final):
    d = cfg.d_model
    tb = pos_blocks.shape[2]
    rt, npl = _token_tile(d)
    b0 = row0 // tb
    mi = _merged_mod_index(cfg, tb)
    kern = functools.partial(_combine_kernel, final=final, eps=cfg.norm_eps)
    n_steps = n_rows // tb
    return pl.pallas_call(
        kern,
        grid=(n_steps,),
        in_specs=[
            pl.BlockSpec((1, 2, tb), lambda i: (b0 + i, 0, 0), memory_space=pltpu.SMEM),
            pl.BlockSpec((1, 2, tb), lambda i: (b0 + jnp.minimum(i + 1, n_steps - 1), 0, 0),
                         memory_space=pltpu.SMEM),
            pl.BlockSpec(memory_space=pl.ANY),
            pl.BlockSpec((tb, d), lambda i: (b0 + i, 0)),
            pl.BlockSpec((1, 8, d), lambda i: (mi(b0 + i), 0, 0)),
            pl.BlockSpec((tb, 2), lambda i: (b0 + i, 0)),
            pl.BlockSpec((1, d), lambda i: (0, 0)),
        ],
        out_specs=pl.BlockSpec((tb, d), lambda i: (i, 0)),
        out_shape=jax.ShapeDtypeStruct((n_rows, d), F32),
        scratch_shapes=[pltpu.VMEM((2, npl, 2 * tb * rt, LANES), F32), pltpu.SemaphoreType.DMA((2,))],
        compiler_params=_params(("arbitrary",)),
        name="moe_combine_final" if final else "moe_combine",
    )(pos_blocks, pos_blocks, y, x, mod_l, wts_t, final_gain)


def _moe(cfg, x, mod_l, gain, w_router_t, router_bias, wg, wu, wd, layer, final_gain, final):
    tb = min(256, cfg.tm)
    hp, idx, wts = _route(cfg, x, mod_l, gain, w_router_t, router_bias)
    pos, block_e, n_used, pad_end, n_blocks = _plan(cfg, idx)
    pos_blocks = pos.reshape(2, cfg.n_tok // tb, tb).transpose(1, 0, 2)
    td = cfg.tm
    x_rows = _dispatch(cfg, hp, pos.reshape(2, cfg.n_tok // td, td).transpose(1, 0, 2), pad_end,
                       n_blocks * cfg.e_block)
    y = _experts(cfg, x_rows, block_e, n_used, wg, wu, wd, layer, n_blocks)
    wts_t = wts.T
    comb = functools.partial(_combine, cfg, y, pos_blocks, wts_t, x, mod_l, final_gain, final=final)
    if final:
        return comb(row0=0, n_rows=cfg.n_p), comb(row0=cfg.n_p, n_rows=cfg.n_s)
    return comb(row0=0, n_rows=cfg.n_tok)


def _forward(cfg, x_prompt, x_sample, cache_a_k, cache_a_v, cache_b_k, cache_b_v, c, c_ctx, w_ada, b_ada,
             norm1, norm2, final_norm, w_in, w_mix_out, lambda_q1, lambda_k1, lambda_q2, lambda_k2,
             subln_gain, na_rel_bias, w_fourier_out, w_router, router_bias, w_exp_gate, w_exp_up,
             w_exp_down):
    d = cfg.d_model
    xp = x_prompt.reshape(cfg.n_p, d)
    xs = x_sample.reshape(cfg.n_s, d)

    n_cond = -(-cfg.n_mod // 8) * 8
    cond = jnp.concatenate([c_ctx[None, :], c, jnp.zeros((n_cond - cfg.n_mod, d), F32)], axis=0)
    mod = _modulation(cfg, cond, w_ada, b_ada)
    mod = mod.reshape(cfg.depth, n_cond, 6, d)[:, :cfg.n_mod]
    mod = jnp.pad(mod, ((0, 0), (0, 0), (0, 2), (0, 0)))

    w_router_t = w_router.T
    fgain = final_norm.reshape(1, d)
    x = (xp, xs)
    kv_cache = None
    for l in range(cfg.depth):
        j = l // 2
        g1 = norm1[l].reshape(1, d)
        if l % 2 == 0:
            lam_init = 0.8 - 0.6 * math.exp(-0.3 * l)
            lam_pack = jnp.zeros((8, LANES), F32).at[:4, :cfg.a_qk].set(
                jnp.stack([lambda_q1[j], lambda_k1[j], lambda_q2[j], lambda_k2[j]]))
            sgain = subln_gain[j].reshape(1, cfg.a_dim)
            w_in_b = w_in[j].astype(BF16)
            if isinstance(x, tuple):
                x_p, x_s = x
            else:
                x_p, x_s = x[:cfg.n_p], x[cfg.n_p:]
            proj_p = _projection(cfg, x_p, mod[l], g1, w_in_b, latent=False, out_dtype=F32)
            proj_s = _projection(cfg, x_s, mod[l], g1, w_in_b, latent=True, out_dtype=BF16)
            if kv_cache is None:
                kv_cache = []
            aw, bw = cfg.a_width, cfg.b_width
            cak = cache_a_k[:, j].reshape(cfg.dec_batch, cfg.past_len, aw)
            cav = cache_a_v[:, j].reshape(cfg.dec_batch, cfg.past_len, aw)
            cbk = cache_b_k[:, j].reshape(cfg.dec_batch, cfg.past_len, bw)
            cbv = cache_b_v[:, j].reshape(cfg.dec_batch, cfg.past_len, bw)
            a_p, new_ak, new_av = _diff_attention(cfg, proj_p, lam_pack, sgain, cfg.batch, cfg.seq, lam_init)
            b_p, new_bk, new_bv = _soft_attention(cfg, proj_p, cfg.batch, cfg.seq)
            kv_cache.append((new_ak, new_av, new_bk, new_bv))
            a_s = _diff_attention(cfg, proj_s, lam_pack, sgain, cfg.dec_batch, cfg.dec_seq, lam_init,
                                  ctx=(cak, cav))
            b_s = _na_attention(cfg, proj_s, cbk, cbv, na_rel_bias[j])
            x = _linear_residual(cfg, [(a_p, a_s), (b_p, b_s)], w_mix_out[j].astype(BF16), x, mod[l])
        else:
            if isinstance(x, tuple):
                x = jnp.concatenate(x, axis=0)
            y = _dft_channels(cfg, x, mod[l], g1)
            f_p = _dft_sequence(cfg, y, cfg.batch, cfg.seq, 0)
            f_s = _dft_sequence(cfg, y, cfg.dec_batch, cfg.dec_seq, cfg.n_p // cfg.dec_seq)
            x = _linear_residual(cfg, [(f_p, f_s)], w_fourier_out[j].astype(BF16), x, mod[l])
        x = _moe(cfg, x, mod[l], norm2[l].reshape(1, d), w_router_t, router_bias,
                 w_exp_gate, w_exp_up, w_exp_down, l, fgain, final=(l == cfg.depth - 1))
    y_p, y_s = x
    n_even = (cfg.depth + 1) // 2
    outs = [y_p.reshape(cfg.batch, cfg.seq, d), y_s.reshape(cfg.dec_batch, cfg.dec_seq, d)]
    for t in range(4):
        heads, hd = (cfg.a_heads, cfg.a_dim) if t < 2 else (cfg.b_heads, cfg.b_dim)
        stacked = jnp.stack([kv_cache[jj][t].reshape(cfg.batch, cfg.seq, heads, hd)
                             for jj in range(n_even)], axis=1)
        outs.append(stacked)
    return tuple(outs)


def kernel(x_prompt, x_sample, cache_a_k, cache_a_v, cache_b_k, cache_b_v, c, c_ctx, w_ada, b_ada, norm1, norm2, final_norm, w_in, w_mix_out, lambda_q1, lambda_k1, lambda_q2, lambda_k2, subln_gain, na_rel_bias, w_fourier_out, w_router, router_bias, w_exp_gate, w_exp_up, w_exp_down):
    return _forward(Cfg(), x_prompt, x_sample, cache_a_k, cache_a_v, cache_b_k, cache_b_v, c, c_ctx, w_ada,
                    b_ada, norm1, norm2, final_norm, w_in, w_mix_out, lambda_q1, lambda_k1, lambda_q2,
                    lambda_k2, subln_gain, na_rel_bias, w_fourier_out, w_router, router_bias, w_exp_gate,
                    w_exp_up, w_exp_down)
```

```python
import functools
import math
from typing import NamedTuple

import numpy as np
import jax
import jax.numpy as jnp
from jax import lax
from jax.experimental import pallas as pl
from jax.experimental.pallas import tpu as pltpu

F32 = jnp.float32
BF16 = jnp.bfloat16

LANES = 128
NEG_BIG = -1e30
VMEM_LIMIT = 56 * 1024 * 1024


class Cfg(NamedTuple):
    d_model: int = 2048
    batch: int = 16
    seq: int = 256
    depth: int = 2
    dec_batch: int = 8
    dec_seq: int = 2048
    past_len: int = 256
    grid_w: int = 64
    a_heads: int = 8
    a_qk: int = 64
    b_heads: int = 8
    b_dim: int = 128
    na_win_r: int = 8
    na_win_c: int = 16
    f_groups: int = 4
    n_experts: int = 16
    n_groups: int = 4
    d_expert: int = 1408
    rope_theta: float = 10000.0
    norm_eps: float = 1e-6
    subln_eps: float = 1e-5
    row_tile: int = 512
    proj_tile: int = 1024
    q_tile: int = 512
    e_block: int = 256

    @property
    def a_dim(self):
        return 2 * self.a_qk

    @property
    def a_width(self):
        return self.a_heads * self.a_dim

    @property
    def b_width(self):
        return self.b_heads * self.b_dim

    @property
    def n_p(self):
        return self.batch * self.seq

    @property
    def n_s(self):
        return self.dec_batch * self.dec_seq

    @property
    def n_tok(self):
        return self.n_p + self.n_s

    @property
    def tm(self):
        return min(self.row_tile, self.n_p, self.dec_seq)

    @property
    def n_mod(self):
        return 1 + self.dec_batch


def _params(sem):
    return pltpu.CompilerParams(dimension_semantics=sem, vmem_limit_bytes=VMEM_LIMIT)


def _norm_mod(x, g, sh, sc, eps):
    ms = jnp.mean(x * x, axis=-1, keepdims=True)
    y = x * lax.rsqrt(ms + eps)
    return (y * g) * (1.0 + sc) + sh


def _silu(x):
    return x * jax.nn.sigmoid(x)


def _merged_mod_index(cfg, tm):
    def f(i):
        r = i * tm
        return jnp.where(r < cfg.n_p, 0, 1 + (r - cfg.n_p) // cfg.dec_seq)
    return f


def _ada_kernel(cond_ref, w_ref, b_ref, o_ref):
    s = _silu(cond_ref[...])
    s_hi = s.astype(BF16)
    s_lo = (s - s_hi.astype(F32)).astype(BF16)
    lhs = jnp.concatenate([s_hi, s_lo], axis=0)
    r = jnp.dot(lhs, w_ref[0].astype(BF16), preferred_element_type=F32)
    n = s.shape[0]
    o_ref[0] = r[:n] + r[n:] + b_ref[0]


def _modulation(cfg, cond, w_ada, b_ada):
    d = cfg.d_model
    r = cond.shape[0]
    tn = math.gcd(1024, 6 * d)
    return pl.pallas_call(
        _ada_kernel,
        grid=(cfg.depth, 6 * d // tn),
        in_specs=[
            pl.BlockSpec((r, d), lambda l, j: (0, 0)),
            pl.BlockSpec((1, d, tn), lambda l, j: (l, 0, j)),
            pl.BlockSpec((1, 1, tn), lambda l, j: (l, 0, j)),
        ],
        out_specs=pl.BlockSpec((1, r, tn), lambda l, j: (l, 0, j)),
        out_shape=jax.ShapeDtypeStruct((cfg.depth, r, 6 * d), F32),
        compiler_params=_params(("arbitrary", "arbitrary")),
        name="ada_modulation",
    )(cond, w_ada, b_ada.reshape(cfg.depth, 1, 6 * d))


PROJ_CHUNK = 256


def _proj_kernel(*refs, rope, n_rope_blocks, eps):
    if rope:
        x_ref, mod_ref, g_ref, w_ref, cos_ref, sa_ref, sb_ref, o_ref, h_ref = refs
    else:
        x_ref, mod_ref, g_ref, w_ref, o_ref, h_ref = refs
    j = pl.program_id(1)

    @pl.when(j == 0)
    def _():
        h = _norm_mod(x_ref[...], g_ref[...], mod_ref[0, 0:1, :], mod_ref[0, 1:2, :], eps)
        h_ref[...] = h.astype(BF16)

    if not rope:
        o_ref[...] = jnp.dot(h_ref[...], w_ref[...], preferred_element_type=F32).astype(o_ref.dtype)
        return

    @pl.when(j < n_rope_blocks)
    def _():
        cos, sa, sb = cos_ref[...], sa_ref[...], sb_ref[...]
        h = h_ref[...]
        tn = w_ref.shape[1]
        cw = min(PROJ_CHUNK, tn)
        for c0 in range(0, tn, cw):
            acc = jnp.dot(h, w_ref[:, c0:c0 + cw], preferred_element_type=F32)
            for c in range(cw // LANES):
                xa = acc[:, c * LANES:(c + 1) * LANES]
                up = pltpu.roll(xa, LANES - 16, 1)
                dn = pltpu.roll(xa, 16, 1)
                lo = c0 + c * LANES
                o_ref[:, lo:lo + LANES] = (xa * cos + up * sa + dn * sb).astype(o_ref.dtype)

    @pl.when(j >= n_rope_blocks)
    def _():
        o_ref[...] = jnp.dot(h_ref[...], w_ref[...], preferred_element_type=F32).astype(o_ref.dtype)


def _rope_tables(cfg):
    t = np.arange(cfg.dec_seq)
    row = (t // cfg.grid_w).astype(np.float64)
    col = (t % cfg.grid_w).astype(np.float64)
    lane = np.arange(LANES)
    l64 = lane % cfg.a_qk
    half = cfg.a_qk // 4
    freq = cfg.rope_theta ** (-(lane % half).astype(np.float64) / half)
    pos = np.where((l64 < cfg.a_qk // 2)[None, :], row[:, None], col[:, None])
    ang = pos * freq[None, :]
    first = (lane % (2 * half)) < half
    cos = np.cos(ang)
    sin = np.sin(ang)
    sa = np.where(first[None, :], -sin, 0.0)
    sb = np.where(first[None, :], 0.0, sin)
    return tuple(jnp.asarray(a, dtype=F32) for a in (cos, sa, sb))


def _projection(cfg, x, mod_l, gain, w, *, latent, out_dtype):
    m, d = x.shape
    n = w.shape[1]
    tm = min(cfg.proj_tile, m, cfg.dec_seq)
    tn = min(1024, cfg.a_width)
    per_seq = cfg.dec_seq // tm if latent else 1
    mod_idx = (lambda i, j: (1 + i // per_seq, 0, 0)) if latent else (lambda i, j: (0, 0, 0))
    in_specs = [
        pl.BlockSpec((tm, d), lambda i, j: (i, 0)),
        pl.BlockSpec((1, 8, d), mod_idx),
        pl.BlockSpec((1, d), lambda i, j: (0, 0)),
        pl.BlockSpec((d, tn), lambda i, j: (0, j)),
    ]
    args = [x, mod_l, gain, w]
    if latent:
        tab_spec = pl.BlockSpec((tm, LANES), lambda i, j: (i % per_seq, 0))
        in_specs += [tab_spec, tab_spec, tab_spec]
        args += list(_rope_tables(cfg))
    kern = functools.partial(_proj_kernel, rope=latent, n_rope_blocks=2 * cfg.a_width // tn,
                             eps=cfg.norm_eps)
    return pl.pallas_call(
        kern,
        grid=(m // tm, n // tn),
        in_specs=in_specs,
        out_specs=pl.BlockSpec((tm, tn), lambda i, j: (i, j)),
        out_shape=jax.ShapeDtypeStruct((m, n), out_dtype),
        scratch_shapes=[pltpu.VMEM((tm, d), BF16)],
        compiler_params=_params(("arbitrary", "arbitrary")),
        name="qkv_projection_latent" if latent else "qkv_projection_context",
    )(*args)


def _nt_dot(a, b):
    return lax.dot_general(a, b, (((1,), (1,)), ((), ())), preferred_element_type=F32)


DIFF_SUB_TILE = 128


def _diff_attn_kernel(*refs, has_ctx, lam_init, eps, qk):
    if has_ctx:
        lam_ref, gain_ref, q_ref, k_ref, v_ref, ck_ref, cv_ref, o_ref = refs
    else:
        lam_ref, gain_ref, q_ref, k_ref, v_ref, o_ref, ko_ref, vo_ref = refs
        ko_ref[...] = k_ref[...]
        vo_ref[...] = v_ref[...]
    lv = lam_ref[...]
    s1 = jnp.sum(lv[0:1] * lv[1:2], axis=-1, keepdims=True)
    s2 = jnp.sum(lv[2:3] * lv[3:4], axis=-1, keepdims=True)
    lam = jnp.exp(s1) - jnp.exp(s2) + lam_init

    hd = 2 * qk
    tq = q_ref.shape[0]
    ts = min(tq, DIFF_SUB_TILE)
    for hh, t in [(hh, t) for hh in range(q_ref.shape[1] // hd) for t in range(tq // ts)]:
        cols = slice(hh * hd, (hh + 1) * hd)
        k = k_ref[:, cols].astype(BF16)
        v = v_ref[:, cols].astype(BF16)
        if has_ctx:
            ck = ck_ref[0, :, cols].astype(BF16)
            cv = cv_ref[0, :, cols].astype(BF16)
        q = q_ref[t * ts:(t + 1) * ts, cols].astype(F32) * (qk ** -0.5 * math.log2(math.e))
        lane = lax.broadcasted_iota(jnp.int32, q.shape, 1)
        qs = jnp.concatenate([jnp.where(lane < qk, q, 0.0), jnp.where(lane >= qk, q, 0.0)],
                             axis=0).astype(BF16)
        s_new = _nt_dot(qs, k)
        m = jnp.max(s_new, axis=-1, keepdims=True)
        if has_ctx:
            s_ctx = _nt_dot(qs, ck)
            m = jnp.maximum(m, jnp.max(s_ctx, axis=-1, keepdims=True))
        p_new = jnp.exp2(s_new - m)
        den = jnp.sum(p_new, axis=-1, keepdims=True)
        o2 = jnp.dot(p_new.astype(BF16), v, preferred_element_type=F32)
        if has_ctx:
            p_ctx = jnp.exp2(s_ctx - m)
            den = den + jnp.sum(p_ctx, axis=-1, keepdims=True)
            o2 = o2 + jnp.dot(p_ctx.astype(BF16), cv, preferred_element_type=F32)
        inv = 1.0 / den
        o = o2[:ts] * inv[:ts] - o2[ts:] * (lam * inv[ts:])
        ms = jnp.mean(o * o, axis=-1, keepdims=True)
        o = (o * lax.rsqrt(ms + eps)) * gain_ref[...]
        o_ref[t * ts:(t + 1) * ts, cols] = (o * (1.0 - lam_init)).astype(o_ref.dtype)


def _diff_attention(cfg, proj, lam_pack, gain, n_batch, seq_len, lam_init, ctx=None):
    hd = cfg.a_dim
    hps = cfg.a_heads if ctx is None else 1
    nh = cfg.a_heads // hps
    bw = hps * hd
    tq = min(cfg.q_tile, seq_len)
    nq = seq_len // tq
    in_specs = [
        pl.BlockSpec((8, LANES), lambda b, h, qi: (0, 0)),
        pl.BlockSpec((1, hd), lambda b, h, qi: (0, 0)),
        pl.BlockSpec((tq, bw), lambda b, h, qi: (b * nq + qi, h)),
        pl.BlockSpec((seq_len, bw), lambda b, h, qi: (b, nh + h)),
        pl.BlockSpec((seq_len, bw), lambda b, h, qi: (b, 2 * nh + h)),
    ]
    args = [lam_pack, gain, proj, proj, proj]
    if ctx is not None:
        ck, cv, cblk = ctx
        past = cfg.past_len
        cspec = pl.BlockSpec((1, past, bw), lambda b, h, qi: (b, cblk, h))
        in_specs += [cspec, cspec]
        args += [ck, cv]
    kern = functools.partial(_diff_attn_kernel, has_ctx=ctx is not None, lam_init=lam_init,
                             eps=cfg.subln_eps, qk=cfg.a_qk)
    out_specs = pl.BlockSpec((tq, bw), lambda b, h, qi: (b * nq + qi, h))
    out_shape = jax.ShapeDtypeStruct((n_batch * seq_len, cfg.a_width), BF16)
    if ctx is None:
        assert nq == 1 and proj.dtype == F32
        kv_spec = pl.BlockSpec((seq_len, bw), lambda b, h, qi: (b, h))
        kv_shape = jax.ShapeDtypeStruct((n_batch * seq_len, cfg.a_width), F32)
        out_specs, out_shape = [out_specs, kv_spec, kv_spec], [out_shape, kv_shape, kv_shape]
    return pl.pallas_call(
        kern,
        grid=(n_batch, nh, nq),
        in_specs=in_specs,
        out_specs=out_specs,
        out_shape=out_shape,
        compiler_params=_params(("arbitrary", "arbitrary", "arbitrary")),
        name="diff_attention_latent" if ctx is not None else "diff_attention_context",
    )(*args)


def _soft_attn_kernel(q_ref, k_ref, v_ref, o_ref, ko_ref, vo_ref, *, hd):
    ko_ref[...] = k_ref[...]
    vo_ref[...] = v_ref[...]
    for hh in range(q_ref.shape[1] // hd):
        cols = slice(hh * hd, (hh + 1) * hd)
        s = _nt_dot(q_ref[:, cols].astype(BF16), k_ref[:, cols].astype(BF16)) * (hd ** -0.5 * math.log2(math.e))
        m = jnp.max(s, axis=-1, keepdims=True)
        p = jnp.exp2(s - m)
        inv = 1.0 / jnp.sum(p, axis=-1, keepdims=True)
        o = jnp.dot(p.astype(BF16), v_ref[:, cols].astype(BF16), preferred_element_type=F32) * inv
        o_ref[:, cols] = o.astype(o_ref.dtype)


def _soft_attention(cfg, proj, n_batch, seq_len):
    bw = cfg.b_width
    assert (3 * cfg.a_width) % bw == 0
    base = 3 * cfg.a_width // bw
    return pl.pallas_call(
        functools.partial(_soft_attn_kernel, hd=cfg.b_dim),
        grid=(n_batch,),
        in_specs=[
            pl.BlockSpec((seq_len, bw), lambda b: (b, base)),
            pl.BlockSpec((seq_len, bw), lambda b: (b, base + 1)),
            pl.BlockSpec((seq_len, bw), lambda b: (b, base + 2)),
        ],
        out_specs=[pl.BlockSpec((seq_len, bw), lambda b: (b, 0))] * 3,
        out_shape=[jax.ShapeDtypeStruct((n_batch * seq_len, bw), BF16),
                   jax.ShapeDtypeStruct((n_batch * seq_len, bw), proj.dtype),
                   jax.ShapeDtypeStruct((n_batch * seq_len, bw), proj.dtype)],
        compiler_params=_params(("arbitrary",)),
        name="softmax_attention_context",
    )(proj, proj, proj)


NA_GROUP_ROWS = 4


def _na_geometry(cfg):
    rows = cfg.dec_seq // cfg.grid_w
    kr = min(cfg.na_win_r, rows)
    grp = min(NA_GROUP_ROWS, rows)
    union = min(rows, kr + grp - 1 + (kr + grp - 1) % 2)
    starts = []
    for g in range(rows // grp):
        rs0 = min(max(g * grp - kr // 2, 0), rows - kr)
        starts.append(min(rs0, rows - union))
    return rows, kr, grp, union, starts


def _na_kernel(ws_ref, plane_ref, q_ref, k_ref, v_ref, ck_ref, cv_ref, tab_ref, o_ref, sctx, pctx, oacc, *,
               n_groups, grp, union, gw, scale):
    gq, uk = grp * gw, union * gw
    ck = ck_ref[0].astype(BF16)
    cv = cv_ref[0].astype(BF16)
    sctx[...] = _nt_dot(q_ref[...], ck) * scale
    first_half = lax.broadcasted_iota(jnp.int32, (gw, LANES), 1) < gw

    def body(g, carry):
        q0 = pl.multiple_of(g * gq, gq)
        k0 = pl.multiple_of(ws_ref[g] * gw, gw)
        q = q_ref[pl.ds(q0, gq), :]
        bias_rows = []
        for a in range(grp):
            base = (g * grp + a) * union
            tiles = [jnp.where(first_half, tab_ref[0, plane_ref[base + 2 * jj]],
                               tab_ref[0, plane_ref[base + 2 * jj + 1]]) for jj in range(union // 2)]
            bias_rows.append(jnp.concatenate(tiles, axis=1))
        bias = jnp.concatenate(bias_rows, axis=0)
        s = _nt_dot(q, k_ref[pl.ds(k0, uk), :]) * scale + bias
        sc = sctx[pl.ds(q0, gq), :]
        m = jnp.maximum(jnp.max(s, axis=-1, keepdims=True), jnp.max(sc, axis=-1, keepdims=True))
        p = jnp.exp2(s - m)
        pc = jnp.exp2(sc - m)
        inv = 1.0 / (jnp.sum(p, axis=-1, keepdims=True) + jnp.sum(pc, axis=-1, keepdims=True))
        oacc[pl.ds(q0, gq), :] = jnp.dot(p.astype(BF16), v_ref[pl.ds(k0, uk), :],
                                         preferred_element_type=F32) * inv
        pctx[pl.ds(q0, gq), :] = (pc * inv).astype(BF16)
        return carry

    lax.fori_loop(0, n_groups, body, 0, unroll=2)
    o_ref[...] = (oacc[...] + jnp.dot(pctx[...], cv, preferred_element_type=F32)).astype(o_ref.dtype)


def _na_bias_table(cfg, rpb):
    w = cfg.grid_w
    assert LANES == 2 * w
    rows, kr, grp, union, starts = _na_geometry(cfg)
    qc = np.arange(w)
    kc = np.arange(w)
    cs = np.clip(qc - cfg.na_win_c // 2, 0, w - cfg.na_win_c)
    col_mask = (kc[None, :] >= cs[:, None]) & (kc[None, :] < cs[:, None] + cfg.na_win_c)
    col_idx = np.clip(kc[None, :] - qc[:, None] + cfg.na_win_c - 1, 0, 2 * cfg.na_win_c - 2)
    n_c = 2 * cfg.na_win_c - 1
    onehot = (col_idx[None] == np.arange(n_c)[:, None, None]) & col_mask[None]
    t = jnp.einsum('hrc,cqk->hrqk', rpb.astype(F32), jnp.asarray(onehot, dtype=F32),
                   precision=lax.Precision.HIGHEST)
    t = jnp.where(jnp.asarray(col_mask)[None, None], t * math.log2(math.e), NEG_BIG)
    n_r = 2 * cfg.na_win_r - 1
    t = jnp.concatenate([t, jnp.full((cfg.b_heads, 1, w, w), NEG_BIG, F32)], axis=1)
    plane = np.full((len(starts), grp, union), n_r, np.int32)
    for g, ws in enumerate(starts):
        for a in range(grp):
            r = g * grp + a
            rs = min(max(r - kr // 2, 0), rows - kr)
            for j in range(union):
                if rs <= ws + j < rs + kr:
                    plane[g, a, j] = ws + j - r + cfg.na_win_r - 1
    return jnp.concatenate([t, t], axis=-1), jnp.asarray(plane.reshape(-1))


def _na_attention(cfg, proj, ck, cv, cblk, rpb):
    hd = cfg.b_dim
    nh = cfg.b_heads
    n = cfg.dec_seq
    rows, kr, grp, union, starts = _na_geometry(cfg)
    n_groups = len(starts)
    base = 3 * cfg.a_width // hd
    past = cfg.past_len
    table, plane = _na_bias_table(cfg, rpb)
    kern = functools.partial(_na_kernel, n_groups=n_groups, grp=grp, union=union, gw=cfg.grid_w,
                             scale=hd ** -0.5 * math.log2(math.e))
    cspec = pl.BlockSpec((1, past, hd), lambda h, b, *_: (b, cblk, h))
    return pl.pallas_call(
        kern,
        grid_spec=pltpu.PrefetchScalarGridSpec(
            num_scalar_prefetch=2,
            grid=(nh, cfg.dec_batch),
            in_specs=[
                pl.BlockSpec((n, hd), lambda h, b, *_: (b, base + h)),
                pl.BlockSpec((n, hd), lambda h, b, *_: (b, base + nh + h)),
                pl.BlockSpec((n, hd), lambda h, b, *_: (b, base + 2 * nh + h)),
                cspec, cspec,
                pl.BlockSpec((1,) + table.shape[1:], lambda h, b, *_: (h, 0, 0, 0)),
            ],
            out_specs=pl.BlockSpec((n, hd), lambda h, b, *_: (b, h)),
            scratch_shapes=[pltpu.VMEM((n, past), F32), pltpu.VMEM((n, past), BF16), pltpu.VMEM((n, hd), F32)],
        ),
        out_shape=jax.ShapeDtypeStruct((cfg.n_s, cfg.b_width), BF16),
        compiler_params=_params(("arbitrary", "arbitrary")),
        name="neighbourhood_attention",
    )(jnp.asarray(np.asarray(starts, np.int32)), plane, proj, proj, proj, ck, cv, table)


def _linres_kernel(*refs, n_parts, n_pb, x_split):
    i = pl.program_id(0)
    is_p = i < n_pb
    pos = 0
    acc = None
    w_ref = refs[2 * n_parts]
    k0 = 0
    for p in range(n_parts):
        a_p, a_s = refs[2 * p], refs[2 * p + 1]
        a = jnp.where(is_p, a_p[...], a_s[...])
        kk = a.shape[1]
        part = jnp.dot(a, w_ref[k0:k0 + kk, :], preferred_element_type=F32)
        acc = part if acc is None else acc + part
        k0 += kk
    pos = 2 * n_parts + 1
    if x_split:
        x = jnp.where(is_p, refs[pos][...], refs[pos + 1][...])
        pos += 2
    else:
        x = refs[pos][...]
        pos += 1
    mod_ref, o_ref = refs[pos], refs[pos + 1]
    o_ref[...] = x + mod_ref[0, 2:3, :] * acc


def _linear_residual(cfg, parts, w, x, mod_l):
    d = cfg.d_model
    tm = min(256, cfg.tm)
    n_pb = cfg.n_p // tm
    n_sb = cfg.n_s // tm
    p_idx = lambda i: (jnp.minimum(i, n_pb - 1), 0)
    s_idx = lambda i: (jnp.maximum(i - n_pb, 0), 0)
    in_specs, args = [], []
    for a_p, a_s in parts:
        kk = a_p.shape[1]
        in_specs += [pl.BlockSpec((tm, kk), p_idx), pl.BlockSpec((tm, kk), s_idx)]
        args += [a_p, a_s]
    in_specs.append(pl.BlockSpec(w.shape, lambda i: (0, 0)))
    args.append(w)
    x_split = isinstance(x, tuple)
    if x_split:
        in_specs += [pl.BlockSpec((tm, d), p_idx), pl.BlockSpec((tm, d), s_idx)]
        args += list(x)
    else:
        in_specs.append(pl.BlockSpec((tm, d), lambda i: (i, 0)))
        args.append(x)
    mi = _merged_mod_index(cfg, tm)
    in_specs.append(pl.BlockSpec((1, 8, d), lambda i: (mi(i), 0, 0)))
    args.append(mod_l)
    kern = functools.partial(_linres_kernel, n_parts=len(parts), n_pb=n_pb, x_split=x_split)
    return pl.pallas_call(
        kern,
        grid=(n_pb + n_sb,),
        in_specs=in_specs,
        out_specs=pl.BlockSpec((tm, d), lambda i: (i, 0)),
        out_shape=jax.ShapeDtypeStruct((cfg.n_tok, d), F32),
        compiler_params=_params(("arbitrary",)),
        name="linear_gated_residual",
    )(*args)


def _dft_chan_kernel(x_ref, mod_ref, g_ref, cs_ref, y_ref, *, groups, eps):
    h = _norm_mod(x_ref[...], g_ref[...], mod_ref[0, 0:1, :], mod_ref[0, 1:2, :], eps).astype(BF16)
    gd = h.shape[1] // groups
    for g in range(groups):
        r = jnp.dot(h[:, g * gd:(g + 1) * gd], cs_ref[...], preferred_element_type=F32)
        y_ref[0, :, g * gd:(g + 1) * gd] = r[:, :gd].astype(BF16)
        y_ref[1, :, g * gd:(g + 1) * gd] = r[:, gd:].astype(BF16)


def _dft_mats(n):
    k = np.arange(n)
    ang = 2.0 * np.pi * ((k[:, None] * k[None, :]) % n) / n
    return np.cos(ang), np.sin(ang)


def _dft_channels(cfg, x, mod_l, gain):
    d = cfg.d_model
    gd = d // cfg.f_groups
    tm = cfg.tm
    c, s = _dft_mats(gd)
    cs = jnp.asarray(np.concatenate([c, s], axis=1), dtype=F32).astype(BF16)
    mi = _merged_mod_index(cfg, tm)
    return pl.pallas_call(
        functools.partial(_dft_chan_kernel, groups=cfg.f_groups, eps=cfg.norm_eps),
        grid=(cfg.n_tok // tm,),
        in_specs=[
            pl.BlockSpec((tm, d), lambda i: (i, 0)),
            pl.BlockSpec((1, 8, d), lambda i: (mi(i), 0, 0)),
            pl.BlockSpec((1, d), lambda i: (0, 0)),
            pl.BlockSpec((gd, 2 * gd), lambda i: (0, 0)),
        ],
        out_specs=pl.BlockSpec((2, tm, d), lambda i: (0, i, 0)),
        out_shape=jax.ShapeDtypeStruct((2, cfg.n_tok, d), BF16),
        compiler_params=_params(("arbitrary",)),
        name="dft_channels",
    )(x, mod_l, gain, cs)


def _dft_seq_kernel(w_ref, y_ref, o_ref, *, scale):
    acc = (jnp.dot(w_ref[0], y_ref[0], preferred_element_type=F32)
           + jnp.dot(w_ref[1], y_ref[1], preferred_element_type=F32))
    o_ref[...] = (acc * scale).astype(o_ref.dtype)


def _dft_sequence(cfg, y, n_batch, seq_len, first_block):
    d = cfg.d_model
    c, s = _dft_mats(seq_len)
    wm = jnp.asarray(np.stack([c, -s]), dtype=F32).astype(BF16)
    tml = min(1024, seq_len)
    tn = min(512, d)
    nm = seq_len // tml
    scale = 1.0 / math.sqrt(seq_len * (d // cfg.f_groups))
    return pl.pallas_call(
        functools.partial(_dft_seq_kernel, scale=scale),
        grid=(n_batch, nm, d // tn),
        in_specs=[
            pl.BlockSpec((2, tml, seq_len), lambda b, mi, j: (0, mi, 0)),
            pl.BlockSpec((2, seq_len, tn), lambda b, mi, j: (0, first_block + b, j)),
        ],
        out_specs=pl.BlockSpec((tml, tn), lambda b, mi, j: (b * nm + mi, j)),
        out_shape=jax.ShapeDtypeStruct((n_batch * seq_len, d), BF16),
        compiler_params=_params(("arbitrary", "arbitrary", "arbitrary")),
        name="dft_sequence_%d" % seq_len,
    )(wm, y)


TOKEN_TILE_ROWS = 8


def _token_tile(d):
    rows = min(TOKEN_TILE_ROWS, d // LANES)
    return rows, d // (rows * LANES)


def _tile_chunk(c, n_tok, rows, first=0):
    return c // rows, pl.ds(first + c % rows, n_tok, stride=rows), slice(None)


def _tile_tokens(ref, first_row, n_rows):
    return ref.at[:, pl.ds(first_row, n_rows), :]


def _route_kernel(x_ref, mod_ref, g_ref, wr_ref, rb_ref, hp_ref, idx_ref, wt_ref, *, eps, n_exp, per_grp):
    h = _norm_mod(x_ref[...], g_ref[...], mod_ref[0, 3:4, :], mod_ref[0, 4:5, :], eps)
    tm, d = h.shape
    rt, _ = _token_tile(d)
    for c in range(d // LANES):
        hp_ref[_tile_chunk(c, tm, rt)] = h[:, c * LANES:(c + 1) * LANES]

    h_hi = h.astype(BF16)
    h_lo = (h - h_hi.astype(F32)).astype(BF16)
    w = wr_ref[...]
    w_hi = w.astype(BF16)
    w_lo = (w - w_hi.astype(F32)).astype(BF16)
    logits = _nt_dot(w_hi, h_hi) + (_nt_dot(w_hi, h_lo) + _nt_dot(w_lo, h_hi))
    scores = jax.nn.sigmoid(logits)
    sel = scores + rb_ref[...]
    n_grp = n_exp // per_grp
    best = None
    gi = None
    for g in range(n_grp):
        v = [sel[g * per_grp + k:g * per_grp + k + 1, :] for k in range(per_grp)]
        gs = None
        for a in range(per_grp):
            for b in range(a + 1, per_grp):
                ps = v[a] + v[b]
                gs = ps if gs is None else jnp.maximum(gs, ps)
        if best is None:
            best, gi = gs, jnp.zeros(gs.shape, jnp.int32)
        else:
            better = gs > best
            gi = jnp.where(better, g, gi)
            best = jnp.where(better, gs, best)
    row = lax.broadcasted_iota(jnp.int32, sel.shape, 0)
    masked = jnp.where(row // per_grp == gi, sel, -jnp.inf)
    m1 = jnp.max(masked, axis=0, keepdims=True)
    i1 = jnp.min(jnp.where(masked == m1, row, n_exp), axis=0, keepdims=True)
    masked2 = jnp.where(row == i1, -jnp.inf, masked)
    m2 = jnp.max(masked2, axis=0, keepdims=True)
    i2 = jnp.min(jnp.where(masked2 == m2, row, n_exp), axis=0, keepdims=True)
    w1 = jnp.sum(jnp.where(row == i1, scores, 0.0), axis=0, keepdims=True)
    w2 = jnp.sum(jnp.where(row == i2, scores, 0.0), axis=0, keepdims=True)
    inv = 1.0 / (w1 + w2)
    idx_ref[...] = jnp.concatenate([i1, i2], axis=0)
    wt_ref[...] = jnp.concatenate([w1 * inv, w2 * inv], axis=0)


def _route(cfg, x, mod_l, gain, w_router_t, router_bias):
    d = cfg.d_model
    tm = cfg.tm
    t = cfg.n_tok
    rt, npl = _token_tile(d)
    mi = _merged_mod_index(cfg, tm)
    kern = functools.partial(_route_kernel, eps=cfg.norm_eps, n_exp=cfg.n_experts,
                             per_grp=cfg.n_experts // cfg.n_groups)
    return pl.pallas_call(
        kern,
        grid=(t // tm,),
        in_specs=[
            pl.BlockSpec((tm, d), lambda i: (i, 0)),
            pl.BlockSpec((1, 8, d), lambda i: (mi(i), 0, 0)),
            pl.BlockSpec((1, d), lambda i: (0, 0)),
            pl.BlockSpec((cfg.n_experts, d), lambda i: (0, 0)),
            pl.BlockSpec((cfg.n_experts, 1), lambda i: (0, 0)),
        ],
        out_specs=[
            pl.BlockSpec((npl, tm * rt, LANES), lambda i: (0, i, 0)),
            pl.BlockSpec((2, tm), lambda i: (0, i)),
            pl.BlockSpec((2, tm), lambda i: (0, i)),
        ],
        out_shape=[
            jax.ShapeDtypeStruct((npl, t * rt, LANES), F32),
            jax.ShapeDtypeStruct((2, t), jnp.int32),
            jax.ShapeDtypeStruct((2, t), F32),
        ],
        compiler_params=_params(("arbitrary",)),
        name="moe_route",
    )(x, mod_l, gain, w_router_t, router_bias.reshape(cfg.n_experts, 1))


def _plan(cfg, idx):
    t = cfg.n_tok
    eb = cfg.e_block
    ne = cfg.n_experts
    n_blocks = -(-(2 * t + ne * (eb - 1)) // eb)
    e_flat = idx.reshape(-1)
    onehot = (e_flat[:, None] == jnp.arange(ne, dtype=jnp.int32)[None, :]).astype(jnp.int32)
    csum = jnp.cumsum(onehot, axis=0)
    rank = jnp.sum(onehot * (csum - 1), axis=1)
    counts = csum[-1]
    padded = (counts + eb - 1) // eb * eb
    pad_end = jnp.cumsum(padded)
    pad_start = pad_end - padded
    pos = (pad_start[e_flat] + rank).astype(jnp.int32).reshape(2, t)
    n_used = (pad_end[-1] // eb).astype(jnp.int32)
    blk = jnp.arange(n_blocks, dtype=jnp.int32)
    blk = jnp.minimum(blk, n_used - 1)
    first_e = jnp.sum((pad_end[None, :] <= (blk * eb)[:, None]).astype(jnp.int32), axis=1)
    block_e = jnp.minimum(first_e, ne - 1).astype(jnp.int32)
    return pos, block_e, n_used.reshape(1), pad_end.astype(jnp.int32), n_blocks


def _weight_schedule(cfg, block_e, n_used, n_blocks):
    n_chunks = _expert_chunks(cfg)[2]
    big = n_blocks + 1
    idx = jnp.arange(n_blocks, dtype=jnp.int32)
    valid = idx < n_used[0]
    prev_e = jnp.concatenate([jnp.full((1,), -1, jnp.int32), block_e[:-1]])
    change = (block_e != prev_e) & valid
    earlier = (idx[None, :] <= idx[:, None]) & change[None, :]
    later = (idx[None, :] > idx[:, None]) & change[None, :]
    run_start = jnp.max(jnp.where(earlier, idx[None, :], 0), axis=1)
    next_start = jnp.minimum(jnp.min(jnp.where(later, idx[None, :], big), axis=1), n_used[0])
    slot = (jnp.sum(earlier.astype(jnp.int32), axis=1) - 1) % 2
    has_next = valid & (next_start < n_used[0])
    next_e = jnp.where(has_next, block_e[jnp.minimum(next_start, n_blocks - 1)], 0)
    run_len = jnp.maximum(next_start - run_start, 1)
    per_block = (n_chunks + run_len - 1) // run_len
    i = idx - run_start
    c0 = jnp.where(has_next, jnp.minimum(i * per_block, n_chunks), 0)
    c1 = jnp.where(has_next, jnp.minimum((i + 1) * per_block, n_chunks), 0)
    return tuple(a.astype(jnp.int32) for a in (slot, next_e, c0, c1))


def _dispatch_kernel(pend_ref, pos_ref, hp_ref, xr_ref, zbuf, sem, zsem, *, n_exp, eb, spt):
    i = pl.program_id(0)

    @pl.when(i == 0)
    def _():
        zbuf[...] = jnp.zeros(zbuf.shape, zbuf.dtype)
        for e in range(n_exp):
            start = pl.multiple_of(jnp.maximum(pend_ref[e] - eb, 0) * spt, eb * spt)
            cp = pltpu.make_async_copy(zbuf, _tile_tokens(xr_ref, start, eb * spt), zsem)
            cp.start()
            cp.wait()
        n_used = pend_ref[n_exp - 1] // eb
        n_blocks = xr_ref.shape[1] // (eb * spt)
        for e in range(n_exp):
            @pl.when(n_used + e < n_blocks)
            def _():
                start = pl.multiple_of((n_used + e) * (eb * spt), eb * spt)
                cp = pltpu.make_async_copy(zbuf, _tile_tokens(xr_ref, start, eb * spt), zsem)
                cp.start()
                cp.wait()

    rows = pos_ref.shape[2]

    def start(r, c):
        src = _tile_tokens(hp_ref, pl.multiple_of(r * spt, spt), spt)
        for k in range(2):
            dst = _tile_tokens(xr_ref, pl.multiple_of(pos_ref[0, k, r] * spt, spt), spt)
            pltpu.make_async_copy(src, dst, sem.at[k]).start(priority=k)
        return c

    lax.fori_loop(0, rows, start, 0, unroll=8)
    for k in range(2):
        pltpu.make_async_copy(hp_ref, _tile_tokens(xr_ref, 0, rows * spt), sem.at[k]).wait()


def _dispatch(cfg, hp, pos_blocks, pad_end, n_rows):
    tb = pos_blocks.shape[2]
    spt, npl = _token_tile(cfg.d_model)
    kern = functools.partial(_dispatch_kernel, n_exp=cfg.n_experts, eb=cfg.e_block, spt=spt)
    return pl.pallas_call(
        kern,
        grid_spec=pltpu.PrefetchScalarGridSpec(
            num_scalar_prefetch=1,
            grid=(cfg.n_tok // tb,),
            in_specs=[
                pl.BlockSpec((1, 2, tb), lambda i, pe: (i, 0, 0), memory_space=pltpu.SMEM),
                pl.BlockSpec((npl, tb * spt, LANES), lambda i, pe: (0, i, 0)),
            ],
            out_specs=pl.BlockSpec(memory_space=pl.ANY),
            scratch_shapes=[
                pltpu.VMEM((npl, cfg.e_block * spt, LANES), F32),
                pltpu.SemaphoreType.DMA((2,)),
                pltpu.SemaphoreType.DMA(()),
            ],
        ),
        out_shape=jax.ShapeDtypeStruct((npl, n_rows * spt, LANES), F32),
        compiler_params=_params(("arbitrary",)),
        name="moe_dispatch",
    )(pad_end, pos_blocks, hp)


EXPERT_CHUNK_ROWS = 128
EXPERT_RING = 6
EXPERT_VMEM_LIMIT = 60 * 1024 * 1024


def _expert_chunks(cfg):
    n_in = cfg.d_model // EXPERT_CHUNK_ROWS
    n_out = cfg.d_expert // EXPERT_CHUNK_ROWS
    return n_in, n_out, 2 * n_in + n_out


def _expert_kernel(nu_ref, e0_ref, slot_ref, ne_ref, c0_ref, c1_ref, xp_ref, wg_ref, wu_ref, wd_ref, y_ref,
                   wg_b, wu_b, wd_b, stg, sem, *, n_in, n_out, layer):
    b = pl.program_id(0)
    n_chunks = 2 * n_in + n_out
    ring = stg.shape[0]
    rows = EXPERT_CHUNK_ROWS

    parts = [(0, n_in, wg_ref, wg_b), (n_in, n_in, wu_ref, wu_b), (2 * n_in, n_out, wd_ref, wd_b)]

    def for_chunk(c, e, fn):
        s = c % ring
        for first, count, src, dst in parts:
            @pl.when((c >= first) & (c < first + count))
            def _():
                row = pl.multiple_of((c - first) * rows, rows)
                window = stg.at[s, :, pl.ds(0, dst.shape[2])]
                cp = pltpu.make_async_copy(src.at[layer, e, pl.ds(row, rows)], window, sem.at[s])
                fn(cp, dst, row, window)

    def start_chunk(c, e):
        for_chunk(c, e, lambda cp, dst, row, window: cp.start())

    def finish_chunk(c, e, dst_slot):
        def cast(cp, dst, row, window):
            cp.wait()
            dst[dst_slot, pl.ds(row, rows), :] = window[...].astype(BF16)
        for_chunk(c, e, cast)

    def load_chunks(e, dst_slot, lo, hi):
        @pl.when((lo == 0) & (hi > 0))
        def _():
            for k in range(min(ring, n_chunks)):
                start_chunk(k, e)

        def body(c, carry):
            finish_chunk(c, e, dst_slot)

            @pl.when(c + ring < n_chunks)
            def _():
                start_chunk(c + ring, e)
            return carry

        lax.fori_loop(lo, hi, body, 0)

    @pl.when(b == 0)
    def _():
        load_chunks(e0_ref[0], 0, 0, n_chunks)

    @pl.when(b < nu_ref[0])
    def _():
        slot = slot_ref[b]
        load_chunks(ne_ref[b], 1 - slot, c0_ref[b], c1_ref[b])
        d = wg_b.shape[1]
        n_chunk = d // LANES
        rt, _ = _token_tile(d)
        eb = xp_ref.shape[1] // rt
        x = jnp.concatenate([xp_ref[_tile_chunk(c, eb, rt)].astype(BF16) for c in range(n_chunk)], axis=1)
        g = jnp.dot(x, wg_b[slot], preferred_element_type=F32)
        u = jnp.dot(x, wu_b[slot], preferred_element_type=F32)
        a = (_silu(g) * u).astype(BF16)
        y = jnp.dot(a, wd_b[slot], preferred_element_type=F32)
        for c in range(n_chunk):
            y_ref[_tile_chunk(c, eb, rt)] = y[:, c * LANES:(c + 1) * LANES]

    @pl.when(b >= nu_ref[0])
    def _():
        y_ref[...] = jnp.zeros(y_ref.shape, y_ref.dtype)


def _experts(cfg, x_rows, block_e, n_used, wg, wu, wd, layer, n_blocks):
    d = cfg.d_model
    f = cfg.d_expert
    eb = cfg.e_block
    rt, npl = _token_tile(d)
    n_in, n_out, _ = _expert_chunks(cfg)
    slot, next_e, c0, c1 = _weight_schedule(cfg, block_e, n_used, n_blocks)
    any_spec = pl.BlockSpec(memory_space=pl.ANY)
    x_idx = lambda b, nu, *_: (0, jnp.minimum(b, nu[0] - 1), 0)
    return pl.pallas_call(
        functools.partial(_expert_kernel, n_in=n_in, n_out=n_out, layer=layer),
        grid_spec=pltpu.PrefetchScalarGridSpec(
            num_scalar_prefetch=6,
            grid=(n_blocks,),
            in_specs=[pl.BlockSpec((npl, eb * rt, LANES), x_idx), any_spec, any_spec, any_spec],
            out_specs=pl.BlockSpec((npl, eb * rt, LANES), lambda b, *_: (0, b, 0)),
            scratch_shapes=[
                pltpu.VMEM((2, d, f), BF16), pltpu.VMEM((2, d, f), BF16), pltpu.VMEM((2, f, d), BF16),
                pltpu.VMEM((EXPERT_RING, EXPERT_CHUNK_ROWS, max(d, f)), F32),
                pltpu.SemaphoreType.DMA((EXPERT_RING,)),
            ],
        ),
        out_shape=jax.ShapeDtypeStruct((npl, n_blocks * eb * rt, LANES), F32),
        compiler_params=pltpu.CompilerParams(dimension_semantics=("arbitrary",),
                                             vmem_limit_bytes=EXPERT_VMEM_LIMIT),
        name="moe_experts",
    )(n_used, block_e[:1], slot, next_e, c0, c1, x_rows, wg, wu, wd)


def _combine_kernel(pos_ref, nxt_ref, y_ref, x_ref, mod_ref, wt_ref, fg_ref, o_ref, ybuf, sem, *, final, eps):
    i = pl.program_id(0)
    n_steps = pl.num_programs(0)
    rows, d = x_ref.shape
    n_chunk = d // LANES
    rt, _ = _token_tile(d)
    slot = i % 2
    per_choice = rows * rt

    def gather(p_ref, s):
        def start(r, c):
            for k in range(2):
                src = _tile_tokens(y_ref, pl.multiple_of(p_ref[0, k, r] * rt, rt), rt)
                dst = _tile_tokens(ybuf.at[s], pl.multiple_of(k * per_choice + r * rt, rt), rt)
                pltpu.make_async_copy(src, dst, sem.at[s]).start(priority=k)
            return c
        lax.fori_loop(0, rows, start, 0, unroll=8)

    @pl.when(i == 0)
    def _():
        gather(pos_ref, 0)

    @pl.when(i + 1 < n_steps)
    def _():
        gather(nxt_ref, 1 - slot)

    pltpu.make_async_copy(_tile_tokens(y_ref, 0, 2 * per_choice), ybuf.at[slot], sem.at[slot]).wait()

    w = wt_ref[...]
    w0, w1 = w[:, 0:1], w[:, 1:2]
    sumsq = jnp.zeros((rows, 1), F32)
    for c in range(n_chunk):
        cols = slice(c * LANES, (c + 1) * LANES)
        moe = (w0 * ybuf[(slot,) + _tile_chunk(c, rows, rt)]
               + w1 * ybuf[(slot,) + _tile_chunk(c, rows, rt, first=per_choice)])
        xc = x_ref[:, cols] + mod_ref[0, 5:6, cols] * moe
        o_ref[:, cols] = xc
        sumsq = sumsq + jnp.sum(xc * xc, axis=-1, keepdims=True)
    if final:
        o_ref[...] = (o_ref[...] * lax.rsqrt(sumsq * (1.0 / d) + eps)) * fg_ref[...]


def _combine(cfg, y, pos_blocks, wts_t, x, mod_l, final_gain, *, row0, n_rows, final):
    d = cfg.d_model
    tb = pos_blocks.shape[2]
    rt, npl = _token_tile(d)
    b0 = row0 // tb
    mi = _merged_mod_index(cfg, tb)
    kern = functools.partial(_combine_kernel, final=final, eps=cfg.norm_eps)
    n_steps = n_rows // tb
    return pl.pallas_call(
        kern,
        grid=(n_steps,),
        in_specs=[
            pl.BlockSpec((1, 2, tb), lambda i: (b0 + i, 0, 0), memory_space=pltpu.SMEM),
            pl.BlockSpec((1, 2, tb), lambda i: (b0 + jnp.minimum(i + 1, n_steps - 1), 0, 0),
                         memory_space=pltpu.SMEM),
            pl.BlockSpec(memory_space=pl.ANY),
            pl.BlockSpec((tb, d), lambda i: (b0 + i, 0)),
            pl.BlockSpec((1, 8, d), lambda i: (mi(b0 + i), 0, 0)),
            pl.BlockSpec((tb, 2), lambda i: (b0 + i, 0)),
            pl.BlockSpec((1, d), lambda i: (0, 0)),
        ],
        out_specs=pl.BlockSpec((tb, d), lambda i: (i, 0)),
        out_shape=jax.ShapeDtypeStruct((n_rows, d), F32),
        scratch_shapes=[pltpu.VMEM((2, npl, 2 * tb * rt, LANES), F32), pltpu.SemaphoreType.DMA((2,))],
        compiler_params=_params(("arbitrary",)),
        name="moe_combine_final" if final else "moe_combine",
    )(pos_blocks, pos_blocks, y, x, mod_l, wts_t, final_gain)


def _moe(cfg, x, mod_l, gain, w_router_t, router_bias, wg, wu, wd, layer, final_gain, final):
    tb = min(256, cfg.tm)
    hp, idx, wts = _route(cfg, x, mod_l, gain, w_router_t, router_bias)
    pos, block_e, n_used, pad_end, n_blocks = _plan(cfg, idx)
    pos_blocks = pos.reshape(2, cfg.n_tok // tb, tb).transpose(1, 0, 2)
    td = cfg.tm
    x_rows = _dispatch(cfg, hp, pos.reshape(2, cfg.n_tok // td, td).transpose(1, 0, 2), pad_end,
                       n_blocks * cfg.e_block)
    y = _experts(cfg, x_rows, block_e, n_used, wg, wu, wd, layer, n_blocks)
    wts_t = wts.T
    comb = functools.partial(_combine, cfg, y, pos_blocks, wts_t, x, mod_l, final_gain, final=final)
    if final:
        return comb(row0=0, n_rows=cfg.n_p), comb(row0=cfg.n_p, n_rows=cfg.n_s)
    return comb(row0=0, n_rows=cfg.n_tok)


def _forward(cfg, x_prompt, x_sample, cache_a_k, cache_a_v, cache_b_k, cache_b_v, c, c_ctx, w_ada, b_ada,
             norm1, norm2, final_norm, w_in, w_mix_out, lambda_q1, lambda_k1, lambda_q2, lambda_k2,
             subln_gain, na_rel_bias, w_fourier_out, w_router, router_bias, w_exp_gate, w_exp_up,
             w_exp_down):
    d = cfg.d_model
    xp = x_prompt.reshape(cfg.n_p, d)
    xs = x_sample.reshape(cfg.n_s, d)

    n_cond = -(-cfg.n_mod // 8) * 8
    cond = jnp.concatenate([c_ctx[None, :], c, jnp.zeros((n_cond - cfg.n_mod, d), F32)], axis=0)
    mod = _modulation(cfg, cond, w_ada, b_ada)
    mod = mod.reshape(cfg.depth, n_cond, 6, d)[:, :cfg.n_mod]
    mod = jnp.pad(mod, ((0, 0), (0, 0), (0, 2), (0, 0)))

    w_router_t = w_router.T
    fgain = final_norm.reshape(1, d)
    x = (xp, xs)
    kv_cache = None
    for l in range(cfg.depth):
        j = l // 2
        g1 = norm1[l].reshape(1, d)
        if l % 2 == 0:
            lam_init = 0.8 - 0.6 * math.exp(-0.3 * l)
            lam_pack = jnp.zeros((8, LANES), F32).at[:4, :cfg.a_qk].set(
                jnp.stack([lambda_q1[j], lambda_k1[j], lambda_q2[j], lambda_k2[j]]))
            sgain = subln_gain[j].reshape(1, cfg.a_dim)
            w_in_b = w_in[j].astype(BF16)
            if isinstance(x, tuple):
                x_p, x_s = x
            else:
                x_p, x_s = x[:cfg.n_p], x[cfg.n_p:]
            proj_p = _projection(cfg, x_p, mod[l], g1, w_in_b, latent=False, out_dtype=F32)
            proj_s = _projection(cfg, x_s, mod[l], g1, w_in_b, latent=True, out_dtype=BF16)
            if kv_cache is None:
                kv_cache = []
            aw, bw = cfg.a_width, cfg.b_width
            cak = cache_a_k.reshape(cfg.dec_batch, -1, aw)
            cav = cache_a_v.reshape(cfg.dec_batch, -1, aw)
            cbk = cache_b_k.reshape(cfg.dec_batch, -1, bw)
            cbv = cache_b_v.reshape(cfg.dec_batch, -1, bw)
            a_p, new_ak, new_av = _diff_attention(cfg, proj_p, lam_pack, sgain, cfg.batch, cfg.seq, lam_init)
            b_p, new_bk, new_bv = _soft_attention(cfg, proj_p, cfg.batch, cfg.seq)
            kv_cache.append((new_ak, new_av, new_bk, new_bv))
            a_s = _diff_attention(cfg, proj_s, lam_pack, sgain, cfg.dec_batch, cfg.dec_seq, lam_init,
                                  ctx=(cak, cav, j))
            b_s = _na_attention(cfg, proj_s, cbk, cbv, j, na_rel_bias[j])
            x = _linear_residual(cfg, [(a_p, a_s), (b_p, b_s)], w_mix_out[j].astype(BF16), x, mod[l])
        else:
            if isinstance(x, tuple):
                x = jnp.concatenate(x, axis=0)
            y = _dft_channels(cfg, x, mod[l], g1)
            f_p = _dft_sequence(cfg, y, cfg.batch, cfg.seq, 0)
            f_s = _dft_sequence(cfg, y, cfg.dec_batch, cfg.dec_seq, cfg.n_p // cfg.dec_seq)
            x = _linear_residual(cfg, [(f_p, f_s)], w_fourier_out[j].astype(BF16), x, mod[l])
        x = _moe(cfg, x, mod[l], norm2[l].reshape(1, d), w_router_t, router_bias,
                 w_exp_gate, w_exp_up, w_exp_down, l, fgain, final=(l == cfg.depth - 1))
    y_p, y_s = x
    n_even = (cfg.depth + 1) // 2
    outs = [y_p.reshape(cfg.batch, cfg.seq, d), y_s.reshape(cfg.dec_batch, cfg.dec_seq, d)]
    for t in range(4):
        heads, hd = (cfg.a_heads, cfg.a_dim) if t < 2 else (cfg.b_heads, cfg.b_dim)
        stacked = jnp.stack([kv_cache[jj][t].reshape(cfg.batch, cfg.seq, heads, hd)
                             for jj in range(n_even)], axis=1)
        outs.append(stacked)
    return tuple(outs)


def kernel(x_prompt, x_sample, cache_a_k, cache_a_v, cache_b_k, cache_b_v, c, c_ctx, w_ada, b_ada, norm1, norm2, final_norm, w_in, w_mix_out, lambda_q1, lambda_k1, lambda_q2, lambda_k2, subln_gain, na_rel_bias, w_fourier_out, w_router, router_bias, w_exp_gate, w_exp_up, w_exp_down):
    return _forward(Cfg(), x_prompt, x_sample, cache_a_k, cache_a_v, cache_b_k, cache_b_v, c, c_ctx, w_ada,
                    b_ada, norm1, norm2, final_norm, w_in, w_mix_out, lambda_q1, lambda_k1, lambda_q2,
                    lambda_k2, subln_gain, na_rel_bias, w_fourier_out, w_router, router_bias, w_exp_gate,
                    w_exp_up, w_exp_down)
```

```python
import functools
import math
from typing import NamedTuple

import numpy as np
import jax
import jax.numpy as jnp
from jax import lax
from jax.experimental import pallas as pl
from jax.experimental.pallas import tpu as pltpu

F32 = jnp.float32
BF16 = jnp.bfloat16

LANES = 128
NEG_BIG = -1e30
VMEM_LIMIT = 56 * 1024 * 1024


class Cfg(NamedTuple):
    d_model: int = 2048
    batch: int = 16
    seq: int = 256
    depth: int = 2
    dec_batch: int = 8
    dec_seq: int = 2048
    past_len: int = 256
    grid_w: int = 64
    a_heads: int = 8
    a_qk: int = 64
    b_heads: int = 8
    b_dim: int = 128
    na_win_r: int = 8
    na_win_c: int = 16
    f_groups: int = 4
    n_experts: int = 16
    n_groups: int = 4
    d_expert: int = 1408
    rope_theta: float = 10000.0
    norm_eps: float = 1e-6
    subln_eps: float = 1e-5
    row_tile: int = 512
    proj_tile: int = 1024
    q_tile: int = 1024
    e_block: int = 256

    @property
    def a_dim(self):
        return 2 * self.a_qk

    @property
    def a_width(self):
        return self.a_heads * self.a_dim

    @property
    def b_width(self):
        return self.b_heads * self.b_dim

    @property
    def n_p(self):
        return self.batch * self.seq

    @property
    def n_s(self):
        return self.dec_batch * self.dec_seq

    @property
    def n_tok(self):
        return self.n_p + self.n_s

    @property
    def tm(self):
        return min(self.row_tile, self.n_p, self.dec_seq)

    @property
    def n_mod(self):
        return 1 + self.dec_batch


def _params(sem):
    return pltpu.CompilerParams(dimension_semantics=sem, vmem_limit_bytes=VMEM_LIMIT)


def _norm_mod(x, g, sh, sc, eps):
    ms = jnp.mean(x * x, axis=-1, keepdims=True)
    y = x * lax.rsqrt(ms + eps)
    return (y * g) * (1.0 + sc) + sh


def _silu(x):
    return x * jax.nn.sigmoid(x)


def _merged_mod_index(cfg, tm):
    def f(i):
        r = i * tm
        return jnp.where(r < cfg.n_p, 0, 1 + (r - cfg.n_p) // cfg.dec_seq)
    return f


def _ada_kernel(cond_ref, w_ref, b_ref, o_ref):
    s = _silu(cond_ref[...])
    s_hi = s.astype(BF16)
    s_lo = (s - s_hi.astype(F32)).astype(BF16)
    lhs = jnp.concatenate([s_hi, s_lo], axis=0)
    r = jnp.dot(lhs, w_ref[0].astype(BF16), preferred_element_type=F32)
    n = s.shape[0]
    o_ref[0] = r[:n] + r[n:] + b_ref[0]


def _modulation(cfg, cond, w_ada, b_ada):
    d = cfg.d_model
    r = cond.shape[0]
    tn = math.gcd(1024, 6 * d)
    return pl.pallas_call(
        _ada_kernel,
        grid=(cfg.depth, 6 * d // tn),
        in_specs=[
            pl.BlockSpec((r, d), lambda l, j: (0, 0)),
            pl.BlockSpec((1, d, tn), lambda l, j: (l, 0, j)),
            pl.BlockSpec((1, 1, tn), lambda l, j: (l, 0, j)),
        ],
        out_specs=pl.BlockSpec((1, r, tn), lambda l, j: (l, 0, j)),
        out_shape=jax.ShapeDtypeStruct((cfg.depth, r, 6 * d), F32),
        compiler_params=_params(("arbitrary", "arbitrary")),
        name="ada_modulation",
    )(cond, w_ada, b_ada.reshape(cfg.depth, 1, 6 * d))


PROJ_CHUNK = 256


def _proj_kernel(*refs, rope, n_rope_blocks, eps):
    if rope:
        x_ref, mod_ref, g_ref, w_ref, cos_ref, sa_ref, sb_ref, o_ref, h_ref = refs
    else:
        x_ref, mod_ref, g_ref, w_ref, o_ref, h_ref = refs
    j = pl.program_id(1)

    @pl.when(j == 0)
    def _():
        h = _norm_mod(x_ref[...], g_ref[...], mod_ref[0, 0:1, :], mod_ref[0, 1:2, :], eps)
        h_ref[...] = h.astype(BF16)

    if not rope:
        o_ref[...] = jnp.dot(h_ref[...], w_ref[...], preferred_element_type=F32).astype(o_ref.dtype)
        return

    @pl.when(j < n_rope_blocks)
    def _():
        cos, sa, sb = cos_ref[...], sa_ref[...], sb_ref[...]
        h = h_ref[...]
        tn = w_ref.shape[1]
        cw = min(PROJ_CHUNK, tn)
        for c0 in range(0, tn, cw):
            acc = jnp.dot(h, w_ref[:, c0:c0 + cw], preferred_element_type=F32)
            for c in range(cw // LANES):
                xa = acc[:, c * LANES:(c + 1) * LANES]
                up = pltpu.roll(xa, LANES - 16, 1)
                dn = pltpu.roll(xa, 16, 1)
                lo = c0 + c * LANES
                o_ref[:, lo:lo + LANES] = (xa * cos + up * sa + dn * sb).astype(o_ref.dtype)

    @pl.when(j >= n_rope_blocks)
    def _():
        o_ref[...] = jnp.dot(h_ref[...], w_ref[...], preferred_element_type=F32).astype(o_ref.dtype)


def _rope_tables(cfg):
    t = np.arange(cfg.dec_seq)
    row = (t // cfg.grid_w).astype(np.float64)
    col = (t % cfg.grid_w).astype(np.float64)
    lane = np.arange(LANES)
    l64 = lane % cfg.a_qk
    half = cfg.a_qk // 4
    freq = cfg.rope_theta ** (-(lane % half).astype(np.float64) / half)
    pos = np.where((l64 < cfg.a_qk // 2)[None, :], row[:, None], col[:, None])
    ang = pos * freq[None, :]
    first = (lane % (2 * half)) < half
    cos = np.cos(ang)
    sin = np.sin(ang)
    sa = np.where(first[None, :], -sin, 0.0)
    sb = np.where(first[None, :], 0.0, sin)
    return tuple(jnp.asarray(a, dtype=F32) for a in (cos, sa, sb))


def _projection(cfg, x, mod_l, gain, w, *, latent, out_dtype):
    m, d = x.shape
    n = w.shape[1]
    tm = min(cfg.proj_tile, m, cfg.dec_seq)
    tn = min(1024, cfg.a_width)
    per_seq = cfg.dec_seq // tm if latent else 1
    mod_idx = (lambda i, j: (1 + i // per_seq, 0, 0)) if latent else (lambda i, j: (0, 0, 0))
    in_specs = [
        pl.BlockSpec((tm, d), lambda i, j: (i, 0)),
        pl.BlockSpec((1, 8, d), mod_idx),
        pl.BlockSpec((1, d), lambda i, j: (0, 0)),
        pl.BlockSpec((d, tn), lambda i, j: (0, j)),
    ]
    args = [x, mod_l, gain, w]
    if latent:
        tab_spec = pl.BlockSpec((tm, LANES), lambda i, j: (i % per_seq, 0))
        in_specs += [tab_spec, tab_spec, tab_spec]
        args += list(_rope_tables(cfg))
    kern = functools.partial(_proj_kernel, rope=latent, n_rope_blocks=2 * cfg.a_width // tn,
                             eps=cfg.norm_eps)
    return pl.pallas_call(
        kern,
        grid=(m // tm, n // tn),
        in_specs=in_specs,
        out_specs=pl.BlockSpec((tm, tn), lambda i, j: (i, j)),
        out_shape=jax.ShapeDtypeStruct((m, n), out_dtype),
        scratch_shapes=[pltpu.VMEM((tm, d), BF16)],
        compiler_params=_params(("arbitrary", "arbitrary")),
        name="qkv_projection_latent" if latent else "qkv_projection_context",
    )(*args)


def _nt_dot(a, b):
    return lax.dot_general(a, b, (((1,), (1,)), ((), ())), preferred_element_type=F32)


DIFF_SUB_TILE = 128


def _diff_attn_kernel(*refs, has_ctx, lam_init, eps, qk):
    if has_ctx:
        lam_ref, gain_ref, q_ref, k_ref, v_ref, ck_ref, cv_ref, o_ref = refs
    else:
        lam_ref, gain_ref, q_ref, k_ref, v_ref, o_ref, ko_ref, vo_ref = refs
        ko_ref[...] = k_ref[...]
        vo_ref[...] = v_ref[...]
    lv = lam_ref[...]
    s1 = jnp.sum(lv[0:1] * lv[1:2], axis=-1, keepdims=True)
    s2 = jnp.sum(lv[2:3] * lv[3:4], axis=-1, keepdims=True)
    lam = jnp.exp(s1) - jnp.exp(s2) + lam_init

    hd = 2 * qk
    tq = q_ref.shape[0]
    ts = min(tq, DIFF_SUB_TILE)
    for hh, t in [(hh, t) for hh in range(q_ref.shape[1] // hd) for t in range(tq // ts)]:
        cols = slice(hh * hd, (hh + 1) * hd)
        k = k_ref[:, cols].astype(BF16)
        v = v_ref[:, cols].astype(BF16)
        if has_ctx:
            ck = ck_ref[0, :, cols].astype(BF16)
            cv = cv_ref[0, :, cols].astype(BF16)
        q = q_ref[t * ts:(t + 1) * ts, cols].astype(F32) * (qk ** -0.5 * math.log2(math.e))
        lane = lax.broadcasted_iota(jnp.int32, q.shape, 1)
        qs = jnp.concatenate([jnp.where(lane < qk, q, 0.0), jnp.where(lane >= qk, q, 0.0)],
                             axis=0).astype(BF16)
        s_new = _nt_dot(qs, k)
        m = jnp.max(s_new, axis=-1, keepdims=True)
        if has_ctx:
            s_ctx = _nt_dot(qs, ck)
            m = jnp.maximum(m, jnp.max(s_ctx, axis=-1, keepdims=True))
        p_new = jnp.exp2(s_new - m)
        den = jnp.sum(p_new, axis=-1, keepdims=True)
        o2 = jnp.dot(p_new.astype(BF16), v, preferred_element_type=F32)
        if has_ctx:
            p_ctx = jnp.exp2(s_ctx - m)
            den = den + jnp.sum(p_ctx, axis=-1, keepdims=True)
            o2 = o2 + jnp.dot(p_ctx.astype(BF16), cv, preferred_element_type=F32)
        inv = 1.0 / den
        o = o2[:ts] * inv[:ts] - o2[ts:] * (lam * inv[ts:])
        ms = jnp.mean(o * o, axis=-1, keepdims=True)
        o = (o * lax.rsqrt(ms + eps)) * gain_ref[...]
        o_ref[t * ts:(t + 1) * ts, cols] = (o * (1.0 - lam_init)).astype(o_ref.dtype)


def _diff_attention(cfg, proj, lam_pack, gain, n_batch, seq_len, lam_init, ctx=None):
    hd = cfg.a_dim
    hps = cfg.a_heads if ctx is None else 1
    nh = cfg.a_heads // hps
    bw = hps * hd
    tq = min(cfg.q_tile, seq_len)
    nq = seq_len // tq
    in_specs = [
        pl.BlockSpec((8, LANES), lambda b, h, qi: (0, 0)),
        pl.BlockSpec((1, hd), lambda b, h, qi: (0, 0)),
        pl.BlockSpec((tq, bw), lambda b, h, qi: (b * nq + qi, h)),
        pl.BlockSpec((seq_len, bw), lambda b, h, qi: (b, nh + h)),
        pl.BlockSpec((seq_len, bw), lambda b, h, qi: (b, 2 * nh + h)),
    ]
    args = [lam_pack, gain, proj, proj, proj]
    if ctx is not None:
        ck, cv = ctx
        past = ck.shape[1]
        cspec = pl.BlockSpec((1, past, bw), lambda b, h, qi: (b, 0, h))
        in_specs += [cspec, cspec]
        args += [ck, cv]
    kern = functools.partial(_diff_attn_kernel, has_ctx=ctx is not None, lam_init=lam_init,
                             eps=cfg.subln_eps, qk=cfg.a_qk)
    out_specs = pl.BlockSpec((tq, bw), lambda b, h, qi: (b * nq + qi, h))
    out_shape = jax.ShapeDtypeStruct((n_batch * seq_len, cfg.a_width), BF16)
    if ctx is None:
        assert nq == 1 and proj.dtype == F32
        kv_spec = pl.BlockSpec((seq_len, bw), lambda b, h, qi: (b, h))
        kv_shape = jax.ShapeDtypeStruct((n_batch * seq_len, cfg.a_width), F32)
        out_specs, out_shape = [out_specs, kv_spec, kv_spec], [out_shape, kv_shape, kv_shape]
    return pl.pallas_call(
        kern,
        grid=(n_batch, nh, nq),
        in_specs=in_specs,
        out_specs=out_specs,
        out_shape=out_shape,
        compiler_params=_params(("arbitrary", "arbitrary", "arbitrary")),
        name="diff_attention_latent" if ctx is not None else "diff_attention_context",
    )(*args)


def _soft_attn_kernel(q_ref, k_ref, v_ref, o_ref, ko_ref, vo_ref, *, hd):
    ko_ref[...] = k_ref[...]
    vo_ref[...] = v_ref[...]
    for hh in range(q_ref.shape[1] // hd):
        cols = slice(hh * hd, (hh + 1) * hd)
        s = _nt_dot(q_ref[:, cols].astype(BF16), k_ref[:, cols].astype(BF16)) * (hd ** -0.5 * math.log2(math.e))
        m = jnp.max(s, axis=-1, keepdims=True)
        p = jnp.exp2(s - m)
        inv = 1.0 / jnp.sum(p, axis=-1, keepdims=True)
        o = jnp.dot(p.astype(BF16), v_ref[:, cols].astype(BF16), preferred_element_type=F32) * inv
        o_ref[:, cols] = o.astype(o_ref.dtype)


def _soft_attention(cfg, proj, n_batch, seq_len):
    bw = cfg.b_width
    assert (3 * cfg.a_width) % bw == 0
    base = 3 * cfg.a_width // bw
    return pl.pallas_call(
        functools.partial(_soft_attn_kernel, hd=cfg.b_dim),
        grid=(n_batch,),
        in_specs=[
            pl.BlockSpec((seq_len, bw), lambda b: (b, base)),
            pl.BlockSpec((seq_len, bw), lambda b: (b, base + 1)),
            pl.BlockSpec((seq_len, bw), lambda b: (b, base + 2)),
        ],
        out_specs=[pl.BlockSpec((seq_len, bw), lambda b: (b, 0))] * 3,
        out_shape=[jax.ShapeDtypeStruct((n_batch * seq_len, bw), BF16),
                   jax.ShapeDtypeStruct((n_batch * seq_len, bw), proj.dtype),
                   jax.ShapeDtypeStruct((n_batch * seq_len, bw), proj.dtype)],
        compiler_params=_params(("arbitrary",)),
        name="softmax_attention_context",
    )(proj, proj, proj)


NA_GROUP_ROWS = 4


def _na_geometry(cfg):
    rows = cfg.dec_seq // cfg.grid_w
    kr = min(cfg.na_win_r, rows)
    grp = min(NA_GROUP_ROWS, rows)
    union = min(rows, kr + grp - 1 + (kr + grp - 1) % 2)
    starts = []
    for g in range(rows // grp):
        rs0 = min(max(g * grp - kr // 2, 0), rows - kr)
        starts.append(min(rs0, rows - union))
    return rows, kr, grp, union, starts


def _na_kernel(ws_ref, plane_ref, q_ref, k_ref, v_ref, ck_ref, cv_ref, tab_ref, o_ref, sctx, pctx, oacc, *,
               n_groups, grp, union, gw, scale):
    gq, uk = grp * gw, union * gw
    ck = ck_ref[0].astype(BF16)
    cv = cv_ref[0].astype(BF16)
    sctx[...] = _nt_dot(q_ref[...], ck) * scale
    first_half = lax.broadcasted_iota(jnp.int32, (gw, LANES), 1) < gw

    def body(g, carry):
        q0 = pl.multiple_of(g * gq, gq)
        k0 = pl.multiple_of(ws_ref[g] * gw, gw)
        q = q_ref[pl.ds(q0, gq), :]
        bias_rows = []
        for a in range(grp):
            base = (g * grp + a) * union
            tiles = [jnp.where(first_half, tab_ref[0, plane_ref[base + 2 * jj]],
                               tab_ref[0, plane_ref[base + 2 * jj + 1]]) for jj in range(union // 2)]
            bias_rows.append(jnp.concatenate(tiles, axis=1))
        bias = jnp.concatenate(bias_rows, axis=0)
        s = _nt_dot(q, k_ref[pl.ds(k0, uk), :]) * scale + bias
        sc = sctx[pl.ds(q0, gq), :]
        m = jnp.maximum(jnp.max(s, axis=-1, keepdims=True), jnp.max(sc, axis=-1, keepdims=True))
        p = jnp.exp2(s - m)
        pc = jnp.exp2(sc - m)
        inv = 1.0 / (jnp.sum(p, axis=-1, keepdims=True) + jnp.sum(pc, axis=-1, keepdims=True))
        oacc[pl.ds(q0, gq), :] = jnp.dot(p.astype(BF16), v_ref[pl.ds(k0, uk), :],
                                         preferred_element_type=F32) * inv
        pctx[pl.ds(q0, gq), :] = (pc * inv).astype(BF16)
        return carry

    lax.fori_loop(0, n_groups, body, 0, unroll=2)
    o_ref[...] = (oacc[...] + jnp.dot(pctx[...], cv, preferred_element_type=F32)).astype(o_ref.dtype)


def _na_bias_table(cfg, rpb):
    w = cfg.grid_w
    assert LANES == 2 * w
    rows, kr, grp, union, starts = _na_geometry(cfg)
    qc = np.arange(w)
    kc = np.arange(w)
    cs = np.clip(qc - cfg.na_win_c // 2, 0, w - cfg.na_win_c)
    col_mask = (kc[None, :] >= cs[:, None]) & (kc[None, :] < cs[:, None] + cfg.na_win_c)
    col_idx = np.clip(kc[None, :] - qc[:, None] + cfg.na_win_c - 1, 0, 2 * cfg.na_win_c - 2)
    n_c = 2 * cfg.na_win_c - 1
    onehot = (col_idx[None] == np.arange(n_c)[:, None, None]) & col_mask[None]
    t = jnp.einsum('hrc,cqk->hrqk', rpb.astype(F32), jnp.asarray(onehot, dtype=F32),
                   precision=lax.Precision.HIGHEST)
    t = jnp.where(jnp.asarray(col_mask)[None, None], t * math.log2(math.e), NEG_BIG)
    n_r = 2 * cfg.na_win_r - 1
    t = jnp.concatenate([t, jnp.full((cfg.b_heads, 1, w, w), NEG_BIG, F32)], axis=1)
    plane = np.full((len(starts), grp, union), n_r, np.int32)
    for g, ws in enumerate(starts):
        for a in range(grp):
            r = g * grp + a
            rs = min(max(r - kr // 2, 0), rows - kr)
            for j in range(union):
                if rs <= ws + j < rs + kr:
                    plane[g, a, j] = ws + j - r + cfg.na_win_r - 1
    return jnp.concatenate([t, t], axis=-1), jnp.asarray(plane.reshape(-1))


def _na_attention(cfg, proj, ck, cv, rpb):
    hd = cfg.b_dim
    nh = cfg.b_heads
    n = cfg.dec_seq
    rows, kr, grp, union, starts = _na_geometry(cfg)
    n_groups = len(starts)
    base = 3 * cfg.a_width // hd
    past = ck.shape[1]
    table, plane = _na_bias_table(cfg, rpb)
    kern = functools.partial(_na_kernel, n_groups=n_groups, grp=grp, union=union, gw=cfg.grid_w,
                             scale=hd ** -0.5 * math.log2(math.e))
    cspec = pl.BlockSpec((1, past, hd), lambda h, b, *_: (b, 0, h))
    return pl.pallas_call(
        kern,
        grid_spec=pltpu.PrefetchScalarGridSpec(
            num_scalar_prefetch=2,
            grid=(nh, cfg.dec_batch),
            in_specs=[
                pl.BlockSpec((n, hd), lambda h, b, *_: (b, base + h)),
                pl.BlockSpec((n, hd), lambda h, b, *_: (b, base + nh + h)),
                pl.BlockSpec((n, hd), lambda h, b, *_: (b, base + 2 * nh + h)),
                cspec, cspec,
                pl.BlockSpec((1,) + table.shape[1:], lambda h, b, *_: (h, 0, 0, 0)),
            ],
            out_specs=pl.BlockSpec((n, hd), lambda h, b, *_: (b, h)),
            scratch_shapes=[pltpu.VMEM((n, past), F32), pltpu.VMEM((n, past), BF16), pltpu.VMEM((n, hd), F32)],
        ),
        out_shape=jax.ShapeDtypeStruct((cfg.n_s, cfg.b_width), BF16),
        compiler_params=_params(("arbitrary", "arbitrary")),
        name="neighbourhood_attention",
    )(jnp.asarray(np.asarray(starts, np.int32)), plane, proj, proj, proj, ck, cv, table)


def _linres_kernel(*refs, n_parts, n_pb, x_split):
    i = pl.program_id(0)
    is_p = i < n_pb
    pos = 0
    acc = None
    w_ref = refs[2 * n_parts]
    k0 = 0
    for p in range(n_parts):
        a_p, a_s = refs[2 * p], refs[2 * p + 1]
        a = jnp.where(is_p, a_p[...], a_s[...])
        kk = a.shape[1]
        part = jnp.dot(a, w_ref[k0:k0 + kk, :], preferred_element_type=F32)
        acc = part if acc is None else acc + part
        k0 += kk
    pos = 2 * n_parts + 1
    if x_split:
        x = jnp.where(is_p, refs[pos][...], refs[pos + 1][...])
        pos += 2
    else:
        x = refs[pos][...]
        pos += 1
    mod_ref, o_ref = refs[pos], refs[pos + 1]
    o_ref[...] = x + mod_ref[0, 2:3, :] * acc


def _linear_residual(cfg, parts, w, x, mod_l):
    d = cfg.d_model
    tm = min(256, cfg.tm)
    n_pb = cfg.n_p // tm
    n_sb = cfg.n_s // tm
    p_idx = lambda i: (jnp.minimum(i, n_pb - 1), 0)
    s_idx = lambda i: (jnp.maximum(i - n_pb, 0), 0)
    in_specs, args = [], []
    for a_p, a_s in parts:
        kk = a_p.shape[1]
        in_specs += [pl.BlockSpec((tm, kk), p_idx), pl.BlockSpec((tm, kk), s_idx)]
        args += [a_p, a_s]
    in_specs.append(pl.BlockSpec(w.shape, lambda i: (0, 0)))
    args.append(w)
    x_split = isinstance(x, tuple)
    if x_split:
        in_specs += [pl.BlockSpec((tm, d), p_idx), pl.BlockSpec((tm, d), s_idx)]
        args += list(x)
    else:
        in_specs.append(pl.BlockSpec((tm, d), lambda i: (i, 0)))
        args.append(x)
    mi = _merged_mod_index(cfg, tm)
    in_specs.append(pl.BlockSpec((1, 8, d), lambda i: (mi(i), 0, 0)))
    args.append(mod_l)
    kern = functools.partial(_linres_kernel, n_parts=len(parts), n_pb=n_pb, x_split=x_split)
    return pl.pallas_call(
        kern,
        grid=(n_pb + n_sb,),
        in_specs=in_specs,
        out_specs=pl.BlockSpec((tm, d), lambda i: (i, 0)),
        out_shape=jax.ShapeDtypeStruct((cfg.n_tok, d), F32),
        compiler_params=_params(("arbitrary",)),
        name="linear_gated_residual",
    )(*args)


def _dft_chan_kernel(x_ref, mod_ref, g_ref, cs_ref, y_ref, *, groups, eps):
    h = _norm_mod(x_ref[...], g_ref[...], mod_ref[0, 0:1, :], mod_ref[0, 1:2, :], eps).astype(BF16)
    gd = h.shape[1] // groups
    for g in range(groups):
        r = jnp.dot(h[:, g * gd:(g + 1) * gd], cs_ref[...], preferred_element_type=F32)
        y_ref[0, :, g * gd:(g + 1) * gd] = r[:, :gd].astype(BF16)
        y_ref[1, :, g * gd:(g + 1) * gd] = r[:, gd:].astype(BF16)


def _dft_mats(n):
    k = np.arange(n)
    ang = 2.0 * np.pi * ((k[:, None] * k[None, :]) % n) / n
    return np.cos(ang), np.sin(ang)


def _dft_channels(cfg, x, mod_l, gain):
    d = cfg.d_model
    gd = d // cfg.f_groups
    tm = cfg.tm
    c, s = _dft_mats(gd)
    cs = jnp.asarray(np.concatenate([c, s], axis=1), dtype=F32).astype(BF16)
    mi = _merged_mod_index(cfg, tm)
    return pl.pallas_call(
        functools.partial(_dft_chan_kernel, groups=cfg.f_groups, eps=cfg.norm_eps),
        grid=(cfg.n_tok // tm,),
        in_specs=[
            pl.BlockSpec((tm, d), lambda i: (i, 0)),
            pl.BlockSpec((1, 8, d), lambda i: (mi(i), 0, 0)),
            pl.BlockSpec((1, d), lambda i: (0, 0)),
            pl.BlockSpec((gd, 2 * gd), lambda i: (0, 0)),
        ],
        out_specs=pl.BlockSpec((2, tm, d), lambda i: (0, i, 0)),
        out_shape=jax.ShapeDtypeStruct((2, cfg.n_tok, d), BF16),
        compiler_params=_params(("arbitrary",)),
        name="dft_channels",
    )(x, mod_l, gain, cs)


def _dft_seq_kernel(w_ref, y_ref, o_ref, *, scale):
    acc = (jnp.dot(w_ref[0], y_ref[0], preferred_element_type=F32)
           + jnp.dot(w_ref[1], y_ref[1], preferred_element_type=F32))
    o_ref[...] = (acc * scale).astype(o_ref.dtype)


def _dft_sequence(cfg, y, n_batch, seq_len, first_block):
    d = cfg.d_model
    c, s = _dft_mats(seq_len)
    wm = jnp.asarray(np.stack([c, -s]), dtype=F32).astype(BF16)
    tml = min(1024, seq_len)
    tn = min(512, d)
    nm = seq_len // tml
    scale = 1.0 / math.sqrt(seq_len * (d // cfg.f_groups))
    return pl.pallas_call(
        functools.partial(_dft_seq_kernel, scale=scale),
        grid=(n_batch, nm, d // tn),
        in_specs=[
            pl.BlockSpec((2, tml, seq_len), lambda b, mi, j: (0, mi, 0)),
            pl.BlockSpec((2, seq_len, tn), lambda b, mi, j: (0, first_block + b, j)),
        ],
        out_specs=pl.BlockSpec((tml, tn), lambda b, mi, j: (b * nm + mi, j)),
        out_shape=jax.ShapeDtypeStruct((n_batch * seq_len, d), BF16),
        compiler_params=_params(("arbitrary", "arbitrary", "arbitrary")),
        name="dft_sequence_%d" % seq_len,
    )(wm, y)


TOKEN_TILE_ROWS = 8


def _token_tile(d):
    rows = min(TOKEN_TILE_ROWS, d // LANES)
    return rows, d // (rows * LANES)


def _tile_chunk(c, n_tok, rows, first=0):
    return c // rows, pl.ds(first + c % rows, n_tok, stride=rows), slice(None)


def _tile_tokens(ref, first_row, n_rows):
    return ref.at[:, pl.ds(first_row, n_rows), :]


def _route_kernel(x_ref, mod_ref, g_ref, wr_ref, rb_ref, hp_ref, idx_ref, wt_ref, *, eps, n_exp, per_grp):
    h = _norm_mod(x_ref[...], g_ref[...], mod_ref[0, 3:4, :], mod_ref[0, 4:5, :], eps)
    tm, d = h.shape
    rt, _ = _token_tile(d)
    for c in range(d // LANES):
        hp_ref[_tile_chunk(c, tm, rt)] = h[:, c * LANES:(c + 1) * LANES]

    h_hi = h.astype(BF16)
    h_lo = (h - h_hi.astype(F32)).astype(BF16)
    w = wr_ref[...]
    w_hi = w.astype(BF16)
    w_lo = (w - w_hi.astype(F32)).astype(BF16)
    logits = _nt_dot(w_hi, h_hi) + (_nt_dot(w_hi, h_lo) + _nt_dot(w_lo, h_hi))
    scores = jax.nn.sigmoid(logits)
    sel = scores + rb_ref[...]
    n_grp = n_exp // per_grp
    best = None
    gi = None
    for g in range(n_grp):
        v = [sel[g * per_grp + k:g * per_grp + k + 1, :] for k in range(per_grp)]
        gs = None
        for a in range(per_grp):
            for b in range(a + 1, per_grp):
                ps = v[a] + v[b]
                gs = ps if gs is None else jnp.maximum(gs, ps)
        if best is None:
            best, gi = gs, jnp.zeros(gs.shape, jnp.int32)
        else:
            better = gs > best
            gi = jnp.where(better, g, gi)
            best = jnp.where(better, gs, best)
    row = lax.broadcasted_iota(jnp.int32, sel.shape, 0)
    masked = jnp.where(row // per_grp == gi, sel, -jnp.inf)
    m1 = jnp.max(masked, axis=0, keepdims=True)
    i1 = jnp.min(jnp.where(masked == m1, row, n_exp), axis=0, keepdims=True)
    masked2 = jnp.where(row == i1, -jnp.inf, masked)
    m2 = jnp.max(masked2, axis=0, keepdims=True)
    i2 = jnp.min(jnp.where(masked2 == m2, row, n_exp), axis=0, keepdims=True)
    w1 = jnp.sum(jnp.where(row == i1, scores, 0.0), axis=0, keepdims=True)
    w2 = jnp.sum(jnp.where(row == i2, scores, 0.0), axis=0, keepdims=True)
    inv = 1.0 / (w1 + w2)
    idx_ref[...] = jnp.concatenate([i1, i2], axis=0)
    wt_ref[...] = jnp.concatenate([w1 * inv, w2 * inv], axis=0)


def _route(cfg, x, mod_l, gain, w_router_t, router_bias):
    d = cfg.d_model
    tm = cfg.tm
    t = cfg.n_tok
    rt, npl = _token_tile(d)
    mi = _merged_mod_index(cfg, tm)
    kern = functools.partial(_route_kernel, eps=cfg.norm_eps, n_exp=cfg.n_experts,
                             per_grp=cfg.n_experts // cfg.n_groups)
    return pl.pallas_call(
        kern,
        grid=(t // tm,),
        in_specs=[
            pl.BlockSpec((tm, d), lambda i: (i, 0)),
            pl.BlockSpec((1, 8, d), lambda i: (mi(i), 0, 0)),
            pl.BlockSpec((1, d), lambda i: (0, 0)),
            pl.BlockSpec((cfg.n_experts, d), lambda i: (0, 0)),
            pl.BlockSpec((cfg.n_experts, 1), lambda i: (0, 0)),
        ],
        out_specs=[
            pl.BlockSpec((npl, tm * rt, LANES), lambda i: (0, i, 0)),
            pl.BlockSpec((2, tm), lambda i: (0, i)),
            pl.BlockSpec((2, tm), lambda i: (0, i)),
        ],
        out_shape=[
            jax.ShapeDtypeStruct((npl, t * rt, LANES), F32),
            jax.ShapeDtypeStruct((2, t), jnp.int32),
            jax.ShapeDtypeStruct((2, t), F32),
        ],
        compiler_params=_params(("arbitrary",)),
        name="moe_route",
    )(x, mod_l, gain, w_router_t, router_bias.reshape(cfg.n_experts, 1))


def _plan(cfg, idx):
    t = cfg.n_tok
    eb = cfg.e_block
    ne = cfg.n_experts
    n_blocks = -(-(2 * t + ne * (eb - 1)) // eb)
    e_flat = idx.reshape(-1)
    onehot = (e_flat[:, None] == jnp.arange(ne, dtype=jnp.int32)[None, :]).astype(jnp.int32)
    csum = jnp.cumsum(onehot, axis=0)
    rank = jnp.sum(onehot * (csum - 1), axis=1)
    counts = csum[-1]
    padded = (counts + eb - 1) // eb * eb
    pad_end = jnp.cumsum(padded)
    pad_start = pad_end - padded
    pos = (pad_start[e_flat] + rank).astype(jnp.int32).reshape(2, t)
    n_used = (pad_end[-1] // eb).astype(jnp.int32)
    blk = jnp.arange(n_blocks, dtype=jnp.int32)
    blk = jnp.minimum(blk, n_used - 1)
    first_e = jnp.sum((pad_end[None, :] <= (blk * eb)[:, None]).astype(jnp.int32), axis=1)
    block_e = jnp.minimum(first_e, ne - 1).astype(jnp.int32)
    return pos, block_e, n_used.reshape(1), pad_end.astype(jnp.int32), n_blocks


def _weight_schedule(cfg, block_e, n_used, n_blocks):
    n_chunks = _expert_chunks(cfg)[2]
    big = n_blocks + 1
    idx = jnp.arange(n_blocks, dtype=jnp.int32)
    valid = idx < n_used[0]
    prev_e = jnp.concatenate([jnp.full((1,), -1, jnp.int32), block_e[:-1]])
    change = (block_e != prev_e) & valid
    earlier = (idx[None, :] <= idx[:, None]) & change[None, :]
    later = (idx[None, :] > idx[:, None]) & change[None, :]
    run_start = jnp.max(jnp.where(earlier, idx[None, :], 0), axis=1)
    next_start = jnp.minimum(jnp.min(jnp.where(later, idx[None, :], big), axis=1), n_used[0])
    slot = (jnp.sum(earlier.astype(jnp.int32), axis=1) - 1) % 2
    has_next = valid & (next_start < n_used[0])
    next_e = jnp.where(has_next, block_e[jnp.minimum(next_start, n_blocks - 1)], 0)
    run_len = jnp.maximum(next_start - run_start, 1)
    per_block = (n_chunks + run_len - 1) // run_len
    i = idx - run_start
    c0 = jnp.where(has_next, jnp.minimum(i * per_block, n_chunks), 0)
    c1 = jnp.where(has_next, jnp.minimum((i + 1) * per_block, n_chunks), 0)
    return tuple(a.astype(jnp.int32) for a in (slot, next_e, c0, c1))


def _dispatch_kernel(pend_ref, pos_ref, hp_ref, xr_ref, zbuf, sem, zsem, *, n_exp, eb, spt):
    i = pl.program_id(0)

    @pl.when(i == 0)
    def _():
        zbuf[...] = jnp.zeros(zbuf.shape, zbuf.dtype)
        for e in range(n_exp):
            start = pl.multiple_of(jnp.maximum(pend_ref[e] - eb, 0) * spt, eb * spt)
            cp = pltpu.make_async_copy(zbuf, _tile_tokens(xr_ref, start, eb * spt), zsem)
            cp.start()
            cp.wait()
        n_used = pend_ref[n_exp - 1] // eb
        n_blocks = xr_ref.shape[1] // (eb * spt)
        for e in range(n_exp):
            @pl.when(n_used + e < n_blocks)
            def _():
                start = pl.multiple_of((n_used + e) * (eb * spt), eb * spt)
                cp = pltpu.make_async_copy(zbuf, _tile_tokens(xr_ref, start, eb * spt), zsem)
                cp.start()
                cp.wait()

    rows = pos_ref.shape[2]

    def start(r, c):
        src = _tile_tokens(hp_ref, pl.multiple_of(r * spt, spt), spt)
        for k in range(2):
            dst = _tile_tokens(xr_ref, pl.multiple_of(pos_ref[0, k, r] * spt, spt), spt)
            pltpu.make_async_copy(src, dst, sem.at[k]).start(priority=k)
        return c

    lax.fori_loop(0, rows, start, 0, unroll=8)
    for k in range(2):
        pltpu.make_async_copy(hp_ref, _tile_tokens(xr_ref, 0, rows * spt), sem.at[k]).wait()


def _dispatch(cfg, hp, pos_blocks, pad_end, n_rows):
    tb = pos_blocks.shape[2]
    spt, npl = _token_tile(cfg.d_model)
    kern = functools.partial(_dispatch_kernel, n_exp=cfg.n_experts, eb=cfg.e_block, spt=spt)
    return pl.pallas_call(
        kern,
        grid_spec=pltpu.PrefetchScalarGridSpec(
            num_scalar_prefetch=1,
            grid=(cfg.n_tok // tb,),
            in_specs=[
                pl.BlockSpec((1, 2, tb), lambda i, pe: (i, 0, 0), memory_space=pltpu.SMEM),
                pl.BlockSpec((npl, tb * spt, LANES), lambda i, pe: (0, i, 0)),
            ],
            out_specs=pl.BlockSpec(memory_space=pl.ANY),
            scratch_shapes=[
                pltpu.VMEM((npl, cfg.e_block * spt, LANES), F32),
                pltpu.SemaphoreType.DMA((2,)),
                pltpu.SemaphoreType.DMA(()),
            ],
        ),
        out_shape=jax.ShapeDtypeStruct((npl, n_rows * spt, LANES), F32),
        compiler_params=_params(("arbitrary",)),
        name="moe_dispatch",
    )(pad_end, pos_blocks, hp)


EXPERT_CHUNK_ROWS = 128
EXPERT_RING = 6
EXPERT_VMEM_LIMIT = 60 * 1024 * 1024


def _expert_chunks(cfg):
    n_in = cfg.d_model // EXPERT_CHUNK_ROWS
    n_out = cfg.d_expert // EXPERT_CHUNK_ROWS
    return n_in, n_out, 2 * n_in + n_out


def _expert_kernel(nu_ref, e0_ref, slot_ref, ne_ref, c0_ref, c1_ref, xp_ref, wg_ref, wu_ref, wd_ref, y_ref,
                   wg_b, wu_b, wd_b, stg, sem, *, n_in, n_out, layer):
    b = pl.program_id(0)
    n_chunks = 2 * n_in + n_out
    ring = stg.shape[0]
    rows = EXPERT_CHUNK_ROWS

    parts = [(0, n_in, wg_ref, wg_b), (n_in, n_in, wu_ref, wu_b), (2 * n_in, n_out, wd_ref, wd_b)]

    def for_chunk(c, e, fn):
        s = c % ring
        for first, count, src, dst in parts:
            @pl.when((c >= first) & (c < first + count))
            def _():
                row = pl.multiple_of((c - first) * rows, rows)
                window = stg.at[s, :, pl.ds(0, dst.shape[2])]
                cp = pltpu.make_async_copy(src.at[layer, e, pl.ds(row, rows)], window, sem.at[s])
                fn(cp, dst, row, window)

    def start_chunk(c, e):
        for_chunk(c, e, lambda cp, dst, row, window: cp.start())

    def finish_chunk(c, e, dst_slot):
        def cast(cp, dst, row, window):
            cp.wait()
            dst[dst_slot, pl.ds(row, rows), :] = window[...].astype(BF16)
        for_chunk(c, e, cast)

    def load_chunks(e, dst_slot, lo, hi):
        @pl.when((lo == 0) & (hi > 0))
        def _():
            for k in range(min(ring, n_chunks)):
                start_chunk(k, e)

        def body(c, carry):
            finish_chunk(c, e, dst_slot)

            @pl.when(c + ring < n_chunks)
            def _():
                start_chunk(c + ring, e)
            return carry

        lax.fori_loop(lo, hi, body, 0)

    @pl.when(b == 0)
    def _():
        load_chunks(e0_ref[0], 0, 0, n_chunks)

    @pl.when(b < nu_ref[0])
    def _():
        slot = slot_ref[b]
        load_chunks(ne_ref[b], 1 - slot, c0_ref[b], c1_ref[b])
        d = wg_b.shape[1]
        n_chunk = d // LANES
        rt, _ = _token_tile(d)
        eb = xp_ref.shape[1] // rt
        x = jnp.concatenate([xp_ref[_tile_chunk(c, eb, rt)].astype(BF16) for c in range(n_chunk)], axis=1)
        g = jnp.dot(x, wg_b[slot], preferred_element_type=F32)
        u = jnp.dot(x, wu_b[slot], preferred_element_type=F32)
        a = (_silu(g) * u).astype(BF16)
        y = jnp.dot(a, wd_b[slot], preferred_element_type=F32)
        for c in range(n_chunk):
            y_ref[_tile_chunk(c, eb, rt)] = y[:, c * LANES:(c + 1) * LANES]

    @pl.when(b >= nu_ref[0])
    def _():
        y_ref[...] = jnp.zeros(y_ref.shape, y_ref.dtype)


def _experts(cfg, x_rows, block_e, n_used, wg, wu, wd, layer, n_blocks):
    d = cfg.d_model
    f = cfg.d_expert
    eb = cfg.e_block
    rt, npl = _token_tile(d)
    n_in, n_out, _ = _expert_chunks(cfg)
    slot, next_e, c0, c1 = _weight_schedule(cfg, block_e, n_used, n_blocks)
    any_spec = pl.BlockSpec(memory_space=pl.ANY)
    x_idx = lambda b, nu, *_: (0, jnp.minimum(b, nu[0] - 1), 0)
    return pl.pallas_call(
        functools.partial(_expert_kernel, n_in=n_in, n_out=n_out, layer=layer),
        grid_spec=pltpu.PrefetchScalarGridSpec(
            num_scalar_prefetch=6,
            grid=(n_blocks,),
            in_specs=[pl.BlockSpec((npl, eb * rt, LANES), x_idx), any_spec, any_spec, any_spec],
            out_specs=pl.BlockSpec((npl, eb * rt, LANES), lambda b, *_: (0, b, 0)),
            scratch_shapes=[
                pltpu.VMEM((2, d, f), BF16), pltpu.VMEM((2, d, f), BF16), pltpu.VMEM((2, f, d), BF16),
                pltpu.VMEM((EXPERT_RING, EXPERT_CHUNK_ROWS, max(d, f)), F32),
                pltpu.SemaphoreType.DMA((EXPERT_RING,)),
            ],
        ),
        out_shape=jax.ShapeDtypeStruct((npl, n_blocks * eb * rt, LANES), F32),
        compiler_params=pltpu.CompilerParams(dimension_semantics=("arbitrary",),
                                             vmem_limit_bytes=EXPERT_VMEM_LIMIT),
        name="moe_experts",
    )(n_used, block_e[:1], slot, next_e, c0, c1, x_rows, wg, wu, wd)


def _combine_kernel(pos_ref, nxt_ref, y_ref, x_ref, mod_ref, wt_ref, fg_ref, o_ref, ybuf, sem, *, final, eps):
    i = pl.program_id(0)
    n_steps = pl.num_programs(0)
    rows, d = x_ref.shape
    n_chunk = d // LANES
    rt, _ = _token_tile(d)
    slot = i % 2
    per_choice = rows * rt

    def gather(p_ref, s):
        def start(r, c):
            for k in range(2):
                src = _tile_tokens(y_ref, pl.multiple_of(p_ref[0, k, r] * rt, rt), rt)
                dst = _tile_tokens(ybuf.at[s], pl.multiple_of(k * per_choice + r * rt, rt), rt)
                pltpu.make_async_copy(src, dst, sem.at[s]).start(priority=k)
            return c
        lax.fori_loop(0, rows, start, 0, unroll=8)

    @pl.when(i == 0)
    def _():
        gather(pos_ref, 0)

    @pl.when(i + 1 < n_steps)
    def _():
        gather(nxt_ref, 1 - slot)

    pltpu.make_async_copy(_tile_tokens(y_ref, 0, 2 * per_choice), ybuf.at[slot], sem.at[slot]).wait()

    w = wt_ref[...]
    w0, w1 = w[:, 0:1], w[:, 1:2]
    sumsq = jnp.zeros((rows, 1), F32)
    for c in range(n_chunk):
        cols = slice(c * LANES, (c + 1) * LANES)
        moe = (w0 * ybuf[(slot,) + _tile_chunk(c, rows, rt)]
               + w1 * ybuf[(slot,) + _tile_chunk(c, rows, rt, first=per_choice)])
        xc = x_ref[:, cols] + mod_ref[0, 5:6, cols] * moe
        o_ref[:, cols] = xc
        sumsq = sumsq + jnp.sum(xc * xc, axis=-1, keepdims=True)
    if final:
        o_ref[...] = (o_ref[...] * lax.rsqrt(sumsq * (1.0 / d) + eps)) * fg_ref[...]


def _combine(cfg, y, pos_blocks, wts_t, x, mod_l, final_gain, *, row0, n_rows, final):
    d = cfg.d_model
    tb = pos_blocks.shape[2]
    rt, npl = _token_tile(d)
    b0 = row0 // tb
    mi = _merged_mod_index(cfg, tb)
    kern = functools.partial(_combine_kernel, final=final, eps=cfg.norm_eps)
    n_steps = n_rows // tb
    return pl.pallas_call(
        kern,
        grid=(n_steps,),
        in_specs=[
            pl.BlockSpec((1, 2, tb), lambda i: (b0 + i, 0, 0), memory_space=pltpu.SMEM),
            pl.BlockSpec((1, 2, tb), lambda i: (b0 + jnp.minimum(i + 1, n_steps - 1), 0, 0),
                         memory_space=pltpu.SMEM),
            pl.BlockSpec(memory_space=pl.ANY),
            pl.BlockSpec((tb, d), lambda i: (b0 + i, 0)),
            pl.BlockSpec((1, 8, d), lambda i: (mi(b0 + i), 0, 0)),
            pl.BlockSpec((tb, 2), lambda i: (b0 + i, 0)),
            pl.BlockSpec((1, d), lambda i: (0, 0)),
        ],
        out_specs=pl.BlockSpec((tb, d), lambda i: (i, 0)),
        out_shape=jax.ShapeDtypeStruct((n_rows, d), F32),
        scratch_shapes=[pltpu.VMEM((2, npl, 2 * tb * rt, LANES), F32), pltpu.SemaphoreType.DMA((2,))],
        compiler_params=_params(("arbitrary",)),
        name="moe_combine_final" if final else "moe_combine",
    )(pos_blocks, pos_blocks, y, x, mod_l, wts_t, final_gain)


def _moe(cfg, x, mod_l, gain, w_router_t, router_bias, wg, wu, wd, layer, final_gain, final):
    tb = min(256, cfg.tm)
    hp, idx, wts = _route(cfg, x, mod_l, gain, w_router_t, router_bias)
    pos, block_e, n_used, pad_end, n_blocks = _plan(cfg, idx)
    pos_blocks = pos.reshape(2, cfg.n_tok // tb, tb).transpose(1, 0, 2)
    td = cfg.tm
    x_rows = _dispatch(cfg, hp, pos.reshape(2, cfg.n_tok // td, td).transpose(1, 0, 2), pad_end,
                       n_blocks * cfg.e_block)
    y = _experts(cfg, x_rows, block_e, n_used, wg, wu, wd, layer, n_blocks)
    wts_t = wts.T
    comb = functools.partial(_combine, cfg, y, pos_blocks, wts_t, x, mod_l, final_gain, final=final)
    if final:
        return comb(row0=0, n_rows=cfg.n_p), comb(row0=cfg.n_p, n_rows=cfg.n_s)
    return comb(row0=0, n_rows=cfg.n_tok)


def _forward(cfg, x_prompt, x_sample, cache_a_k, cache_a_v, cache_b_k, cache_b_v, c, c_ctx, w_ada, b_ada,
             norm1, norm2, final_norm, w_in, w_mix_out, lambda_q1, lambda_k1, lambda_q2, lambda_k2,
             subln_gain, na_rel_bias, w_fourier_out, w_router, router_bias, w_exp_gate, w_exp_up,
             w_exp_down):
    d = cfg.d_model
    xp = x_prompt.reshape(cfg.n_p, d)
    xs = x_sample.reshape(cfg.n_s, d)

    n_cond = -(-cfg.n_mod // 8) * 8
    cond = jnp.concatenate([c_ctx[None, :], c, jnp.zeros((n_cond - cfg.n_mod, d), F32)], axis=0)
    mod = _modulation(cfg, cond, w_ada, b_ada)
    mod = mod.reshape(cfg.depth, n_cond, 6, d)[:, :cfg.n_mod]
    mod = jnp.pad(mod, ((0, 0), (0, 0), (0, 2), (0, 0)))

    w_router_t = w_router.T
    fgain = final_norm.reshape(1, d)
    x = (xp, xs)
    kv_cache = None
    for l in range(cfg.depth):
        j = l // 2
        g1 = norm1[l].reshape(1, d)
        if l % 2 == 0:
            lam_init = 0.8 - 0.6 * math.exp(-0.3 * l)
            lam_pack = jnp.zeros((8, LANES), F32).at[:4, :cfg.a_qk].set(
                jnp.stack([lambda_q1[j], lambda_k1[j], lambda_q2[j], lambda_k2[j]]))
            sgain = subln_gain[j].reshape(1, cfg.a_dim)
            w_in_b = w_in[j].astype(BF16)
            if isinstance(x, tuple):
                x_p, x_s = x
            else:
                x_p, x_s = x[:cfg.n_p], x[cfg.n_p:]
            proj_p = _projection(cfg, x_p, mod[l], g1, w_in_b, latent=False, out_dtype=F32)
            proj_s = _projection(cfg, x_s, mod[l], g1, w_in_b, latent=True, out_dtype=BF16)
            if kv_cache is None:
                kv_cache = []
            aw, bw = cfg.a_width, cfg.b_width
            cak = cache_a_k[:, j].reshape(cfg.dec_batch, cfg.past_len, aw)
            cav = cache_a_v[:, j].reshape(cfg.dec_batch, cfg.past_len, aw)
            cbk = cache_b_k[:, j].reshape(cfg.dec_batch, cfg.past_len, bw)
            cbv = cache_b_v[:, j].reshape(cfg.dec_batch, cfg.past_len, bw)
            a_p, new_ak, new_av = _diff_attention(cfg, proj_p, lam_pack, sgain, cfg.batch, cfg.seq, lam_init)
            b_p, new_bk, new_bv = _soft_attention(cfg, proj_p, cfg.batch, cfg.seq)
            kv_cache.append((new_ak, new_av, new_bk, new_bv))
            a_s = _diff_attention(cfg, proj_s, lam_pack, sgain, cfg.dec_batch, cfg.dec_seq, lam_init,
                                  ctx=(cak, cav))
            b_s = _na_attention(cfg, proj_s, cbk, cbv, na_rel_bias[j])
            x = _linear_residual(cfg, [(a_p, a_s), (b_p, b_s)], w_mix_out[j].astype(BF16), x, mod[l])
        else:
            if isinstance(x, tuple):
                x = jnp.concatenate(x, axis=0)
            y = _dft_channels(cfg, x, mod[l], g1)
            f_p = _dft_sequence(cfg, y, cfg.batch, cfg.seq, 0)
            f_s = _dft_sequence(cfg, y, cfg.dec_batch, cfg.dec_seq, cfg.n_p // cfg.dec_seq)
            x = _linear_residual(cfg, [(f_p, f_s)], w_fourier_out[j].astype(BF16), x, mod[l])
        x = _moe(cfg, x, mod[l], norm2[l].reshape(1, d), w_router_t, router_bias,
                 w_exp_gate, w_exp_up, w_exp_down, l, fgain, final=(l == cfg.depth - 1))
    y_p, y_s = x
    n_even = (cfg.depth + 1) // 2
    outs = [y_p.reshape(cfg.batch, cfg.seq, d), y_s.reshape(cfg.dec_batch, cfg.dec_seq, d)]
    for t in range(4):
        heads, hd = (cfg.a_heads, cfg.a_dim) if t < 2 else (cfg.b_heads, cfg.b_dim)
        stacked = jnp.stack([kv_cache[jj][t].reshape(cfg.batch, cfg.seq, heads, hd)
                             for jj in range(n_even)], axis=1)
        outs.append(stacked)
    return tuple(outs)


def kernel(x_prompt, x_sample, cache_a_k, cache_a_v, cache_b_k, cache_b_v, c, c_ctx, w_ada, b_ada, norm1, norm2, final_norm, w_in, w_mix_out, lambda_q1, lambda_k1, lambda_q2, lambda_k2, subln_gain, na_rel_bias, w_fourier_out, w_router, router_bias, w_exp_gate, w_exp_up, w_exp_down):
    return _forward(Cfg(), x_prompt, x_sample, cache_a_k, cache_a_v, cache_b_k, cache_b_v, c, c_ctx, w_ada,
                    b_ada, norm1, norm2, final_norm, w_in, w_mix_out, lambda_q1, lambda_k1, lambda_q2,
                    lambda_k2, subln_gain, na_rel_bias, w_fourier_out, w_router, router_bias, w_exp_gate,
                    w_exp_up, w_exp_down)
```
